```python
import math
import jax
import jax.numpy as jnp
from jax import lax
import numpy as np

D_MODEL = 1024
BATCH = 8
SEQ = 4096
DEPTH = 2

CTX_LEN = 256
GRID_W = 64
D_FF = 2816
CHUNK = 128
LN_EPS = 1e-5
N_MOD = 9

NA_HEADS = 8
NA_HEAD_DIM = 64
NA_WIN_H = 8
NA_WIN_W = 16
NA_WIDTH = NA_HEADS * NA_HEAD_DIM
ML_HEADS = 4
ML_HEAD_DIM = 128
ML_WIDTH = ML_HEADS * ML_HEAD_DIM
ML_CONV = 5
S5_GROUP = 16
S5_GROUPS = 32
S5_WIDTH = S5_GROUP * S5_GROUPS
S5_STATE = 64
RET_HEADS = 4
RET_HEAD_DIM = 128
RET_WIDTH = RET_HEADS * RET_HEAD_DIM
ROPE_BASE = 10000.0

EV_SIZES = (NA_WIDTH, NA_WIDTH, NA_WIDTH, ML_WIDTH, ML_WIDTH, ML_WIDTH, 2 * ML_HEADS, 2 * ML_HEADS)
OD_SIZES = (S5_WIDTH, RET_WIDTH, RET_WIDTH, RET_WIDTH, RET_WIDTH)
EV_IN = sum(EV_SIZES)
OD_IN = sum(OD_SIZES)
MIX_WIDTH = NA_WIDTH + ML_WIDTH

F32 = jnp.float32

kernel_name = 'hybrid_dit_natten_mlstm_s5_retention'


def layer_norm(h, g, b):
    hf = h.astype(F32)
    mu = jnp.mean(hf, -1, keepdims=True)
    var = jnp.mean(jnp.square(hf - mu), -1, keepdims=True)
    return ((hf - mu) * lax.rsqrt(var + LN_EPS) * g.astype(F32) + b.astype(F32)).astype(h.dtype)


def modulate(h, mod, j):
    return h * (1.0 + mod[..., 3 * j + 1, :, :]) + mod[..., 3 * j, :, :]


def gate(mod, j):
    return mod[..., 3 * j + 2, :, :]


def swiglu(h, w_gate, w_up, w_down):
    return (jax.nn.silu(h @ w_gate) * (h @ w_up)) @ w_down


def half_ffn(h, mod, j, w, g, b, alpha):
    y = swiglu(modulate(h, mod, j), *w)
    return layer_norm(alpha * h + 0.5 * gate(mod, j) * y, g, b)


def split_cols(z, sizes):
    parts, off = [], 0
    for s in sizes:
        parts.append(z[..., off:off + s])
        off += s
    return parts


def head_norm(h, w):
    bsz, nh, t, d = h.shape
    mu = jnp.mean(h, -1, keepdims=True)
    var = jnp.mean(jnp.square(h - mu), -1, keepdims=True)
    hn = (h - mu) * lax.rsqrt(var + LN_EPS)
    return hn.transpose(0, 2, 1, 3).reshape(bsz, t, nh * d) * w.astype(F32)


def to_chunks(a, axis):
    t = a.shape[axis]
    a = a.reshape(a.shape[:axis] + (t // CHUNK, CHUNK) + a.shape[axis + 1:])
    return jnp.moveaxis(a, axis, 0)


def from_chunks(a, axis):
    a = jnp.moveaxis(a, 0, axis)
    return a.reshape(a.shape[:axis] + (a.shape[axis] * a.shape[axis + 1],) + a.shape[axis + 2:])


def bidir_scan(scan_fn, ctx_xs, lat_xs, dir_params, init_state, axis):
    ys_c, ys_l = [], []
    for d in range(2):
        flip = (lambda a: jnp.flip(a, axis)) if d == 1 else (lambda a: a)
        y_c, state = scan_fn(*[flip(a) for a in ctx_xs[d]], *dir_params[d], init_state)
        y_l, _ = scan_fn(*[flip(a) for a in lat_xs[d]], *dir_params[d], state)
        ys_c.append(flip(y_c))
        ys_l.append(flip(y_l))
    return ys_c[0] + ys_c[1], ys_l[0] + ys_l[1]


def context_attention(q, k, v):
    s = jnp.einsum('bqhd,bkhd->bhqk', q, k) * (q.shape[-1] ** -0.5)
    p = jax.nn.softmax(s.astype(F32), axis=-1).astype(v.dtype)
    o = jnp.einsum('bhqk,bkhd->bqhd', p, v)
    return o.reshape(o.shape[0], o.shape[1], -1)


def neighbourhood_attention(q, k, v, k_ctx, v_ctx, rpb):
    bsz, seq, nh, dh = q.shape
    rows = seq // GRID_W
    kh, kw = min(NA_WIN_H, rows), NA_WIN_W
    n_loc = kh * kw
    grid = lambda a: a.reshape(bsz, rows, GRID_W, nh, dh).transpose(0, 3, 1, 2, 4)
    kg, vg = grid(k), grid(v)
    q_rows = q.reshape(bsz, rows, GRID_W, nh, dh).transpose(1, 0, 3, 2, 4) * (dh ** -0.5)
    kc, vc = k_ctx.transpose(0, 2, 1, 3), v_ctx.transpose(0, 2, 1, 3)
    cols = jnp.arange(GRID_W)
    col_idx = jnp.clip(cols - kw // 2, 0, GRID_W - kw)[:, None] + jnp.arange(kw)[None, :]
    col_off = col_idx - cols[:, None] + (NA_WIN_W - 1)

    def one_row(args):
        qr, r = args
        rs = jnp.clip(r - kh // 2, 0, rows - kh)
        kwin = jnp.take(lax.dynamic_slice_in_dim(kg, rs, kh, axis=2), col_idx, axis=3)
        vwin = jnp.take(lax.dynamic_slice_in_dim(vg, rs, kh, axis=2), col_idx, axis=3)
        row_off = rs + jnp.arange(kh) - r + (NA_WIN_H - 1)
        bias = rpb[:, row_off[:, None, None], col_off[None, :, :]].transpose(0, 2, 1, 3)
        s_loc = jnp.einsum('bhqd,bhrqwd->bhqrw', qr, kwin) + bias
        s_ctx = jnp.einsum('bhqd,bhcd->bhqc', qr, kc)
        logits = jnp.concatenate([s_loc.reshape(bsz, nh, GRID_W, n_loc), s_ctx], -1).astype(F32)
        p = jax.nn.softmax(logits, axis=-1).astype(v.dtype)
        p_loc = p[..., :n_loc].reshape(bsz, nh, GRID_W, kh, kw)
        return (jnp.einsum('bhqrw,bhrqwd->bhqd', p_loc, vwin)
                + jnp.einsum('bhqc,bhcd->bhqd', p[..., n_loc:], vc))

    out = lax.map(one_row, (q_rows, jnp.arange(rows)))
    return out.transpose(1, 0, 3, 2, 4).reshape(bsz, seq, nh * dh)


def mlstm_scan(q, k, v, i_pre, log_f, state):
    tri = jnp.tril(jnp.ones((CHUNK, CHUNK), dtype=bool))

    def step(carry, xs):
        c_mat, n_vec, m = carry
        qc, kc, vc, ic, fc = xs
        b = jnp.cumsum(fc, axis=-1)
        dlog = jnp.where(tri, b[..., :, None] - b[..., None, :] + ic[..., None, :], -jnp.inf)
        inter = b + m[..., None]
        m_t = jnp.maximum(inter, jnp.max(dlog, axis=-1))
        s = jnp.einsum('bhtd,bhsd->bhts', qc, kc) * jnp.exp(dlog - m_t[..., None])
        w_inter = jnp.exp(inter - m_t)
        num = (jnp.einsum('bhts,bhsd->bhtd', s, vc)
               + w_inter[..., None] * jnp.einsum('bhvk,bhtk->bhtv', c_mat, qc))
        den = jnp.sum(s, -1) + w_inter * jnp.einsum('bhk,bhtk->bht', n_vec, qc)
        h = num / jnp.maximum(jnp.abs(den), jnp.exp(-m_t))[..., None]
        b_last = b[..., -1]
        w_log = b_last[..., None] - b + ic
        m_new = jnp.maximum(b_last + m, jnp.max(w_log, -1))
        carry_decay = jnp.exp(b_last + m - m_new)
        w_in = jnp.exp(w_log - m_new[..., None])
        c_new = carry_decay[..., None, None] * c_mat + jnp.einsum('bhsv,bhsk->bhvk', vc * w_in[..., None], kc)
        n_new = carry_decay[..., None] * n_vec + jnp.einsum('bhs,bhsk->bhk', w_in, kc)
        return (c_new, n_new, m_new), h

    xs = tuple(to_chunks(a, 2) for a in (q, k, v, i_pre, log_f))
    state, hs = lax.scan(step, state, xs)
    return from_chunks(hs, 2), state


def centred_depthwise_conv(x, w, b):
    y = lax.conv_general_dilated(x, w[:, None, :], window_strides=(1,), padding='SAME',
                                 dimension_numbers=('NWC', 'WIO', 'NWC'), feature_group_count=x.shape[-1])
    return y + b


def mlstm_prepare(parts, conv_w, conv_b, wq, wk, i_bias, f_bias):
    xm, zv, zo, zi, zf = parts
    bsz, t, _ = xm.shape
    xc = jax.nn.silu(centred_depthwise_conv(xm, conv_w, conv_b)).astype(F32)
    xc = xc.reshape(bsz, t, ML_HEADS, ML_HEAD_DIM)
    q = jnp.einsum('bthd,hde->bhte', xc, wq.astype(F32))
    k = jnp.einsum('bthd,hde->bhte', xc, wk.astype(F32)) * (ML_HEAD_DIM ** -0.5)
    v = zv.astype(F32).reshape(bsz, t, ML_HEADS, ML_HEAD_DIM).transpose(0, 2, 1, 3)
    i_pre = (zi.astype(F32).reshape(bsz, t, 2, ML_HEADS) + i_bias.astype(F32)).transpose(2, 0, 3, 1)
    log_f = jax.nn.log_sigmoid(zf.astype(F32).reshape(bsz, t, 2, ML_HEADS)
                               + f_bias.astype(F32)).transpose(2, 0, 3, 1)
    xs = [(q, k, v, i_pre[d], log_f[d]) for d in range(2)]
    return xs, jax.nn.sigmoid(zo.astype(F32))


def even_mixer(zc, zl, rpb, conv_w, conv_b, wq, wk, i_bias, f_bias, gn_w):
    pc, pl = split_cols(zc, EV_SIZES), split_cols(zl, EV_SIZES)
    heads = lambda a: a.reshape(a.shape[0], a.shape[1], NA_HEADS, NA_HEAD_DIM)
    na_c = context_attention(heads(pc[0]), heads(pc[1]), heads(pc[2]))
    na_l = neighbourhood_attention(heads(pl[0]), heads(pl[1]), heads(pl[2]),
                                   heads(pc[1]), heads(pc[2]), rpb)
    xs_c, o_c = mlstm_prepare(pc[3:], conv_w, conv_b, wq, wk, i_bias, f_bias)
    xs_l, o_l = mlstm_prepare(pl[3:], conv_w, conv_b, wq, wk, i_bias, f_bias)
    bsz = zl.shape[0]
    init = (jnp.zeros((bsz, ML_HEADS, ML_HEAD_DIM, ML_HEAD_DIM), F32),
            jnp.zeros((bsz, ML_HEADS, ML_HEAD_DIM), F32),
            jnp.zeros((bsz, ML_HEADS), F32))
    h_c, h_l = bidir_scan(mlstm_scan, xs_c, xs_l, ((), ()), init, 2)
    ml_c = (o_c * head_norm(h_c, gn_w)).astype(zc.dtype)
    ml_l = (o_l * head_norm(h_l, gn_w)).astype(zl.dtype)
    return jnp.concatenate([na_c, ml_c], -1), jnp.concatenate([na_l, ml_l], -1)


def s5_scan(u, lam_re, lam_im, log_dt, b_re, b_im, c_re, c_im, state):
    dt = jnp.exp(log_dt)[:, None]
    zr, zi = lam_re * dt, lam_im * dt
    mag = jnp.exp(zr)
    a_re, a_im = mag * jnp.cos(zi), mag * jnp.sin(zi)
    lam_sq = jnp.square(lam_re) + jnp.square(lam_im)
    e_re = ((a_re - 1.0) * lam_re + a_im * lam_im) / lam_sq
    e_im = (a_im * lam_re - (a_re - 1.0) * lam_im) / lam_sq
    bb_re = e_re[..., None] * b_re - e_im[..., None] * b_im
    bb_im = e_re[..., None] * b_im + e_im[..., None] * b_re
    steps = jnp.arange(1, CHUNK + 1, dtype=F32)[:, None, None]
    pmag = jnp.exp(steps * zr)
    p_re, p_im = pmag * jnp.cos(steps * zi), pmag * jnp.sin(steps * zi)

    def combine(e1, e2):
        a1r, a1i, b1r, b1i = e1
        a2r, a2i, b2r, b2i = e2
        return (a2r * a1r - a2i * a1i, a2r * a1i + a2i * a1r,
                a2r * b1r - a2i * b1i + b2r, a2r * b1i + a2i * b1r + b2i)

    def step(carry, uc):
        x0r, x0i = carry
        bu_re = jnp.einsum('blgq,gpq->blgp', uc, bb_re)
        bu_im = jnp.einsum('blgq,gpq->blgp', uc, bb_im)
        _, _, xr, xi = lax.associative_scan(
            combine, (jnp.broadcast_to(a_re, bu_re.shape), jnp.broadcast_to(a_im, bu_im.shape), bu_re, bu_im),
            axis=1)
        xr, xi = (xr + p_re * x0r[:, None] - p_im * x0i[:, None],
                  xi + p_re * x0i[:, None] + p_im * x0r[:, None])
        y = jnp.einsum('blgp,gqp->blgq', xr, c_re) - jnp.einsum('blgp,gqp->blgq', xi, c_im)
        return (xr[:, -1], xi[:, -1]), y

    state, ys = lax.scan(step, state, to_chunks(u, 1))
    return from_chunks(ys, 1), state


def retention_scan(q, k, v, log_g, state):
    j = jnp.arange(CHUNK, dtype=F32)
    rel = j[:, None] - j[None, :]
    dmat = jnp.where(rel >= 0, jnp.exp(jnp.maximum(rel, 0.0) * log_g[:, None, None]), 0.0)
    q_dec = jnp.exp((j + 1.0) * log_g[:, None])[..., None]
    k_dec = jnp.exp((CHUNK - 1.0 - j) * log_g[:, None])[..., None]
    c_dec = jnp.exp(CHUNK * log_g)[:, None, None]

    def step(r, xs):
        qc, kc, vc = xs
        inner = jnp.einsum('bhts,bhsv->bhtv', jnp.einsum('bhtd,bhsd->bhts', qc, kc) * dmat, vc)
        cross = jnp.einsum('bhtd,bhdv->bhtv', qc * q_dec, r)
        r_new = c_dec * r + jnp.einsum('bhsd,bhsv->bhdv', kc * k_dec, vc)
        return r_new, inner + cross

    xs = tuple(to_chunks(a, 2) for a in (q, k, v))
    state, ys = lax.scan(step, state, xs)
    return from_chunks(ys, 2), state


def axial_rope(x, pos_row, pos_col):
    d = x.shape[-1]
    half, nf = d // 2, d // 4
    freqs = ROPE_BASE ** (-jnp.arange(nf, dtype=F32) / nf)

    def rotate(xp, pos):
        ang = pos.astype(F32)[:, None] * freqs
        cos, sin = jnp.cos(ang)[None, :, None, :], jnp.sin(ang)[None, :, None, :]
        x1, x2 = xp[..., :nf], xp[..., nf:]
        return jnp.concatenate([x1 * cos - x2 * sin, x2 * cos + x1 * sin], -1)

    return jnp.concatenate([rotate(x[..., :half], pos_row), rotate(x[..., half:], pos_col)], -1)


def odd_mixer(zc, zl, pos_row, pos_col, lam_re, lam_im, log_dt, b_re, b_im, c_re, c_im,
              d_skip, glu_w, glu_b, decay_logit, gn_w):
    pc, pl = split_cols(zc, OD_SIZES), split_cols(zl, OD_SIZES)
    bsz = zl.shape[0]
    groups = lambda a: a.astype(F32).reshape(a.shape[0], a.shape[1], S5_GROUPS, S5_GROUP)
    uc, ul = groups(pc[0]), groups(pl[0])
    s5_p = [tuple(p[d].astype(F32) for p in (lam_re, lam_im, log_dt, b_re, b_im, c_re, c_im)) for d in range(2)]
    s5_init = (jnp.zeros((bsz, S5_GROUPS, S5_STATE), F32), jnp.zeros((bsz, S5_GROUPS, S5_STATE), F32))
    yc, yl = bidir_scan(s5_scan, [(uc,), (uc,)], [(ul,), (ul,)], s5_p, s5_init, 1)

    def s5_out(y, u):
        y = jax.nn.gelu(y + d_skip.astype(F32) * u)
        y = y.reshape(y.shape[0], y.shape[1], S5_WIDTH)
        return y * jax.nn.sigmoid(y @ glu_w.astype(F32) + glu_b.astype(F32))

    heads = lambda a: a.astype(F32).reshape(a.shape[0], a.shape[1], RET_HEADS, RET_HEAD_DIM)
    bhtd = lambda a: a.transpose(0, 2, 1, 3)
    k_scale = RET_HEAD_DIM ** -0.5
    rc_xs = (bhtd(heads(pc[1])), bhtd(heads(pc[2])) * k_scale, bhtd(heads(pc[3])))
    rl_xs = (bhtd(axial_rope(heads(pl[1]), pos_row, pos_col)),
             bhtd(axial_rope(heads(pl[2]), pos_row, pos_col)) * k_scale,
             bhtd(heads(pl[3])))
    log_g = jax.nn.log_sigmoid(decay_logit.astype(F32))
    ret_init = jnp.zeros((bsz, RET_HEADS, RET_HEAD_DIM, RET_HEAD_DIM), F32)
    rc, rl = bidir_scan(retention_scan, [rc_xs, rc_xs], [rl_xs, rl_xs], [(log_g[0],), (log_g[1],)], ret_init, 2)
    ret_c = jax.nn.silu(pc[4].astype(F32)) * head_norm(rc, gn_w)
    ret_l = jax.nn.silu(pl[4].astype(F32)) * head_norm(rl, gn_w)
    out_c = jnp.concatenate([s5_out(yc, uc), ret_c], -1).astype(zc.dtype)
    out_l = jnp.concatenate([s5_out(yl, ul), ret_l], -1).astype(zl.dtype)
    return out_c, out_l


def setup_inputs(seed: int = 0) -> dict:
    key = jax.random.key(seed)
    ks = jax.random.split(key, 40)
    n_ev, n_od = (DEPTH + 1) // 2, DEPTH // 2
    beta = (8.0 * DEPTH) ** -0.25
    D = D_MODEL
    nrm = lambda i, shape, s: jax.random.normal(ks[i], shape, F32) * s
    ones_n = lambda i, shape: 1.0 + nrm(i, shape, 0.02)
    return {
        'x': nrm(0, (BATCH, SEQ, D), 1.0),
        'c': nrm(1, (BATCH, D), 1.0),
        'ctx': nrm(2, (BATCH, CTX_LEN, D), 1.0),
        'c_ctx': nrm(3, (D,), 1.0),
        'ada_w': nrm(4, (DEPTH, D, N_MOD * D), 0.5 * D ** -0.5),
        'ada_b': nrm(5, (DEPTH, N_MOD * D), 0.02),
        'ffn_w_gate': nrm(6, (DEPTH, 2, D, D_FF), D ** -0.5),
        'ffn_w_up': nrm(7, (DEPTH, 2, D, D_FF), D ** -0.5),
        'ffn_w_down': nrm(8, (DEPTH, 2, D_FF, D), beta * D_FF ** -0.5),
        'ln_g': ones_n(9, (DEPTH, 3, D)),
        'ln_b': nrm(10, (DEPTH, 3, D), 0.02),
        'ev_w_in': nrm(11, (n_ev, D, EV_IN), D ** -0.5),
        'ev_w_out': nrm(12, (n_ev, MIX_WIDTH, D), beta * MIX_WIDTH ** -0.5),
        'na_rpb': nrm(13, (n_ev, NA_HEADS, 2 * NA_WIN_H - 1, 2 * NA_WIN_W - 1), 0.1),
        'ml_conv_w': nrm(14, (n_ev, ML_CONV, ML_WIDTH), ML_CONV ** -0.5),
        'ml_conv_b': nrm(15, (n_ev, ML_WIDTH), 0.02),
        'ml_wq': nrm(16, (n_ev, ML_HEADS, ML_HEAD_DIM, ML_HEAD_DIM), ML_HEAD_DIM ** -0.5),
        'ml_wk': nrm(17, (n_ev, ML_HEADS, ML_HEAD_DIM, ML_HEAD_DIM), ML_HEAD_DIM ** -0.5),
        'ml_i_bias': nrm(18, (n_ev, 2, ML_HEADS), 0.1),
        'ml_f_bias': jnp.linspace(3.0, 6.0, ML_HEADS, dtype=F32) + nrm(19, (n_ev, 2, ML_HEADS), 0.1),
        'ml_gn_w': ones_n(20, (n_ev, ML_WIDTH)),
        'od_w_in': nrm(21, (n_od, D, OD_IN), D ** -0.5),
        'od_w_out': nrm(22, (n_od, MIX_WIDTH, D), beta * MIX_WIDTH ** -0.5),
        's5_lam_re': -0.5 + nrm(23, (n_od, 2, S5_GROUPS, S5_STATE), 0.01),
        's5_lam_im': math.pi * jnp.arange(S5_STATE, dtype=F32) + nrm(24, (n_od, 2, S5_GROUPS, S5_STATE), 0.01),
        's5_log_dt': jax.random.uniform(ks[25], (n_od, 2, S5_GROUPS), F32, math.log(1e-3), math.log(1e-1)),
        's5_b_re': nrm(26, (n_od, 2, S5_GROUPS, S5_STATE, S5_GROUP), (2.0 * S5_GROUP) ** -0.5),
        's5_b_im': nrm(27, (n_od, 2, S5_GROUPS, S5_STATE, S5_GROUP), (2.0 * S5_GROUP) ** -0.5),
        's5_c_re': nrm(28, (n_od, 2, S5_GROUPS, S5_GROUP, S5_STATE), 0.5),
        's5_c_im': nrm(29, (n_od, 2, S5_GROUPS, S5_GROUP, S5_STATE), 0.5),
        's5_d': nrm(30, (n_od, S5_GROUPS, S5_GROUP), 0.5),
        's5_glu_w': nrm(31, (n_od, S5_WIDTH, S5_WIDTH), S5_WIDTH ** -0.5),
        's5_glu_b': nrm(32, (n_od, S5_WIDTH), 0.02),
        'ret_decay_logit': (jnp.log(2.0 ** (5.0 + jnp.arange(RET_HEADS, dtype=F32)) - 1.0)
                            + nrm(33, (n_od, 2, RET_HEADS), 0.01)),
        'ret_gn_w': ones_n(34, (n_od, RET_WIDTH)),
    }


def reference(x, c, ctx, c_ctx, ada_w, ada_b, ffn_w_gate, ffn_w_up, ffn_w_down, ln_g, ln_b,
              ev_w_in, ev_w_out, na_rpb, ml_conv_w, ml_conv_b, ml_wq, ml_wk, ml_i_bias, ml_f_bias, ml_gn_w,
              od_w_in, od_w_out, s5_lam_re, s5_lam_im, s5_log_dt, s5_b_re, s5_b_im, s5_c_re, s5_c_im,
              s5_d, s5_glu_w, s5_glu_b, ret_decay_logit, ret_gn_w):
    bsz, seq, dm = x.shape
    t = jnp.arange(seq)
    pos_row, pos_col = t // GRID_W, t % GRID_W
    alpha = (2.0 * DEPTH) ** 0.25
    silu_c, silu_cc = jax.nn.silu(c), jax.nn.silu(c_ctx)
    cx = ctx
    for l in range(DEPTH):
        e = l // 2
        last = l == DEPTH - 1
        mod_l = (silu_c @ ada_w[l] + ada_b[l]).reshape(bsz, N_MOD, 1, dm)
        mod_c = (silu_cc @ ada_w[l] + ada_b[l]).reshape(N_MOD, 1, dm)
        ffn_a = (ffn_w_gate[l, 0], ffn_w_up[l, 0], ffn_w_down[l, 0])
        ffn_b = (ffn_w_gate[l, 1], ffn_w_up[l, 1], ffn_w_down[l, 1])
        x = half_ffn(x, mod_l, 0, ffn_a, ln_g[l, 0], ln_b[l, 0], alpha)
        cx = half_ffn(cx, mod_c, 0, ffn_a, ln_g[l, 0], ln_b[l, 0], alpha)
        hl, hc = modulate(x, mod_l, 1), modulate(cx, mod_c, 1)
        if l % 2 == 0:
            yc, yl = even_mixer(hc @ ev_w_in[e], hl @ ev_w_in[e], na_rpb[e], ml_conv_w[e], ml_conv_b[e],
                                ml_wq[e], ml_wk[e], ml_i_bias[e], ml_f_bias[e], ml_gn_w[e])
            w_out = ev_w_out[e]
        else:
            yc, yl = odd_mixer(hc @ od_w_in[e], hl @ od_w_in[e], pos_row, pos_col,
                               s5_lam_re[e], s5_lam_im[e], s5_log_dt[e], s5_b_re[e], s5_b_im[e],
                               s5_c_re[e], s5_c_im[e], s5_d[e], s5_glu_w[e], s5_glu_b[e],
                               ret_decay_logit[e], ret_gn_w[e])
            w_out = od_w_out[e]
        x = layer_norm(alpha * x + gate(mod_l, 1) * (yl.astype(x.dtype) @ w_out), ln_g[l, 1], ln_b[l, 1])
        x = half_ffn(x, mod_l, 2, ffn_b, ln_g[l, 2], ln_b[l, 2], alpha)
        if not last:
            cx = layer_norm(alpha * cx + gate(mod_c, 1) * (yc.astype(cx.dtype) @ w_out), ln_g[l, 1], ln_b[l, 1])
            cx = half_ffn(cx, mod_c, 2, ffn_b, ln_g[l, 2], ln_b[l, 2], alpha)
    return x
```

```python
import functools
import math

import jax
import jax.numpy as jnp
from jax import lax
from jax.experimental import pallas as pl
from jax.experimental.pallas import tpu as pltpu

F32 = jnp.float32
BF16 = jnp.bfloat16

GRID_W = 64
LN_EPS = 1e-5
N_MOD = 9
NA_HEADS, NA_HEAD_DIM, NA_WIN_H, NA_WIN_W = 8, 64, 8, 16
ML_HEADS, ML_HEAD_DIM, ML_CONV = 4, 128, 5
S5_GROUP, S5_GROUPS, S5_STATE = 16, 32, 64
RET_HEADS, RET_HEAD_DIM = 4, 128
ROPE_BASE = 10000.0
HALF = 512

LANES = 128
SUBLANES = 8
VMEM_LIMIT = 56 * 1024 * 1024

TM = 256
CHUNK = 128
S5_SUB = 16
NEG = -1e30


def _cparams(sem):
    return pltpu.CompilerParams(dimension_semantics=sem, vmem_limit_bytes=VMEM_LIMIT)


def _const_spec(shape):
    nd = len(shape)
    return pl.BlockSpec(shape, lambda *_: (0,) * nd, pipeline_mode=pl.Buffered(1))


def _dot(a, b):
    return jnp.dot(a, b, preferred_element_type=F32)


def _dot_nt(a, b):
    return lax.dot_general(a, b, (((1,), (1,)), ((), ())), preferred_element_type=F32)


def _dot_tn(a, b):
    return lax.dot_general(a, b, (((0,), (0,)), ((), ())), preferred_element_type=F32)


def _sigmoid(x):
    return 1.0 / (1.0 + jnp.exp(-x))


def _silu(x):
    return x * _sigmoid(x)


def _log_sigmoid(x):
    return jnp.minimum(x, 0.0) - jnp.log1p(jnp.exp(-jnp.abs(x)))


def _gelu_tanh(x):
    return 0.5 * x * (1.0 + jnp.tanh(math.sqrt(2.0 / math.pi) * (x + 0.044715 * (x * x * x))))


def _layer_norm(r, g, b):
    mu = jnp.mean(r, axis=-1, keepdims=True)
    c = r - mu
    var = jnp.mean(c * c, axis=-1, keepdims=True)
    return c * lax.rsqrt(var + LN_EPS) * g + b


def _head_norm(h, w, n_heads, head_dim):
    parts = []
    for k in range(n_heads):
        hk = h[:, k * head_dim:(k + 1) * head_dim]
        mu = jnp.mean(hk, axis=-1, keepdims=True)
        c = hk - mu
        var = jnp.mean(c * c, axis=-1, keepdims=True)
        parts.append(c * lax.rsqrt(var + LN_EPS))
    return jnp.concatenate(parts, axis=-1) * w


def _mod_row(mod_ref, j, k):
    i = 3 * j + k
    return mod_ref[i:i + 1, :]


def _mod_kernel(c_ref, w_ref, b_ref, o_ref):
    c = c_ref[...]
    s = _silu(c).astype(BF16)
    o_ref[...] = _dot(s, w_ref[...].astype(BF16)) + b_ref[...]


def _mod_table(c_rows, ada_w, ada_b):
    depth, d, n = ada_w.shape
    rows = c_rows.shape[0]
    tn = 1024
    out = pl.pallas_call(
        _mod_kernel,
        grid=(depth, n // tn),
        in_specs=[pl.BlockSpec((rows, d), lambda l, j: (0, 0)),
                  pl.BlockSpec((None, d, tn), lambda l, j: (l, 0, j)),
                  pl.BlockSpec((None, 1, tn), lambda l, j: (l, 0, j))],
        out_specs=pl.BlockSpec((None, rows, tn), lambda l, j: (l, 0, j)),
        out_shape=jax.ShapeDtypeStruct((depth, rows, n), F32),
        compiler_params=_cparams(("parallel", "parallel")),
        name="mod_table",
    )(c_rows, ada_w, ada_b.reshape(depth, 1, n))
    return out.reshape(depth, rows, N_MOD, d)


def _tok_spec(width, tm=TM):
    return pl.BlockSpec((None, tm, width), lambda b, t: (b, t, 0))


def _mod_block(d, batch, n_ctx_tiles):
    return pl.BlockSpec((None, N_MOD, d), lambda b, t: (jnp.where(t < n_ctx_tiles, batch, b), 0, 0))


def _ffn_kernel(x_ref, mod_ref, wg_ref, wu_ref, wd_ref, lng_ref, lnb_ref, o_ref, *, j, alpha):
    x = x_ref[...]
    h = (x * (1.0 + _mod_row(mod_ref, j, 1)) + _mod_row(mod_ref, j, 0)).astype(BF16)
    g = _dot(h, wg_ref[...])
    u = _dot(h, wu_ref[...])
    a = (_silu(g) * u).astype(BF16)
    y = _dot(a, wd_ref[...])
    r = alpha * x + (0.5 * _mod_row(mod_ref, j, 2)) * y
    o_ref[...] = _layer_norm(r, lng_ref[...], lnb_ref[...])


def _ffn(xa, mod, j, wg, wu, wd, lng, lnb, alpha, n_ctx_tiles):
    bsz, t, d = xa.shape
    dff = wg.shape[1]
    return pl.pallas_call(
        functools.partial(_ffn_kernel, j=j, alpha=alpha),
        grid=(bsz, t // TM),
        in_specs=[_tok_spec(d), _mod_block(d, bsz, n_ctx_tiles),
                  _const_spec((d, dff)), _const_spec((d, dff)), _const_spec((dff, d)),
                  _const_spec((1, d)), _const_spec((1, d))],
        out_specs=_tok_spec(d),
        out_shape=jax.ShapeDtypeStruct((bsz, t, d), F32),
        compiler_params=_cparams(("parallel", "parallel")),
        name="ffn",
    )(xa, mod, wg, wu, wd, lng.reshape(1, d), lnb.reshape(1, d))


def _rope(z, cos, sin):
    lane = lax.broadcasted_iota(jnp.int32, (1, LANES), 1)
    first = (lane % 64) < 32
    parts = []
    for k in range(z.shape[1] // LANES):
        zk = z[:, k * LANES:(k + 1) * LANES]
        partner = jnp.where(first, pltpu.roll(zk, LANES - 32, axis=1), pltpu.roll(zk, 32, axis=1))
        parts.append(zk * cos + partner * sin)
    return jnp.concatenate(parts, axis=-1)


def _inproj_kernel(*refs, outs, use_rope):
    x_ref, mod_ref, w_ref = refs[:3]
    k = 3
    if use_rope:
        cos_ref, sin_ref = refs[3:5]
        k = 5
    out_refs = refs[k:]
    h = (x_ref[...] * (1.0 + _mod_row(mod_ref, 1, 1)) + _mod_row(mod_ref, 1, 0)).astype(BF16)
    for (off, width, scale, rope), o_ref in zip(outs, out_refs):
        z = _dot(h, w_ref[:, off:off + width])
        if rope:
            z = _rope(z, cos_ref[...], sin_ref[...])
        if scale != 1.0:
            z = z * scale
        o_ref[...] = z.astype(o_ref.dtype)


def _inproj(xa, mod, w, outs, dtypes, n_ctx_tiles, rope_tabs=None):
    bsz, t, d = xa.shape
    in_specs = [_tok_spec(d), _mod_block(d, bsz, n_ctx_tiles), _const_spec(w.shape)]
    args = [xa, mod, w]
    if rope_tabs is not None:
        in_specs += [pl.BlockSpec((TM, LANES), lambda b, t: (t, 0))] * 2
        args += list(rope_tabs)
    return pl.pallas_call(
        functools.partial(_inproj_kernel, outs=tuple(outs), use_rope=rope_tabs is not None),
        grid=(bsz, t // TM),
        in_specs=in_specs,
        out_specs=[_tok_spec(o[1]) for o in outs],
        out_shape=[jax.ShapeDtypeStruct((bsz, t, o[1]), dt) for o, dt in zip(outs, dtypes)],
        compiler_params=_cparams(("parallel", "parallel")),
        name="inproj",
    )(*args)


def _softmax_pv(s_list, v_list):
    m = s_list[0].max(axis=-1, keepdims=True)
    for s in s_list[1:]:
        m = jnp.maximum(m, s.max(axis=-1, keepdims=True))
    acc, den = None, None
    for s, v in zip(s_list, v_list):
        p = jnp.exp(s - m)
        l = p.sum(axis=-1, keepdims=True)
        o = _dot(p.astype(BF16), v)
        acc = o if acc is None else acc + o
        den = l if den is None else den + l
    return acc / den


def _na_kernel(q_ref, k_ref, v_ref, bias_ref, o_ref, *, n_ctx, rows, n_ctx_tiles):
    t = pl.program_id(1)
    kh_rows = min(NA_WIN_H, rows)
    n_loc = kh_rows * GRID_W
    dh = NA_HEAD_DIM

    @pl.when(t < n_ctx_tiles)
    def _ctx():
        outs = []
        for h in range(NA_HEADS):
            hs = slice(h * dh, (h + 1) * dh)
            s = _dot_nt(q_ref[:, hs], k_ref[0:n_ctx, hs])
            outs.append(_softmax_pv([s], [v_ref[0:n_ctx, hs]]))
        o_ref[...] = jnp.concatenate(outs, axis=-1)

    @pl.when(t >= n_ctx_tiles)
    def _latent():
        for i in range(TM // GRID_W):
            r = (t - n_ctx_tiles) * (TM // GRID_W) + i
            rs = jnp.clip(r - kh_rows // 2, 0, rows - kh_rows)
            var = r - rs
            start = pl.multiple_of(n_ctx + rs * GRID_W, GRID_W)
            outs = []
            for h in range(NA_HEADS):
                hs = slice(h * dh, (h + 1) * dh)
                qh = q_ref[i * GRID_W:(i + 1) * GRID_W, hs]
                s_loc = _dot_nt(qh, k_ref[pl.ds(start, n_loc), hs]) + bias_ref[var, h]
                s_ctx = _dot_nt(qh, k_ref[0:n_ctx, hs])
                outs.append(_softmax_pv([s_loc, s_ctx], [v_ref[pl.ds(start, n_loc), hs], v_ref[0:n_ctx, hs]]))
            o_ref[i * GRID_W:(i + 1) * GRID_W, :] = jnp.concatenate(outs, axis=-1)


def _na_bias_table(rpb, rows):
    kh = min(NA_WIN_H, rows)
    cols = jnp.arange(GRID_W)
    cs = jnp.clip(cols - NA_WIN_W // 2, 0, GRID_W - NA_WIN_W)
    kc = jnp.arange(GRID_W)
    in_win = (kc[None, :] >= cs[:, None]) & (kc[None, :] < cs[:, None] + NA_WIN_W)
    col_off = jnp.clip(kc[None, :] - cols[:, None] + NA_WIN_W - 1, 0, 2 * NA_WIN_W - 2)
    var = jnp.arange(kh)
    a = jnp.arange(kh)
    row_off = jnp.clip(a[None, :] - var[:, None] + NA_WIN_H - 1, 0, 2 * NA_WIN_H - 2)
    tab = rpb[:, row_off[:, None, :, None], col_off[None, :, None, :]]
    tab = jnp.where(in_win[None, None, :, None, :], tab, NEG)
    return tab.transpose(1, 0, 2, 3, 4).reshape(kh, NA_HEADS, GRID_W, kh * GRID_W).astype(F32)


def _na(q, k, v, rpb, n_ctx):
    bsz, t, w = q.shape
    rows = (t - n_ctx) // GRID_W
    bias = _na_bias_table(rpb, rows)
    return pl.pallas_call(
        functools.partial(_na_kernel, n_ctx=n_ctx, rows=rows, n_ctx_tiles=n_ctx // TM),
        grid=(bsz, t // TM),
        in_specs=[_tok_spec(w),
                  pl.BlockSpec((None, t, w), lambda b, i: (b, 0, 0)),
                  pl.BlockSpec((None, t, w), lambda b, i: (b, 0, 0)),
                  _const_spec(bias.shape)],
        out_specs=_tok_spec(w),
        out_shape=jax.ShapeDtypeStruct((bsz, t, w), F32),
        compiler_params=_cparams(("parallel", "arbitrary")),
        name="natten",
    )(q, k, v, bias)


def _mlprep_kernel(xm_ref, prev_ref, next_ref, cw_ref, cb_ref, wq_ref, wk_ref, q_ref, k_ref,
                   *, n_ctx_tiles, n_tiles):
    t = pl.program_id(1)
    has_prev = jnp.logical_and(t != 0, t != n_ctx_tiles)
    has_next = jnp.logical_and(t != n_ctx_tiles - 1, t != n_tiles - 1)
    prev = jnp.where(has_prev, prev_ref[...], 0.0)
    nxt = jnp.where(has_next, next_ref[...], 0.0)
    ext = jnp.concatenate([prev, xm_ref[...], nxt], axis=0)
    acc = cb_ref[...]
    for j in range(ML_CONV):
        off = SUBLANES + j - ML_CONV // 2
        acc = acc + cw_ref[j:j + 1, :] * ext[off:off + TM, :]
    xc = _silu(acc).astype(BF16)
    scale = ML_HEAD_DIM ** -0.5
    for h in range(ML_HEADS):
        hs = slice(h * ML_HEAD_DIM, (h + 1) * ML_HEAD_DIM)
        q_ref[:, hs] = _dot(xc[:, hs], wq_ref[h]).astype(BF16)
        k_ref[:, hs] = (_dot(xc[:, hs], wk_ref[h]) * scale).astype(BF16)


def _mlprep(xm, conv_w, conv_b, wq, wk, n_ctx):
    bsz, t, w = xm.shape
    nt = t // TM
    per = TM // SUBLANES
    last = t // SUBLANES - 1
    return pl.pallas_call(
        functools.partial(_mlprep_kernel, n_ctx_tiles=n_ctx // TM, n_tiles=nt),
        grid=(bsz, nt),
        in_specs=[_tok_spec(w),
                  pl.BlockSpec((None, SUBLANES, w), lambda b, i: (b, jnp.maximum(i * per - 1, 0), 0)),
                  pl.BlockSpec((None, SUBLANES, w), lambda b, i: (b, jnp.minimum((i + 1) * per, last), 0)),
                  _const_spec(conv_w.shape), _const_spec((1, w)),
                  _const_spec(wq.shape), _const_spec(wk.shape)],
        out_specs=[_tok_spec(w), _tok_spec(w)],
        out_shape=[jax.ShapeDtypeStruct((bsz, t, w), BF16)] * 2,
        compiler_params=_cparams(("parallel", "parallel")),
        name="mlstm_prep",
    )(xm, xm, xm, conv_w, conv_b.reshape(1, w), wq, wk)


def _scan_specs(width, n_chunks, n_ctx_chunks):
    fwd = pl.BlockSpec((None, CHUNK, width), lambda b, i: (b, i, 0))
    bwd = pl.BlockSpec(
        (None, CHUNK, width),
        lambda b, i: (b, jnp.where(i < n_ctx_chunks, n_ctx_chunks - 1 - i, n_chunks + n_ctx_chunks - 1 - i), 0))
    return fwd, bwd


def _split3(x):
    hi = x.astype(BF16)
    r1 = x - hi.astype(F32)
    mid = r1.astype(BF16)
    lo = (r1 - mid.astype(F32)).astype(BF16)
    return hi, mid, lo


def _mlscan_kernel(qf_ref, kf_ref, vf_ref, gf_ref, qb_ref, kb_ref, vb_ref, gb_ref, bias_ref,
                   of_ref, ob_ref, st_ref, m_ref):
    i = pl.program_id(1)
    dk = ML_HEAD_DIM

    @pl.when(i == 0)
    def _init():
        st_ref[...] = jnp.zeros_like(st_ref)
        m_ref[...] = jnp.zeros_like(m_ref)

    tt = lax.broadcasted_iota(jnp.int32, (CHUNK, CHUNK), 0)
    ss = lax.broadcasted_iota(jnp.int32, (CHUNK, CHUNK), 1)
    lane = lax.broadcasted_iota(jnp.int32, (CHUNK, LANES), 1)
    ones_col = (lane == 0).astype(BF16)

    for d, (q_ref, k_ref, v_ref, g_ref, o_ref) in enumerate(
            ((qf_ref, kf_ref, vf_ref, gf_ref, of_ref), (qb_ref, kb_ref, vb_ref, gb_ref, ob_ref))):
        causal = (ss <= tt) if d == 0 else (ss >= tt)
        tri = causal.astype(BF16)
        g = g_ref[...] + bias_ref[...]
        ig = g[:, :LANES]
        lf = _log_sigmoid(g[:, LANES:])
        hi, mid, lo = _split3(lf)
        bc = _dot(tri, hi) + _dot(tri, mid) + _dot(tri, lo)
        e = ig - bc
        e_t = e.T
        last = CHUNK - 1 if d == 0 else 0
        for h in range(ML_HEADS):
            c = d * ML_HEADS + h
            hs = slice(h * dk, (h + 1) * dk)
            q, k = q_ref[:, hs], k_ref[:, hs]
            v_aug = jnp.concatenate([v_ref[:, hs], ones_col], axis=-1)
            m_prev = m_ref[c:c + 1, 0:1]
            b_col = bc[:, c:c + 1]
            e_col = e[:, c:c + 1]
            dlog = jnp.where(causal, b_col + e_t[c:c + 1, :], -jnp.inf)
            m_t = jnp.maximum(b_col + m_prev, dlog.max(axis=-1, keepdims=True))
            s = _dot_nt(q, k) * jnp.exp(dlog - m_t)
            w_inter = jnp.exp(b_col + m_prev - m_t)
            state = st_ref[c]
            res = _dot(s.astype(BF16), v_aug) + w_inter * _dot(q, state.astype(BF16))
            den = res[:, dk:dk + 1]
            o_ref[:, hs] = res[:, :dk] / jnp.maximum(jnp.abs(den), jnp.exp(-m_t))
            b_last = bc[last:last + 1, c:c + 1]
            w_log = b_last + e_col
            m_new = jnp.maximum(b_last + m_prev, w_log.max(axis=0, keepdims=True))
            w_in = jnp.exp(w_log - m_new)
            st_ref[c] = jnp.exp(b_last + m_prev - m_new) * state + _dot_tn(k, (v_aug * w_in).astype(BF16))
            m_ref[c:c + 1, :] = jnp.broadcast_to(m_new, (1, LANES))


def _mlscan(q, k, v, gates, gate_bias, n_ctx):
    bsz, t, w = q.shape
    nc, ncc = t // CHUNK, n_ctx // CHUNK
    f, bk = _scan_specs(w, nc, ncc)
    gf, gb = _scan_specs(2 * LANES, nc, ncc)
    n_chain = 2 * ML_HEADS
    return pl.pallas_call(
        _mlscan_kernel,
        grid=(bsz, nc),
        in_specs=[f, f, f, gf, bk, bk, bk, gb, _const_spec((1, 2 * LANES))],
        out_specs=[f, bk],
        out_shape=[jax.ShapeDtypeStruct((bsz, t, w), F32)] * 2,
        scratch_shapes=[pltpu.VMEM((n_chain, ML_HEAD_DIM, 2 * ML_HEAD_DIM), F32),
                        pltpu.VMEM((SUBLANES, LANES), F32)],
        compiler_params=_cparams(("parallel", "arbitrary")),
        name="mlstm_scan",
    )(q, k, v, gates, q, k, v, gates, gate_bias)


def _retscan_kernel(qf_ref, kf_ref, vf_ref, qb_ref, kb_ref, vb_ref, logit_ref, of_ref, ob_ref, st_ref):
    i = pl.program_id(1)
    dk = RET_HEAD_DIM

    @pl.when(i == 0)
    def _init():
        st_ref[...] = jnp.zeros_like(st_ref)

    tt = lax.broadcasted_iota(jnp.int32, (CHUNK, CHUNK), 0)
    ss = lax.broadcasted_iota(jnp.int32, (CHUNK, CHUNK), 1)
    pos = lax.broadcasted_iota(jnp.int32, (CHUNK, 1), 0).astype(F32)
    log_g = _log_sigmoid(logit_ref[...])

    for d, (q_ref, k_ref, v_ref, o_ref) in enumerate(
            ((qf_ref, kf_ref, vf_ref, of_ref), (qb_ref, kb_ref, vb_ref, ob_ref))):
        dist = (tt - ss) if d == 0 else (ss - tt)
        distf = jnp.maximum(dist, 0).astype(F32)
        step = pos if d == 0 else (CHUNK - 1.0) - pos
        for h in range(RET_HEADS):
            c = d * RET_HEADS + h
            hs = slice(h * dk, (h + 1) * dk)
            lg = log_g[c:c + 1, 0:1]
            dmat = jnp.where(dist >= 0, jnp.exp(distf * lg), 0.0)
            q, k, v = q_ref[:, hs], k_ref[:, hs], v_ref[:, hs]
            state = st_ref[c]
            inner = _dot((_dot_nt(q, k) * dmat).astype(BF16), v)
            cross = jnp.exp((step + 1.0) * lg) * _dot(q, state.astype(BF16))
            o_ref[:, hs] = inner + cross
            k_dec = jnp.exp((CHUNK - 1.0 - step) * lg)
            st_ref[c] = jnp.exp(CHUNK * lg) * state + _dot_tn(k, (v.astype(F32) * k_dec).astype(BF16))


def _retscan(q, k, v, decay_logit, n_ctx):
    bsz, t, w = q.shape
    nc, ncc = t // CHUNK, n_ctx // CHUNK
    f, bk = _scan_specs(w, nc, ncc)
    logit = jnp.broadcast_to(decay_logit.astype(F32).reshape(2 * RET_HEADS, 1), (2 * RET_HEADS, LANES))
    return pl.pallas_call(
        _retscan_kernel,
        grid=(bsz, nc),
        in_specs=[f, f, f, bk, bk, bk, _const_spec((2 * RET_HEADS, LANES))],
        out_specs=[f, bk],
        out_shape=[jax.ShapeDtypeStruct((bsz, t, w), F32)] * 2,
        scratch_shapes=[pltpu.VMEM((2 * RET_HEADS, RET_HEAD_DIM, RET_HEAD_DIM), F32)],
        compiler_params=_cparams(("parallel", "arbitrary")),
        name="retention_scan",
    )(q, k, v, q, k, v, logit)


def _s5_weights(lam_re, lam_im, log_dt, b_re, b_im, c_re, c_im):
    hp = lax.Precision.HIGHEST
    n = S5_SUB
    dt = jnp.exp(log_dt)[..., None]
    zr, zi = lam_re * dt, lam_im * dt
    steps = jnp.arange(n + 1, dtype=F32)[:, None, None, None]
    pmag = jnp.exp(steps * zr)
    ak_re, ak_im = pmag * jnp.cos(steps * zi), pmag * jnp.sin(steps * zi)
    a_re, a_im = ak_re[1], ak_im[1]
    lam_sq = jnp.square(lam_re) + jnp.square(lam_im)
    e_re = ((a_re - 1.0) * lam_re + a_im * lam_im) / lam_sq
    e_im = (a_im * lam_re - (a_re - 1.0) * lam_im) / lam_sq
    bb_re = e_re[..., None] * b_re - e_im[..., None] * b_im
    bb_im = e_re[..., None] * b_im + e_im[..., None] * b_re
    ab_re = ak_re[..., None] * bb_re - ak_im[..., None] * bb_im
    ab_im = ak_re[..., None] * bb_im + ak_im[..., None] * bb_re
    kern = (jnp.einsum('dgqp,kdgpr->kdgqr', c_re, ab_re, precision=hp)
            - jnp.einsum('dgqp,kdgpr->kdgqr', c_im, ab_im, precision=hp))
    sig = jnp.arange(n)
    lag = sig[None, :] - sig[:, None]
    toe = jnp.where((lag >= 0)[:, :, None, None, None, None],
                    kern[jnp.clip(lag, 0, n)], 0.0)
    toe = toe.transpose(2, 3, 0, 5, 1, 4)
    vin_re = ab_re[n - 1 - sig].transpose(1, 2, 0, 4, 3)
    vin_im = ab_im[n - 1 - sig].transpose(1, 2, 0, 4, 3)
    ap_re, ap_im = ak_re[1:], ak_im[1:]
    w_re = c_re[None] * ap_re[:, :, :, None, :] - c_im[None] * ap_im[:, :, :, None, :]
    w_im = -(c_re[None] * ap_im[:, :, :, None, :] + c_im[None] * ap_re[:, :, :, None, :])
    w_re = w_re.transpose(1, 2, 4, 0, 3)
    w_im = w_im.transpose(1, 2, 4, 0, 3)

    def orient(a, axes):
        return jnp.stack([a[0], jnp.flip(a[1], axis=tuple(x - 1 for x in axes))], axis=0)

    toe = orient(toe, (2, 4))
    vin_re, vin_im = orient(vin_re, (2,)), orient(vin_im, (2,))
    w_re, w_im = orient(w_re, (3,)), orient(w_im, (3,))
    g = lam_re.shape[1]
    nq = n * S5_GROUP
    toe = toe.reshape(2, g, nq, nq)
    vin_re, vin_im = vin_re.reshape(2, g, nq, S5_STATE), vin_im.reshape(2, g, nq, S5_STATE)
    tv = jnp.concatenate([toe, vin_re, vin_im, vin_im, vin_re], axis=-1).astype(BF16)
    w = jnp.concatenate([w_re.reshape(2, g, S5_STATE, nq), w_im.reshape(2, g, S5_STATE, nq)], axis=2).astype(BF16)
    an_re, an_im = ak_re[n], ak_im[n]
    a1 = jnp.concatenate([an_re, an_re], axis=-1)[:, :, None, :]
    a2 = jnp.concatenate([-an_im, an_im], axis=-1)[:, :, None, :]
    return tv, w, a1, a2


def _s5_kernel(u_ref, tv_ref, w_ref, a1_ref, a2_ref, y_ref, in_ref, xp_ref, *, bsz, n_sub, n_ctx_sub):
    nq = S5_SUB * S5_GROUP
    ns = 2 * S5_STATE
    u = u_ref[...]
    y = None
    for d in range(2):
        r = _dot(u, tv_ref[d])
        in_ref[d] = r[:, nq:]
        y = r[:, :nq] if y is None else y + r[:, :nq]
    a1f, a2f, a1b, a2b = a1_ref[0], a2_ref[0], a1_ref[1], a2_ref[1]

    def body(i, carry):
        xf, xsf, xb, xsb = carry
        rf = pl.multiple_of(i * bsz, bsz)
        jb = jnp.where(i < n_ctx_sub, n_ctx_sub - 1 - i, n_sub + n_ctx_sub - 1 - i)
        rb = pl.multiple_of(jb * bsz, bsz)
        xp_ref[0, pl.ds(rf, bsz), :] = xf
        xp_ref[1, pl.ds(rb, bsz), :] = xb
        inf = in_ref[0, pl.ds(rf, bsz), :]
        inb = in_ref[1, pl.ds(rb, bsz), :]
        nxf = a1f * xf + a2f * xsf + inf[:, :ns]
        nxsf = a1f * xsf - a2f * xf + inf[:, ns:]
        nxb = a1b * xb + a2b * xsb + inb[:, :ns]
        nxsb = a1b * xsb - a2b * xb + inb[:, ns:]
        return nxf, nxsf, nxb, nxsb

    z = jnp.zeros((bsz, ns), F32)
    lax.fori_loop(0, n_sub, body, (z, z, z, z))
    for d in range(2):
        y = y + _dot(xp_ref[d].astype(BF16), w_ref[d])
    y_ref[...] = y


def _s5(u, tv, w, a1, a2, n_ctx):
    bsz, t, width = u.shape
    g = S5_GROUPS
    n_sub = t // S5_SUB
    nq = S5_SUB * S5_GROUP
    rows = n_sub * bsz
    ug = (u.astype(BF16).reshape(bsz, n_sub, S5_SUB, g, S5_GROUP)
          .transpose(3, 1, 0, 2, 4).reshape(g, rows, nq))
    yg = pl.pallas_call(
        functools.partial(_s5_kernel, bsz=bsz, n_sub=n_sub, n_ctx_sub=n_ctx // S5_SUB),
        grid=(g,),
        in_specs=[pl.BlockSpec((None, rows, nq), lambda i: (i, 0, 0)),
                  pl.BlockSpec((2, None, nq, 2 * nq), lambda i: (0, i, 0, 0)),
                  pl.BlockSpec((2, None, 2 * S5_STATE, nq), lambda i: (0, i, 0, 0)),
                  pl.BlockSpec((2, None, 1, 2 * S5_STATE), lambda i: (0, i, 0, 0)),
                  pl.BlockSpec((2, None, 1, 2 * S5_STATE), lambda i: (0, i, 0, 0))],
        out_specs=pl.BlockSpec((None, rows, nq), lambda i: (i, 0, 0)),
        out_shape=jax.ShapeDtypeStruct((g, rows, nq), F32),
        scratch_shapes=[pltpu.VMEM((2, rows, nq), F32), pltpu.VMEM((2, rows, 2 * S5_STATE), F32)],
        compiler_params=_cparams(("parallel",)),
        name="s5",
    )(ug, tv, w, a1, a2)
    return (yg.reshape(g, n_sub, bsz, S5_SUB, S5_GROUP)
            .transpose(2, 1, 3, 0, 4).reshape(bsz, t, width))


def _out_even_kernel(x_ref, mod_ref, na_ref, hf_ref, hb_ref, zo_ref, gn_ref, w_ref, lng_ref, lnb_ref,
                     o_ref, *, alpha):
    ml = _sigmoid(zo_ref[...]) * _head_norm(hf_ref[...] + hb_ref[...], gn_ref[...], ML_HEADS, ML_HEAD_DIM)
    y = _dot(na_ref[...].astype(BF16), w_ref[0:HALF, :]) + _dot(ml.astype(BF16), w_ref[HALF:, :])
    r = alpha * x_ref[...] + _mod_row(mod_ref, 1, 2) * y
    o_ref[...] = _layer_norm(r, lng_ref[...], lnb_ref[...])


def _out_odd_kernel(x_ref, mod_ref, ys_ref, u_ref, dsk_ref, gw_ref, gb_ref, rf_ref, rb_ref, gr_ref, gn_ref,
                    w_ref, lng_ref, lnb_ref, o_ref, *, alpha):
    s = _gelu_tanh(ys_ref[...] + dsk_ref[...] * u_ref[...])
    s = s * _sigmoid(_dot(s.astype(BF16), gw_ref[...]) + gb_ref[...])
    ret = _silu(gr_ref[...]) * _head_norm(rf_ref[...] + rb_ref[...], gn_ref[...], RET_HEADS, RET_HEAD_DIM)
    y = _dot(s.astype(BF16), w_ref[0:HALF, :]) + _dot(ret.astype(BF16), w_ref[HALF:, :])
    r = alpha * x_ref[...] + _mod_row(mod_ref, 1, 2) * y
    o_ref[...] = _layer_norm(r, lng_ref[...], lnb_ref[...])


def _out_call(kernel, name, xa, mod, extra, alpha, n_ctx_tiles):
    bsz, t, d = xa.shape
    in_specs = [_tok_spec(d), _mod_block(d, bsz, n_ctx_tiles)]
    args = [xa, mod]
    for kind, a in extra:
        in_specs.append(_tok_spec(a.shape[-1]) if kind == 't' else _const_spec(a.shape))
        args.append(a)
    return pl.pallas_call(
        functools.partial(kernel, alpha=alpha),
        grid=(bsz, t // TM),
        in_specs=in_specs,
        out_specs=_tok_spec(d),
        out_shape=jax.ShapeDtypeStruct((bsz, t, d), F32),
        compiler_params=_cparams(("parallel", "parallel")),
        name=name,
    )(*args)


def _rope_tables(n_ctx, seq):
    nf = RET_HEAD_DIM // 4
    freqs = ROPE_BASE ** (-jnp.arange(nf, dtype=F32) / nf)
    tok = jnp.arange(seq)
    ang_r = (tok // GRID_W).astype(F32)[:, None] * freqs
    ang_c = (tok % GRID_W).astype(F32)[:, None] * freqs
    cos = jnp.concatenate([jnp.cos(ang_r)] * 2 + [jnp.cos(ang_c)] * 2, axis=-1)
    sin = jnp.concatenate([-jnp.sin(ang_r), jnp.sin(ang_r), -jnp.sin(ang_c), jnp.sin(ang_c)], axis=-1)
    cos = jnp.concatenate([jnp.ones((n_ctx, LANES), F32), cos], axis=0)
    sin = jnp.concatenate([jnp.zeros((n_ctx, LANES), F32), sin], axis=0)
    return cos, sin


def _even_layer_mixer(xa, mod, w_in, w_out, rpb, conv_w, conv_b, wq, wk, i_bias, f_bias, gn_w,
                      lng, lnb, alpha, n_ctx):
    d = xa.shape[-1]
    n_ctx_tiles = n_ctx // TM
    ng = 2 * ML_HEADS
    main = 6 * HALF
    w_gates = jnp.zeros((d, 2 * LANES), F32)
    w_gates = w_gates.at[:, 0:ng].set(w_in[:, main:main + ng]).at[:, LANES:LANES + ng].set(w_in[:, main + ng:])
    w_all = jnp.concatenate([w_in[:, :main], w_gates], axis=1).astype(BF16)
    gate_bias = jnp.zeros((1, 2 * LANES), F32)
    gate_bias = gate_bias.at[0, 0:ng].set(i_bias.reshape(ng)).at[0, LANES:LANES + ng].set(f_bias.reshape(ng))
    outs = [(0, HALF, NA_HEAD_DIM ** -0.5, False), (HALF, HALF, 1.0, False), (2 * HALF, HALF, 1.0, False),
            (3 * HALF, HALF, 1.0, False), (4 * HALF, HALF, 1.0, False), (5 * HALF, HALF, 1.0, False),
            (main, 2 * LANES, 1.0, False)]
    dts = [BF16, BF16, BF16, F32, BF16, F32, F32]
    q_na, k_na, v_na, xm, zv, zo, gates = _inproj(xa, mod, w_all, outs, dts, n_ctx_tiles)
    na = _na(q_na, k_na, v_na, rpb, n_ctx)
    q_ml, k_ml = _mlprep(xm, conv_w, conv_b, wq.astype(BF16), wk.astype(BF16), n_ctx)
    hf, hb = _mlscan(q_ml, k_ml, zv, gates, gate_bias, n_ctx)
    return _out_call(
        _out_even_kernel, "out_even", xa, mod,
        [('t', na), ('t', hf), ('t', hb), ('t', zo), ('c', gn_w.reshape(1, HALF)), ('c', w_out.astype(BF16)),
         ('c', lng.reshape(1, d)), ('c', lnb.reshape(1, d))], alpha, n_ctx_tiles)


def _odd_layer_mixer(xa, mod, w_in, w_out, s5_params, d_skip, glu_w, glu_b, decay_logit, gn_w,
                     lng, lnb, alpha, n_ctx, rope_tabs):
    d = xa.shape[-1]
    n_ctx_tiles = n_ctx // TM
    outs = [(0, HALF, 1.0, False), (HALF, HALF, 1.0, True), (2 * HALF, HALF, RET_HEAD_DIM ** -0.5, True),
            (3 * HALF, HALF, 1.0, False), (4 * HALF, HALF, 1.0, False)]
    dts = [F32, BF16, BF16, BF16, F32]
    u, q_r, k_r, v_r, g_r = _inproj(xa, mod, w_in.astype(BF16), outs, dts, n_ctx_tiles, rope_tabs)
    ys = _s5(u, *_s5_weights(*s5_params), n_ctx)
    rf, rb = _retscan(q_r, k_r, v_r, decay_logit, n_ctx)
    return _out_call(
        _out_odd_kernel, "out_odd", xa, mod,
        [('t', ys), ('t', u), ('c', d_skip.reshape(1, HALF)), ('c', glu_w.astype(BF16)),
         ('c', glu_b.reshape(1, HALF)), ('t', rf), ('t', rb), ('t', g_r), ('c', gn_w.reshape(1, HALF)),
         ('c', w_out.astype(BF16)), ('c', lng.reshape(1, d)), ('c', lnb.reshape(1, d))],
        alpha, n_ctx_tiles)


def kernel(x, c, ctx, c_ctx, ada_w, ada_b, ffn_w_gate, ffn_w_up, ffn_w_down, ln_g, ln_b, ev_w_in, ev_w_out, na_rpb, ml_conv_w, ml_conv_b, ml_wq, ml_wk, ml_i_bias, ml_f_bias, ml_gn_w, od_w_in, od_w_out, s5_lam_re, s5_lam_im, s5_log_dt, s5_b_re, s5_b_im, s5_c_re, s5_c_im, s5_d, s5_glu_w, s5_glu_b, ret_decay_logit, ret_gn_w):
    bsz, seq, d = x.shape
    n_ctx = ctx.shape[1]
    depth = ada_w.shape[0]
    assert n_ctx % TM == 0 and seq % TM == 0 and seq % GRID_W == 0 and bsz == SUBLANES
    n_ctx_tiles = n_ctx // TM
    alpha = (2.0 * depth) ** 0.25

    rows = bsz + SUBLANES
    c_rows = jnp.zeros((rows, d), F32).at[:bsz].set(c).at[bsz].set(c_ctx)
    mod_all = _mod_table(c_rows, ada_w, ada_b)

    xa = jnp.concatenate([ctx, x], axis=1)
    rope_tabs = _rope_tables(n_ctx, seq)
    for l in range(depth):
        e = l // 2
        mod = mod_all[l]
        ffn = lambda j, k, h: _ffn(h, mod, j, ffn_w_gate[l, k].astype(BF16), ffn_w_up[l, k].astype(BF16),
                                   ffn_w_down[l, k].astype(BF16), ln_g[l, j], ln_b[l, j], alpha, n_ctx_tiles)
        xa = ffn(0, 0, xa)
        if l % 2 == 0:
            xa = _even_layer_mixer(xa, mod, ev_w_in[e], ev_w_out[e], na_rpb[e], ml_conv_w[e], ml_conv_b[e],
                                   ml_wq[e], ml_wk[e], ml_i_bias[e], ml_f_bias[e], ml_gn_w[e],
                                   ln_g[l, 1], ln_b[l, 1], alpha, n_ctx)
        else:
            s5_params = (s5_lam_re[e], s5_lam_im[e], s5_log_dt[e], s5_b_re[e], s5_b_im[e], s5_c_re[e], s5_c_im[e])
            xa = _odd_layer_mixer(xa, mod, od_w_in[e], od_w_out[e], s5_params, s5_d[e], s5_glu_w[e], s5_glu_b[e],
                                  ret_decay_logit[e], ret_gn_w[e], ln_g[l, 1], ln_b[l, 1], alpha, n_ctx, rope_tabs)
        xa = ffn(2, 1, xa)
    return xa[:, n_ctx:, :]
```

```python
import functools
import math

import jax
import jax.numpy as jnp
from jax import lax
from jax.experimental import pallas as pl
from jax.experimental.pallas import tpu as pltpu

F32 = jnp.float32
BF16 = jnp.bfloat16

GRID_W = 64
LN_EPS = 1e-5
N_MOD = 9
NA_HEADS, NA_HEAD_DIM, NA_WIN_H, NA_WIN_W = 8, 64, 8, 16
ML_HEADS, ML_HEAD_DIM, ML_CONV = 4, 128, 5
S5_GROUP, S5_GROUPS, S5_STATE = 16, 32, 64
RET_HEADS, RET_HEAD_DIM = 4, 128
ROPE_BASE = 10000.0
HALF = 512

LANES = 128
SUBLANES = 8
VMEM_LIMIT = 56 * 1024 * 1024

TM = 256
CHUNK = 128
S5_SUB = 16
NEG = -1e30


def _cparams(sem):
    return pltpu.CompilerParams(dimension_semantics=sem, vmem_limit_bytes=VMEM_LIMIT)


def _const_spec(shape):
    nd = len(shape)
    return pl.BlockSpec(shape, lambda *_: (0,) * nd, pipeline_mode=pl.Buffered(1))


def _dot(a, b):
    return jnp.dot(a, b, preferred_element_type=F32)


def _dot_nt(a, b):
    return lax.dot_general(a, b, (((1,), (1,)), ((), ())), preferred_element_type=F32)


def _dot_tn(a, b):
    return lax.dot_general(a, b, (((0,), (0,)), ((), ())), preferred_element_type=F32)


def _sigmoid(x):
    return 1.0 / (1.0 + jnp.exp(-x))


def _silu(x):
    return x * _sigmoid(x)


def _log_sigmoid(x):
    return jnp.minimum(x, 0.0) - jnp.log1p(jnp.exp(-jnp.abs(x)))


def _gelu_tanh(x):
    return 0.5 * x * (1.0 + jnp.tanh(math.sqrt(2.0 / math.pi) * (x + 0.044715 * (x * x * x))))


def _layer_norm(r, g, b):
    mu = jnp.mean(r, axis=-1, keepdims=True)
    c = r - mu
    var = jnp.mean(c * c, axis=-1, keepdims=True)
    return c * lax.rsqrt(var + LN_EPS) * g + b


def _head_norm(h, w, n_heads, head_dim):
    parts = []
    for k in range(n_heads):
        hk = h[:, k * head_dim:(k + 1) * head_dim]
        mu = jnp.mean(hk, axis=-1, keepdims=True)
        c = hk - mu
        var = jnp.mean(c * c, axis=-1, keepdims=True)
        parts.append(c * lax.rsqrt(var + LN_EPS))
    return jnp.concatenate(parts, axis=-1) * w


def _mod_row(mod_ref, j, k):
    i = 3 * j + k
    return mod_ref[i:i + 1, :]


def _mod_kernel(c_ref, w_ref, b_ref, o_ref):
    c = c_ref[...]
    s = _silu(c).astype(BF16)
    o_ref[...] = _dot(s, w_ref[...].astype(BF16)) + b_ref[...]


def _mod_table(c_rows, ada_w, ada_b):
    depth, d, n = ada_w.shape
    rows = c_rows.shape[0]
    tn = 1024
    out = pl.pallas_call(
        _mod_kernel,
        grid=(depth, n // tn),
        in_specs=[pl.BlockSpec((rows, d), lambda l, j: (0, 0)),
                  pl.BlockSpec((None, d, tn), lambda l, j: (l, 0, j)),
                  pl.BlockSpec((None, 1, tn), lambda l, j: (l, 0, j))],
        out_specs=pl.BlockSpec((None, rows, tn), lambda l, j: (l, 0, j)),
        out_shape=jax.ShapeDtypeStruct((depth, rows, n), F32),
        compiler_params=_cparams(("parallel", "parallel")),
        name="mod_table",
    )(c_rows, ada_w, ada_b.reshape(depth, 1, n))
    return out.reshape(depth, rows, N_MOD, d)


def _tok_spec(width, tm=TM):
    return pl.BlockSpec((None, tm, width), lambda b, t: (b, t, 0))


def _mod_block(d, batch, n_ctx_tiles):
    return pl.BlockSpec((None, N_MOD, d), lambda b, t: (jnp.where(t < n_ctx_tiles, batch, b), 0, 0))


def _ffn_kernel(*refs, j, alpha, n_ctx_tiles, split_in):
    if split_in:
        ctx_ref, x_ref = refs[:2]
        refs = refs[1:]
        x = jnp.where(pl.program_id(1) < n_ctx_tiles, ctx_ref[...], x_ref[...])
    else:
        x = refs[0][...]
    _, mod_ref, wg_ref, wu_ref, wd_ref, lng_ref, lnb_ref, o_ref = refs
    h = (x * (1.0 + _mod_row(mod_ref, j, 1)) + _mod_row(mod_ref, j, 0)).astype(BF16)
    g = _dot(h, wg_ref[...])
    u = _dot(h, wu_ref[...])
    a = (_silu(g) * u).astype(BF16)
    y = _dot(a, wd_ref[...])
    r = alpha * x + (0.5 * _mod_row(mod_ref, j, 2)) * y
    o_ref[...] = _layer_norm(r, lng_ref[...], lnb_ref[...])


def _ffn(xa, mod, j, wg, wu, wd, lng, lnb, alpha, n_ctx_tiles, ctx=None, latent_only=False):
    bsz, t, d = xa.shape
    dff = wg.shape[1]
    nct = n_ctx_tiles
    weights = [_const_spec((d, dff)), _const_spec((d, dff)), _const_spec((dff, d)),
               _const_spec((1, d)), _const_spec((1, d))]
    wargs = (wg, wu, wd, lng.reshape(1, d), lnb.reshape(1, d))
    if ctx is not None:
        t_out = t + ctx.shape[1]
        grid = (bsz, t_out // TM)
        in_specs = [pl.BlockSpec((None, TM, d), lambda b, i: (b, jnp.minimum(i, nct - 1), 0)),
                    pl.BlockSpec((None, TM, d), lambda b, i: (b, jnp.maximum(i - nct, 0), 0)),
                    _mod_block(d, bsz, nct)]
        args = (ctx, xa, mod)
        out_spec = _tok_spec(d)
    elif latent_only:
        t_out = t - nct * TM
        grid = (bsz, t_out // TM)
        in_specs = [pl.BlockSpec((None, TM, d), lambda b, i: (b, i + nct, 0)),
                    pl.BlockSpec((None, N_MOD, d), lambda b, i: (b, 0, 0))]
        args = (xa, mod)
        out_spec = _tok_spec(d)
    else:
        t_out = t
        grid = (bsz, t // TM)
        in_specs = [_tok_spec(d), _mod_block(d, bsz, nct)]
        args = (xa, mod)
        out_spec = _tok_spec(d)
    return pl.pallas_call(
        functools.partial(_ffn_kernel, j=j, alpha=alpha, n_ctx_tiles=nct, split_in=ctx is not None),
        grid=grid,
        in_specs=in_specs + weights,
        out_specs=out_spec,
        out_shape=jax.ShapeDtypeStruct((bsz, t_out, d), F32),
        compiler_params=_cparams(("parallel", "parallel")),
        name="ffn",
    )(*args, *wargs)


def _rope(z, cos, sin):
    lane = lax.broadcasted_iota(jnp.int32, (1, LANES), 1)
    first = (lane % 64) < 32
    parts = []
    for k in range(z.shape[1] // LANES):
        zk = z[:, k * LANES:(k + 1) * LANES]
        partner = jnp.where(first, pltpu.roll(zk, LANES - 32, axis=1), pltpu.roll(zk, 32, axis=1))
        parts.append(zk * cos + partner * sin)
    return jnp.concatenate(parts, axis=-1)


def _inproj_kernel(*refs, outs, use_rope):
    x_ref, mod_ref, w_ref = refs[:3]
    k = 3
    if use_rope:
        cos_ref, sin_ref = refs[3:5]
        k = 5
    out_refs = refs[k:]
    h = (x_ref[...] * (1.0 + _mod_row(mod_ref, 1, 1)) + _mod_row(mod_ref, 1, 0)).astype(BF16)
    for (off, width, scale, rope), o_ref in zip(outs, out_refs):
        z = _dot(h, w_ref[:, off:off + width])
        if rope:
            z = _rope(z, cos_ref[...], sin_ref[...])
        if scale != 1.0:
            z = z * scale
        o_ref[...] = z.astype(o_ref.dtype)


def _inproj(xa, mod, w, outs, dtypes, n_ctx_tiles, rope_tabs=None):
    bsz, t, d = xa.shape
    in_specs = [_tok_spec(d), _mod_block(d, bsz, n_ctx_tiles), _const_spec(w.shape)]
    args = [xa, mod, w]
    if rope_tabs is not None:
        in_specs += [pl.BlockSpec((TM, LANES), lambda b, t: (t, 0))] * 2
        args += list(rope_tabs)
    return pl.pallas_call(
        functools.partial(_inproj_kernel, outs=tuple(outs), use_rope=rope_tabs is not None),
        grid=(bsz, t // TM),
        in_specs=in_specs,
        out_specs=[_tok_spec(o[1]) for o in outs],
        out_shape=[jax.ShapeDtypeStruct((bsz, t, o[1]), dt) for o, dt in zip(outs, dtypes)],
        compiler_params=_cparams(("parallel", "parallel")),
        name="inproj",
    )(*args)


NA_ROWS_PER_TILE = TM // GRID_W
NA_SLAB_ROWS = NA_WIN_H + NA_ROWS_PER_TILE


def _softmax_pv(s_list, v_list):
    m = s_list[0].max(axis=-1, keepdims=True)
    for s in s_list[1:]:
        m = jnp.maximum(m, s.max(axis=-1, keepdims=True))
    acc, den = None, None
    for s, v in zip(s_list, v_list):
        p = jnp.exp(s - m)
        l = p.sum(axis=-1, keepdims=True)
        o = _dot(p.astype(BF16), v)
        acc = o if acc is None else acc + o
        den = l if den is None else den + l
    return acc / den


def _na_kernel(q_ref, k_ref, v_ref, bias_ref, o_ref, *, n_ctx, rows, n_ctx_tiles):
    t = pl.program_id(1)
    n_loc = NA_SLAB_ROWS * GRID_W
    lane = lax.broadcasted_iota(jnp.int32, (1, LANES), 1)
    low = lane < NA_HEAD_DIM
    zero = jnp.zeros((), BF16)

    def pair(p, keys):
        ls = slice(p * LANES, (p + 1) * LANES)
        q2 = q_ref[:, ls]
        outs = []
        for e in range(2):
            qm = jnp.where(low if e == 0 else jnp.logical_not(low), q2, zero)
            s_list, v_list = [], []
            for kk, vv, bias in keys(ls, 2 * p + e):
                s = _dot_nt(qm, kk)
                s_list.append(s if bias is None else s + bias)
                v_list.append(vv)
            outs.append(_softmax_pv(s_list, v_list))
        o_ref[:, ls] = jnp.where(low, outs[0], outs[1])

    @pl.when(t < n_ctx_tiles)
    def _ctx():
        for p in range(NA_HEADS // 2):
            pair(p, lambda ls, h: [(k_ref[0:n_ctx, ls], v_ref[0:n_ctx, ls], None)])

    @pl.when(t >= n_ctx_tiles)
    def _latent():
        r0 = (t - n_ctx_tiles) * NA_ROWS_PER_TILE
        rs0 = jnp.clip(r0 - NA_WIN_H // 2, 0, rows - NA_SLAB_ROWS)
        var = (r0 - rs0) // NA_ROWS_PER_TILE
        start = pl.multiple_of(n_ctx + rs0 * GRID_W, GRID_W)
        for p in range(NA_HEADS // 2):
            pair(p, lambda ls, h: [(k_ref[pl.ds(start, n_loc), ls], v_ref[pl.ds(start, n_loc), ls], bias_ref[var, h]),
                                   (k_ref[0:n_ctx, ls], v_ref[0:n_ctx, ls], None)])


def _na_bias_table(rpb, rows):
    rpt, slab = NA_ROWS_PER_TILE, NA_SLAB_ROWS
    cols = jnp.arange(GRID_W)
    cs = jnp.clip(cols - NA_WIN_W // 2, 0, GRID_W - NA_WIN_W)
    kc = jnp.arange(GRID_W)
    in_win = (kc[None, :] >= cs[:, None]) & (kc[None, :] < cs[:, None] + NA_WIN_W)
    col_off = kc[None, :] - cols[:, None] + NA_WIN_W - 1
    diff = (jnp.arange(3) * rpt)[:, None, None]
    i = jnp.arange(rpt)[None, :, None]
    a = jnp.arange(slab)[None, None, :]
    w0 = jnp.clip(diff + i - NA_WIN_H // 2, 0, slab - NA_WIN_H)
    row_ok = (a >= w0) & (a < w0 + NA_WIN_H)
    row_off = a - diff - i + NA_WIN_H - 1
    row_sel = ((row_off[..., None] == jnp.arange(2 * NA_WIN_H - 1)) & row_ok[..., None]).astype(F32)
    col_sel = (col_off[:, :, None] == jnp.arange(2 * NA_WIN_W - 1)).astype(F32)
    tab = jnp.einsum('hrc,viar,qkc->vhiqak', rpb.astype(F32), row_sel, col_sel, precision=lax.Precision.HIGHEST)
    ok = row_ok[:, None, :, None, :, None] & in_win[None, None, None, :, None, :]
    tab = jnp.where(ok, tab, NEG)
    return tab.reshape(3, NA_HEADS, rpt * GRID_W, slab * GRID_W)


def _na(q, k, v, rpb, n_ctx):
    bsz, t, w = q.shape
    rows = (t - n_ctx) // GRID_W
    assert rows >= NA_SLAB_ROWS and rows % NA_ROWS_PER_TILE == 0
    bias = _na_bias_table(rpb, rows)
    return pl.pallas_call(
        functools.partial(_na_kernel, n_ctx=n_ctx, rows=rows, n_ctx_tiles=n_ctx // TM),
        grid=(bsz, t // TM),
        in_specs=[_tok_spec(w),
                  pl.BlockSpec((None, t, w), lambda b, i: (b, 0, 0)),
                  pl.BlockSpec((None, t, w), lambda b, i: (b, 0, 0)),
                  _const_spec(bias.shape)],
        out_specs=_tok_spec(w),
        out_shape=jax.ShapeDtypeStruct((bsz, t, w), F32),
        compiler_params=_cparams(("parallel", "arbitrary")),
        name="natten",
    )(q, k, v, bias)


def _mlprep_kernel(xm_ref, prev_ref, next_ref, cw_ref, cb_ref, wq_ref, wk_ref, q_ref, k_ref,
                   *, n_ctx_tiles, n_tiles):
    t = pl.program_id(1)
    has_prev = jnp.logical_and(t != 0, t != n_ctx_tiles)
    has_next = jnp.logical_and(t != n_ctx_tiles - 1, t != n_tiles - 1)
    prev = jnp.where(has_prev, prev_ref[...], 0.0)
    nxt = jnp.where(has_next, next_ref[...], 0.0)
    ext = jnp.concatenate([prev, xm_ref[...], nxt], axis=0)
    acc = cb_ref[...]
    for j in range(ML_CONV):
        off = SUBLANES + j - ML_CONV // 2
        acc = acc + cw_ref[j:j + 1, :] * ext[off:off + TM, :]
    xc = _silu(acc).astype(BF16)
    scale = ML_HEAD_DIM ** -0.5
    for h in range(ML_HEADS):
        hs = slice(h * ML_HEAD_DIM, (h + 1) * ML_HEAD_DIM)
        q_ref[:, hs] = _dot(xc[:, hs], wq_ref[h]).astype(BF16)
        k_ref[:, hs] = (_dot(xc[:, hs], wk_ref[h]) * scale).astype(BF16)


def _mlprep(xm, conv_w, conv_b, wq, wk, n_ctx):
    bsz, t, w = xm.shape
    nt = t // TM
    per = TM // SUBLANES
    last = t // SUBLANES - 1
    return pl.pallas_call(
        functools.partial(_mlprep_kernel, n_ctx_tiles=n_ctx // TM, n_tiles=nt),
        grid=(bsz, nt),
        in_specs=[_tok_spec(w),
                  pl.BlockSpec((None, SUBLANES, w), lambda b, i: (b, jnp.maximum(i * per - 1, 0), 0)),
                  pl.BlockSpec((None, SUBLANES, w), lambda b, i: (b, jnp.minimum((i + 1) * per, last), 0)),
                  _const_spec(conv_w.shape), _const_spec((1, w)),
                  _const_spec(wq.shape), _const_spec(wk.shape)],
        out_specs=[_tok_spec(w), _tok_spec(w)],
        out_shape=[jax.ShapeDtypeStruct((bsz, t, w), BF16)] * 2,
        compiler_params=_cparams(("parallel", "parallel")),
        name="mlstm_prep",
    )(xm, xm, xm, conv_w, conv_b.reshape(1, w), wq, wk)


def _scan_specs(width, n_chunks, n_ctx_chunks):
    fwd = pl.BlockSpec((None, CHUNK, width), lambda b, i: (b, i, 0))
    bwd = pl.BlockSpec(
        (None, CHUNK, width),
        lambda b, i: (b, jnp.where(i < n_ctx_chunks, n_ctx_chunks - 1 - i, n_chunks + n_ctx_chunks - 1 - i), 0))
    return fwd, bwd


def _split3(x):
    hi = x.astype(BF16)
    r1 = x - hi.astype(F32)
    mid = r1.astype(BF16)
    lo = (r1 - mid.astype(F32)).astype(BF16)
    return hi, mid, lo


def _mlscan_kernel(qf_ref, kf_ref, vf_ref, gf_ref, qb_ref, kb_ref, vb_ref, gb_ref, bias_ref,
                   of_ref, ob_ref, st_ref, m_ref):
    i = pl.program_id(1)
    dk = ML_HEAD_DIM

    @pl.when(i == 0)
    def _init():
        st_ref[...] = jnp.zeros_like(st_ref)
        m_ref[...] = jnp.zeros_like(m_ref)

    tt = lax.broadcasted_iota(jnp.int32, (CHUNK, CHUNK), 0)
    ss = lax.broadcasted_iota(jnp.int32, (CHUNK, CHUNK), 1)
    lane = lax.broadcasted_iota(jnp.int32, (CHUNK, LANES), 1)
    ones_col = (lane == 0).astype(BF16)

    for d, (q_ref, k_ref, v_ref, g_ref, o_ref) in enumerate(
            ((qf_ref, kf_ref, vf_ref, gf_ref, of_ref), (qb_ref, kb_ref, vb_ref, gb_ref, ob_ref))):
        causal = (ss <= tt) if d == 0 else (ss >= tt)
        tri = causal.astype(BF16)
        g = g_ref[...] + bias_ref[...]
        ig = g[:, :LANES]
        lf = _log_sigmoid(g[:, LANES:])
        hi, mid, lo = _split3(lf)
        bc = _dot(tri, hi) + _dot(tri, mid) + _dot(tri, lo)
        e = ig - bc
        e_t = e.T
        last = CHUNK - 1 if d == 0 else 0
        for h in range(ML_HEADS):
            c = d * ML_HEADS + h
            hs = slice(h * dk, (h + 1) * dk)
            q, k = q_ref[:, hs], k_ref[:, hs]
            v_aug = jnp.concatenate([v_ref[:, hs], ones_col], axis=-1)
            m_prev = m_ref[c:c + 1, 0:1]
            b_col = bc[:, c:c + 1]
            e_col = e[:, c:c + 1]
            dlog = jnp.where(causal, b_col + e_t[c:c + 1, :], -jnp.inf)
            m_t = jnp.maximum(b_col + m_prev, dlog.max(axis=-1, keepdims=True))
            s = _dot_nt(q, k) * jnp.exp(dlog - m_t)
            w_inter = jnp.exp(b_col + m_prev - m_t)
            state = st_ref[c]
            res = _dot(s.astype(BF16), v_aug) + w_inter * _dot(q, state.astype(BF16))
            den = res[:, dk:dk + 1]
            o_ref[:, hs] = res[:, :dk] / jnp.maximum(jnp.abs(den), jnp.exp(-m_t))
            b_last = bc[last:last + 1, c:c + 1]
            w_log = b_last + e_col
            m_new = jnp.maximum(b_last + m_prev, w_log.max(axis=0, keepdims=True))
            w_in = jnp.exp(w_log - m_new)
            st_ref[c] = jnp.exp(b_last + m_prev - m_new) * state + _dot_tn(k, (v_aug * w_in).astype(BF16))
            m_ref[c:c + 1, :] = jnp.broadcast_to(m_new, (1, LANES))


def _mlscan(q, k, v, gates, gate_bias, n_ctx):
    bsz, t, w = q.shape
    nc, ncc = t // CHUNK, n_ctx // CHUNK
    f, bk = _scan_specs(w, nc, ncc)
    gf, gb = _scan_specs(2 * LANES, nc, ncc)
    n_chain = 2 * ML_HEADS
    return pl.pallas_call(
        _mlscan_kernel,
        grid=(bsz, nc),
        in_specs=[f, f, f, gf, bk, bk, bk, gb, _const_spec((1, 2 * LANES))],
        out_specs=[f, bk],
        out_shape=[jax.ShapeDtypeStruct((bsz, t, w), F32)] * 2,
        scratch_shapes=[pltpu.VMEM((n_chain, ML_HEAD_DIM, 2 * ML_HEAD_DIM), F32),
                        pltpu.VMEM((SUBLANES, LANES), F32)],
        compiler_params=_cparams(("parallel", "arbitrary")),
        name="mlstm_scan",
    )(q, k, v, gates, q, k, v, gates, gate_bias)


def _retscan_kernel(qf_ref, kf_ref, vf_ref, qb_ref, kb_ref, vb_ref, logit_ref, of_ref, ob_ref, st_ref):
    i = pl.program_id(1)
    dk = RET_HEAD_DIM

    @pl.when(i == 0)
    def _init():
        st_ref[...] = jnp.zeros_like(st_ref)

    tt = lax.broadcasted_iota(jnp.int32, (CHUNK, CHUNK), 0)
    ss = lax.broadcasted_iota(jnp.int32, (CHUNK, CHUNK), 1)
    pos = lax.broadcasted_iota(jnp.int32, (CHUNK, 1), 0).astype(F32)
    log_g = _log_sigmoid(logit_ref[...])

    for d, (q_ref, k_ref, v_ref, o_ref) in enumerate(
            ((qf_ref, kf_ref, vf_ref, of_ref), (qb_ref, kb_ref, vb_ref, ob_ref))):
        dist = (tt - ss) if d == 0 else (ss - tt)
        distf = jnp.maximum(dist, 0).astype(F32)
        step = pos if d == 0 else (CHUNK - 1.0) - pos
        for h in range(RET_HEADS):
            c = d * RET_HEADS + h
            hs = slice(h * dk, (h + 1) * dk)
            lg = log_g[c:c + 1, 0:1]
            dmat = jnp.where(dist >= 0, jnp.exp(distf * lg), 0.0)
            q, k, v = q_ref[:, hs], k_ref[:, hs], v_ref[:, hs]
            state = st_ref[c]
            inner = _dot((_dot_nt(q, k) * dmat).astype(BF16), v)
            cross = jnp.exp((step + 1.0) * lg) * _dot(q, state.astype(BF16))
            o_ref[:, hs] = inner + cross
            k_dec = jnp.exp((CHUNK - 1.0 - step) * lg)
            st_ref[c] = jnp.exp(CHUNK * lg) * state + _dot_tn(k, (v.astype(F32) * k_dec).astype(BF16))


def _retscan(q, k, v, decay_logit, n_ctx):
    bsz, t, w = q.shape
    nc, ncc = t // CHUNK, n_ctx // CHUNK
    f, bk = _scan_specs(w, nc, ncc)
    logit = jnp.broadcast_to(decay_logit.astype(F32).reshape(2 * RET_HEADS, 1), (2 * RET_HEADS, LANES))
    return pl.pallas_call(
        _retscan_kernel,
        grid=(bsz, nc),
        in_specs=[f, f, f, bk, bk, bk, _const_spec((2 * RET_HEADS, LANES))],
        out_specs=[f, bk],
        out_shape=[jax.ShapeDtypeStruct((bsz, t, w), F32)] * 2,
        scratch_shapes=[pltpu.VMEM((2 * RET_HEADS, RET_HEAD_DIM, RET_HEAD_DIM), F32)],
        compiler_params=_cparams(("parallel", "arbitrary")),
        name="retention_scan",
    )(q, k, v, q, k, v, logit)


def _s5_weights(lam_re, lam_im, log_dt, b_re, b_im, c_re, c_im):
    hp = lax.Precision.HIGHEST
    n = S5_SUB
    dt = jnp.exp(log_dt)[..., None]
    zr, zi = lam_re * dt, lam_im * dt
    steps = jnp.arange(n + 1, dtype=F32)[:, None, None, None]
    pmag = jnp.exp(steps * zr)
    ak_re, ak_im = pmag * jnp.cos(steps * zi), pmag * jnp.sin(steps * zi)
    a_re, a_im = ak_re[1], ak_im[1]
    lam_sq = jnp.square(lam_re) + jnp.square(lam_im)
    e_re = ((a_re - 1.0) * lam_re + a_im * lam_im) / lam_sq
    e_im = (a_im * lam_re - (a_re - 1.0) * lam_im) / lam_sq
    bb_re = e_re[..., None] * b_re - e_im[..., None] * b_im
    bb_im = e_re[..., None] * b_im + e_im[..., None] * b_re
    ab_re = ak_re[..., None] * bb_re - ak_im[..., None] * bb_im
    ab_im = ak_re[..., None] * bb_im + ak_im[..., None] * bb_re
    kern = (jnp.einsum('dgqp,kdgpr->kdgqr', c_re, ab_re, precision=hp)
            - jnp.einsum('dgqp,kdgpr->kdgqr', c_im, ab_im, precision=hp))
    sig = jnp.arange(n)
    lag = sig[None, :] - sig[:, None]
    toe = jnp.where((lag >= 0)[:, :, None, None, None, None],
                    kern[jnp.clip(lag, 0, n)], 0.0)
    toe = toe.transpose(2, 3, 0, 5, 1, 4)
    vin_re = ab_re[n - 1 - sig].transpose(1, 2, 0, 4, 3)
    vin_im = ab_im[n - 1 - sig].transpose(1, 2, 0, 4, 3)
    ap_re, ap_im = ak_re[1:], ak_im[1:]
    w_re = c_re[None] * ap_re[:, :, :, None, :] - c_im[None] * ap_im[:, :, :, None, :]
    w_im = -(c_re[None] * ap_im[:, :, :, None, :] + c_im[None] * ap_re[:, :, :, None, :])
    w_re = w_re.transpose(1, 2, 4, 0, 3)
    w_im = w_im.transpose(1, 2, 4, 0, 3)

    def orient(a, axes):
        return jnp.stack([a[0], jnp.flip(a[1], axis=tuple(x - 1 for x in axes))], axis=0)

    toe = orient(toe, (2, 4))
    vin_re, vin_im = orient(vin_re, (2,)), orient(vin_im, (2,))
    w_re, w_im = orient(w_re, (3,)), orient(w_im, (3,))
    g = lam_re.shape[1]
    nq = n * S5_GROUP
    toe = toe.reshape(2, g, nq, nq)
    vin_re, vin_im = vin_re.reshape(2, g, nq, S5_STATE), vin_im.reshape(2, g, nq, S5_STATE)
    tv = jnp.concatenate([toe, vin_re, vin_im, vin_im, vin_re], axis=-1).astype(BF16)
    w = jnp.concatenate([w_re.reshape(2, g, S5_STATE, nq), w_im.reshape(2, g, S5_STATE, nq)], axis=2).astype(BF16)
    an_re, an_im = ak_re[n], ak_im[n]
    a1 = jnp.concatenate([an_re, an_re], axis=-1)[:, :, None, :]
    a2 = jnp.concatenate([-an_im, an_im], axis=-1)[:, :, None, :]
    return tv, w, a1, a2


def _s5_kernel(u_ref, tv_ref, w_ref, a1_ref, a2_ref, y_ref, in_ref, xp_ref, *, bsz, n_sub, n_ctx_sub):
    nq = S5_SUB * S5_GROUP
    ns = 2 * S5_STATE
    u = u_ref[...]
    y = None
    for d in range(2):
        r = _dot(u, tv_ref[d])
        in_ref[d] = r[:, nq:]
        y = r[:, :nq] if y is None else y + r[:, :nq]
    a1f, a2f, a1b, a2b = a1_ref[0], a2_ref[0], a1_ref[1], a2_ref[1]

    def body(i, carry):
        xf, xsf, xb, xsb = carry
        rf = pl.multiple_of(i * bsz, bsz)
        jb = jnp.where(i < n_ctx_sub, n_ctx_sub - 1 - i, n_sub + n_ctx_sub - 1 - i)
        rb = pl.multiple_of(jb * bsz, bsz)
        xp_ref[0, pl.ds(rf, bsz), :] = xf
        xp_ref[1, pl.ds(rb, bsz), :] = xb
        inf = in_ref[0, pl.ds(rf, bsz), :]
        inb = in_ref[1, pl.ds(rb, bsz), :]
        nxf = a1f * xf + a2f * xsf + inf[:, :ns]
        nxsf = a1f * xsf - a2f * xf + inf[:, ns:]
        nxb = a1b * xb + a2b * xsb + inb[:, :ns]
        nxsb = a1b * xsb - a2b * xb + inb[:, ns:]
        return nxf, nxsf, nxb, nxsb

    z = jnp.zeros((bsz, ns), F32)
    lax.fori_loop(0, n_sub, body, (z, z, z, z))
    for d in range(2):
        y = y + _dot(xp_ref[d].astype(BF16), w_ref[d])
    y_ref[...] = y


def _s5(u, tv, w, a1, a2, n_ctx):
    bsz, t, width = u.shape
    g = S5_GROUPS
    n_sub = t // S5_SUB
    nq = S5_SUB * S5_GROUP
    rows = n_sub * bsz
    ug = (u.astype(BF16).reshape(bsz, n_sub, S5_SUB, g, S5_GROUP)
          .transpose(3, 1, 0, 2, 4).reshape(g, rows, nq))
    yg = pl.pallas_call(
        functools.partial(_s5_kernel, bsz=bsz, n_sub=n_sub, n_ctx_sub=n_ctx // S5_SUB),
        grid=(g,),
        in_specs=[pl.BlockSpec((None, rows, nq), lambda i: (i, 0, 0)),
                  pl.BlockSpec((2, None, nq, 2 * nq), lambda i: (0, i, 0, 0)),
                  pl.BlockSpec((2, None, 2 * S5_STATE, nq), lambda i: (0, i, 0, 0)),
                  pl.BlockSpec((2, None, 1, 2 * S5_STATE), lambda i: (0, i, 0, 0)),
                  pl.BlockSpec((2, None, 1, 2 * S5_STATE), lambda i: (0, i, 0, 0))],
        out_specs=pl.BlockSpec((None, rows, nq), lambda i: (i, 0, 0)),
        out_shape=jax.ShapeDtypeStruct((g, rows, nq), F32),
        scratch_shapes=[pltpu.VMEM((2, rows, nq), F32), pltpu.VMEM((2, rows, 2 * S5_STATE), F32)],
        compiler_params=_cparams(("parallel",)),
        name="s5",
    )(ug, tv, w, a1, a2)
    return (yg.reshape(g, n_sub, bsz, S5_SUB, S5_GROUP)
            .transpose(2, 1, 3, 0, 4).reshape(bsz, t, width))


def _out_even_kernel(x_ref, mod_ref, na_ref, hf_ref, hb_ref, zo_ref, gn_ref, w_ref, lng_ref, lnb_ref,
                     o_ref, *, alpha):
    ml = _sigmoid(zo_ref[...]) * _head_norm(hf_ref[...] + hb_ref[...], gn_ref[...], ML_HEADS, ML_HEAD_DIM)
    y = _dot(na_ref[...].astype(BF16), w_ref[0:HALF, :]) + _dot(ml.astype(BF16), w_ref[HALF:, :])
    r = alpha * x_ref[...] + _mod_row(mod_ref, 1, 2) * y
    o_ref[...] = _layer_norm(r, lng_ref[...], lnb_ref[...])


def _out_odd_kernel(x_ref, mod_ref, ys_ref, u_ref, dsk_ref, gw_ref, gb_ref, rf_ref, rb_ref, gr_ref, gn_ref,
                    w_ref, lng_ref, lnb_ref, o_ref, *, alpha):
    s = _gelu_tanh(ys_ref[...] + dsk_ref[...] * u_ref[...])
    s = s * _sigmoid(_dot(s.astype(BF16), gw_ref[...]) + gb_ref[...])
    ret = _silu(gr_ref[...]) * _head_norm(rf_ref[...] + rb_ref[...], gn_ref[...], RET_HEADS, RET_HEAD_DIM)
    y = _dot(s.astype(BF16), w_ref[0:HALF, :]) + _dot(ret.astype(BF16), w_ref[HALF:, :])
    r = alpha * x_ref[...] + _mod_row(mod_ref, 1, 2) * y
    o_ref[...] = _layer_norm(r, lng_ref[...], lnb_ref[...])


def _out_call(kernel, name, xa, mod, extra, alpha, n_ctx_tiles):
    bsz, t, d = xa.shape
    in_specs = [_tok_spec(d), _mod_block(d, bsz, n_ctx_tiles)]
    args = [xa, mod]
    for kind, a in extra:
        in_specs.append(_tok_spec(a.shape[-1]) if kind == 't' else _const_spec(a.shape))
        args.append(a)
    return pl.pallas_call(
        functools.partial(kernel, alpha=alpha),
        grid=(bsz, t // TM),
        in_specs=in_specs,
        out_specs=_tok_spec(d),
        out_shape=jax.ShapeDtypeStruct((bsz, t, d), F32),
        compiler_params=_cparams(("parallel", "parallel")),
        name=name,
    )(*args)


def _rope_tables(n_ctx, seq):
    nf = RET_HEAD_DIM // 4
    freqs = ROPE_BASE ** (-jnp.arange(nf, dtype=F32) / nf)
    tok = jnp.arange(seq)
    ang_r = (tok // GRID_W).astype(F32)[:, None] * freqs
    ang_c = (tok % GRID_W).astype(F32)[:, None] * freqs
    cos = jnp.concatenate([jnp.cos(ang_r)] * 2 + [jnp.cos(ang_c)] * 2, axis=-1)
    sin = jnp.concatenate([-jnp.sin(ang_r), jnp.sin(ang_r), -jnp.sin(ang_c), jnp.sin(ang_c)], axis=-1)
    cos = jnp.concatenate([jnp.ones((n_ctx, LANES), F32), cos], axis=0)
    sin = jnp.concatenate([jnp.zeros((n_ctx, LANES), F32), sin], axis=0)
    return cos, sin


def _even_layer_mixer(xa, mod, w_in, w_out, rpb, conv_w, conv_b, wq, wk, i_bias, f_bias, gn_w,
                      lng, lnb, alpha, n_ctx):
    d = xa.shape[-1]
    n_ctx_tiles = n_ctx // TM
    ng = 2 * ML_HEADS
    main = 6 * HALF
    w_gates = jnp.zeros((d, 2 * LANES), F32)
    w_gates = w_gates.at[:, 0:ng].set(w_in[:, main:main + ng]).at[:, LANES:LANES + ng].set(w_in[:, main + ng:])
    w_all = jnp.concatenate([w_in[:, :main], w_gates], axis=1).astype(BF16)
    gate_bias = jnp.zeros((1, 2 * LANES), F32)
    gate_bias = gate_bias.at[0, 0:ng].set(i_bias.reshape(ng)).at[0, LANES:LANES + ng].set(f_bias.reshape(ng))
    outs = [(0, HALF, NA_HEAD_DIM ** -0.5, False), (HALF, HALF, 1.0, False), (2 * HALF, HALF, 1.0, False),
            (3 * HALF, HALF, 1.0, False), (4 * HALF, HALF, 1.0, False), (5 * HALF, HALF, 1.0, False),
            (main, 2 * LANES, 1.0, False)]
    dts = [BF16, BF16, BF16, F32, BF16, F32, F32]
    q_na, k_na, v_na, xm, zv, zo, gates = _inproj(xa, mod, w_all, outs, dts, n_ctx_tiles)
    na = _na(q_na, k_na, v_na, rpb, n_ctx)
    q_ml, k_ml = _mlprep(xm, conv_w, conv_b, wq.astype(BF16), wk.astype(BF16), n_ctx)
    hf, hb = _mlscan(q_ml, k_ml, zv, gates, gate_bias, n_ctx)
    return _out_call(
        _out_even_kernel, "out_even", xa, mod,
        [('t', na), ('t', hf), ('t', hb), ('t', zo), ('c', gn_w.reshape(1, HALF)), ('c', w_out.astype(BF16)),
         ('c', lng.reshape(1, d)), ('c', lnb.reshape(1, d))], alpha, n_ctx_tiles)


def _odd_layer_mixer(xa, mod, w_in, w_out, s5_params, d_skip, glu_w, glu_b, decay_logit, gn_w,
                     lng, lnb, alpha, n_ctx, rope_tabs):
    d = xa.shape[-1]
    n_ctx_tiles = n_ctx // TM
    outs = [(0, HALF, 1.0, False), (HALF, HALF, 1.0, True), (2 * HALF, HALF, RET_HEAD_DIM ** -0.5, True),
            (3 * HALF, HALF, 1.0, False), (4 * HALF, HALF, 1.0, False)]
    dts = [F32, BF16, BF16, BF16, F32]
    u, q_r, k_r, v_r, g_r = _inproj(xa, mod, w_in.astype(BF16), outs, dts, n_ctx_tiles, rope_tabs)
    ys = _s5(u, *_s5_weights(*s5_params), n_ctx)
    rf, rb = _retscan(q_r, k_r, v_r, decay_logit, n_ctx)
    return _out_call(
        _out_odd_kernel, "out_odd", xa, mod,
        [('t', ys), ('t', u), ('c', d_skip.reshape(1, HALF)), ('c', glu_w.astype(BF16)),
         ('c', glu_b.reshape(1, HALF)), ('t', rf), ('t', rb), ('t', g_r), ('c', gn_w.reshape(1, HALF)),
         ('c', w_out.astype(BF16)), ('c', lng.reshape(1, d)), ('c', lnb.reshape(1, d))],
        alpha, n_ctx_tiles)


def kernel(x, c, ctx, c_ctx, ada_w, ada_b, ffn_w_gate, ffn_w_up, ffn_w_down, ln_g, ln_b, ev_w_in, ev_w_out, na_rpb, ml_conv_w, ml_conv_b, ml_wq, ml_wk, ml_i_bias, ml_f_bias, ml_gn_w, od_w_in, od_w_out, s5_lam_re, s5_lam_im, s5_log_dt, s5_b_re, s5_b_im, s5_c_re, s5_c_im, s5_d, s5_glu_w, s5_glu_b, ret_decay_logit, ret_gn_w):
    bsz, seq, d = x.shape
    n_ctx = ctx.shape[1]
    depth = ada_w.shape[0]
    assert n_ctx % TM == 0 and seq % TM == 0 and seq % GRID_W == 0 and bsz == SUBLANES
    n_ctx_tiles = n_ctx // TM
    alpha = (2.0 * depth) ** 0.25

    rows = bsz + SUBLANES
    c_rows = jnp.zeros((rows, d), F32).at[:bsz].set(c).at[bsz].set(c_ctx)
    mod_all = _mod_table(c_rows, ada_w, ada_b)

    xa = x
    rope_tabs = _rope_tables(n_ctx, seq)
    for l in range(depth):
        e = l // 2
        mod = mod_all[l]
        ffn = lambda j, k, h, **kw: _ffn(h, mod, j, ffn_w_gate[l, k].astype(BF16), ffn_w_up[l, k].astype(BF16),
                                         ffn_w_down[l, k].astype(BF16), ln_g[l, j], ln_b[l, j], alpha,
                                         n_ctx_tiles, **kw)
        xa = ffn(0, 0, xa, ctx=ctx) if l == 0 else ffn(0, 0, xa)
        if l % 2 == 0:
            xa = _even_layer_mixer(xa, mod, ev_w_in[e], ev_w_out[e], na_rpb[e], ml_conv_w[e], ml_conv_b[e],
                                   ml_wq[e], ml_wk[e], ml_i_bias[e], ml_f_bias[e], ml_gn_w[e],
                                   ln_g[l, 1], ln_b[l, 1], alpha, n_ctx)
        else:
            s5_params = (s5_lam_re[e], s5_lam_im[e], s5_log_dt[e], s5_b_re[e], s5_b_im[e], s5_c_re[e], s5_c_im[e])
            xa = _odd_layer_mixer(xa, mod, od_w_in[e], od_w_out[e], s5_params, s5_d[e], s5_glu_w[e], s5_glu_b[e],
                                  ret_decay_logit[e], ret_gn_w[e], ln_g[l, 1], ln_b[l, 1], alpha, n_ctx, rope_tabs)
        xa = ffn(2, 1, xa, latent_only=(l == depth - 1))
    return xa
```

```python
import functools
import math

import jax
import jax.numpy as jnp
from jax import lax
from jax.experimental import pallas as pl
from jax.experimental.pallas import tpu as pltpu

F32 = jnp.float32
BF16 = jnp.bfloat16

GRID_W = 64
LN_EPS = 1e-5
N_MOD = 9
NA_HEADS, NA_HEAD_DIM, NA_WIN_H, NA_WIN_W = 8, 64, 8, 16
ML_HEADS, ML_HEAD_DIM, ML_CONV = 4, 128, 5
S5_GROUP, S5_GROUPS, S5_STATE = 16, 32, 64
RET_HEADS, RET_HEAD_DIM = 4, 128
ROPE_BASE = 10000.0
HALF = 512

LANES = 128
SUBLANES = 8
VMEM_LIMIT = 56 * 1024 * 1024

TM = 256
CHUNK = 256
S5_SUB = 8
S5_OCT = LANES // S5_GROUP
NEG = -1e30


def _cparams(sem):
    return pltpu.CompilerParams(dimension_semantics=sem, vmem_limit_bytes=VMEM_LIMIT)


def _const_spec(shape):
    nd = len(shape)
    return pl.BlockSpec(shape, lambda *_: (0,) * nd, pipeline_mode=pl.Buffered(1))


def _dot(a, b):
    return jnp.dot(a, b, preferred_element_type=F32)


def _dot_nt(a, b):
    return lax.dot_general(a, b, (((1,), (1,)), ((), ())), preferred_element_type=F32)


def _dot_tn(a, b):
    return lax.dot_general(a, b, (((0,), (0,)), ((), ())), preferred_element_type=F32)


def _sigmoid(x):
    return 1.0 / (1.0 + jnp.exp(-x))


def _silu(x):
    return x * _sigmoid(x)


def _log_sigmoid(x):
    return jnp.minimum(x, 0.0) - jnp.log1p(jnp.exp(-jnp.abs(x)))


def _gelu_tanh(x):
    return 0.5 * x * (1.0 + jnp.tanh(math.sqrt(2.0 / math.pi) * (x + 0.044715 * (x * x * x))))


def _layer_norm(r, g, b):
    mu = jnp.mean(r, axis=-1, keepdims=True)
    c = r - mu
    var = jnp.mean(c * c, axis=-1, keepdims=True)
    return c * lax.rsqrt(var + LN_EPS) * g + b


def _head_norm(h, w, n_heads, head_dim):
    parts = []
    for k in range(n_heads):
        hk = h[:, k * head_dim:(k + 1) * head_dim]
        mu = jnp.mean(hk, axis=-1, keepdims=True)
        c = hk - mu
        var = jnp.mean(c * c, axis=-1, keepdims=True)
        parts.append(c * lax.rsqrt(var + LN_EPS))
    return jnp.concatenate(parts, axis=-1) * w


def _mod_row(mod_ref, j, k):
    i = 3 * j + k
    return mod_ref[i:i + 1, :]


def _mod_kernel(c_ref, w_ref, b_ref, o_ref):
    c = c_ref[...]
    s = _silu(c).astype(BF16)
    o_ref[...] = _dot(s, w_ref[...].astype(BF16)) + b_ref[...]


def _mod_table(c_rows, ada_w, ada_b):
    depth, d, n = ada_w.shape
    rows = c_rows.shape[0]
    tn = 1024
    out = pl.pallas_call(
        _mod_kernel,
        grid=(depth, n // tn),
        in_specs=[pl.BlockSpec((rows, d), lambda l, j: (0, 0)),
                  pl.BlockSpec((None, d, tn), lambda l, j: (l, 0, j)),
                  pl.BlockSpec((None, 1, tn), lambda l, j: (l, 0, j))],
        out_specs=pl.BlockSpec((None, rows, tn), lambda l, j: (l, 0, j)),
        out_shape=jax.ShapeDtypeStruct((depth, rows, n), F32),
        compiler_params=_cparams(("parallel", "parallel")),
        name="mod_table",
    )(c_rows, ada_w, ada_b.reshape(depth, 1, n))
    return out.reshape(depth, rows, N_MOD, d)


def _tok_spec(width, tm=TM):
    return pl.BlockSpec((None, tm, width), lambda b, t: (b, t, 0))


def _mod_block(d, batch, n_ctx_tiles):
    return pl.BlockSpec((None, N_MOD, d), lambda b, t: (jnp.where(t < n_ctx_tiles, batch, b), 0, 0))


def _ffn_kernel(*refs, j, alpha, n_ctx_tiles, split_in):
    if split_in:
        ctx_ref, x_ref = refs[:2]
        refs = refs[1:]
        x = jnp.where(pl.program_id(1) < n_ctx_tiles, ctx_ref[...], x_ref[...])
    else:
        x = refs[0][...]
    _, mod_ref, wg_ref, wu_ref, wd_ref, lng_ref, lnb_ref, o_ref = refs
    h = (x * (1.0 + _mod_row(mod_ref, j, 1)) + _mod_row(mod_ref, j, 0)).astype(BF16)
    g = _dot(h, wg_ref[...])
    u = _dot(h, wu_ref[...])
    a = (_silu(g) * u).astype(BF16)
    y = _dot(a, wd_ref[...])
    r = alpha * x + (0.5 * _mod_row(mod_ref, j, 2)) * y
    o_ref[...] = _layer_norm(r, lng_ref[...], lnb_ref[...])


def _ffn(xa, mod, j, wg, wu, wd, lng, lnb, alpha, n_ctx_tiles, ctx=None, latent_only=False):
    bsz, t, d = xa.shape
    dff = wg.shape[1]
    nct = n_ctx_tiles
    weights = [_const_spec((d, dff)), _const_spec((d, dff)), _const_spec((dff, d)),
               _const_spec((1, d)), _const_spec((1, d))]
    wargs = (wg, wu, wd, lng.reshape(1, d), lnb.reshape(1, d))
    if ctx is not None:
        t_out = t + ctx.shape[1]
        grid = (bsz, t_out // TM)
        in_specs = [pl.BlockSpec((None, TM, d), lambda b, i: (b, jnp.minimum(i, nct - 1), 0)),
                    pl.BlockSpec((None, TM, d), lambda b, i: (b, jnp.maximum(i - nct, 0), 0)),
                    _mod_block(d, bsz, nct)]
        args = (ctx, xa, mod)
        out_spec = _tok_spec(d)
    elif latent_only:
        t_out = t - nct * TM
        grid = (bsz, t_out // TM)
        in_specs = [pl.BlockSpec((None, TM, d), lambda b, i: (b, i + nct, 0)),
                    pl.BlockSpec((None, N_MOD, d), lambda b, i: (b, 0, 0))]
        args = (xa, mod)
        out_spec = _tok_spec(d)
    else:
        t_out = t
        grid = (bsz, t // TM)
        in_specs = [_tok_spec(d), _mod_block(d, bsz, nct)]
        args = (xa, mod)
        out_spec = _tok_spec(d)
    return pl.pallas_call(
        functools.partial(_ffn_kernel, j=j, alpha=alpha, n_ctx_tiles=nct, split_in=ctx is not None),
        grid=grid,
        in_specs=in_specs + weights,
        out_specs=out_spec,
        out_shape=jax.ShapeDtypeStruct((bsz, t_out, d), F32),
        compiler_params=_cparams(("parallel", "parallel")),
        name="ffn",
    )(*args, *wargs)


def _rope(z, cos, sin):
    lane = lax.broadcasted_iota(jnp.int32, (1, LANES), 1)
    first = (lane % 64) < 32
    parts = []
    for k in range(z.shape[1] // LANES):
        zk = z[:, k * LANES:(k + 1) * LANES]
        partner = jnp.where(first, pltpu.roll(zk, LANES - 32, axis=1), pltpu.roll(zk, 32, axis=1))
        parts.append(zk * cos + partner * sin)
    return jnp.concatenate(parts, axis=-1)


def _inproj_kernel(*refs, outs, use_rope):
    x_ref, mod_ref, w_ref = refs[:3]
    k = 3
    if use_rope:
        cos_ref, sin_ref = refs[3:5]
        k = 5
    out_refs = refs[k:]
    h = (x_ref[...] * (1.0 + _mod_row(mod_ref, 1, 1)) + _mod_row(mod_ref, 1, 0)).astype(BF16)
    for (off, width, scale, rope), o_ref in zip(outs, out_refs):
        z = _dot(h, w_ref[:, off:off + width])
        if rope:
            z = _rope(z, cos_ref[...], sin_ref[...])
        if scale != 1.0:
            z = z * scale
        o_ref[...] = z.astype(o_ref.dtype)


def _inproj(xa, mod, w, outs, dtypes, n_ctx_tiles, rope_tabs=None):
    bsz, t, d = xa.shape
    in_specs = [_tok_spec(d), _mod_block(d, bsz, n_ctx_tiles), _const_spec(w.shape)]
    args = [xa, mod, w]
    if rope_tabs is not None:
        in_specs += [pl.BlockSpec((TM, LANES), lambda b, t: (t, 0))] * 2
        args += list(rope_tabs)
    return pl.pallas_call(
        functools.partial(_inproj_kernel, outs=tuple(outs), use_rope=rope_tabs is not None),
        grid=(bsz, t // TM),
        in_specs=in_specs,
        out_specs=[_tok_spec(o[1]) for o in outs],
        out_shape=[jax.ShapeDtypeStruct((bsz, t, o[1]), dt) for o, dt in zip(outs, dtypes)],
        compiler_params=_cparams(("parallel", "parallel")),
        name="inproj",
    )(*args)


NA_ROWS_PER_TILE = TM // GRID_W
NA_SLAB_ROWS = NA_WIN_H + NA_ROWS_PER_TILE


def _softmax_pv(s_list, v_list):
    m = s_list[0].max(axis=-1, keepdims=True)
    for s in s_list[1:]:
        m = jnp.maximum(m, s.max(axis=-1, keepdims=True))
    acc, den = None, None
    for s, v in zip(s_list, v_list):
        p = jnp.exp(s - m)
        l = p.sum(axis=-1, keepdims=True)
        o = _dot(p.astype(BF16), v)
        acc = o if acc is None else acc + o
        den = l if den is None else den + l
    return acc / den


def _na_kernel(q_ref, k_ref, v_ref, bias_ref, o_ref, *, n_ctx, rows, n_ctx_tiles):
    t = pl.program_id(1)
    n_loc = NA_SLAB_ROWS * GRID_W
    lane = lax.broadcasted_iota(jnp.int32, (1, LANES), 1)
    low = lane < NA_HEAD_DIM
    zero = jnp.zeros((), BF16)

    def pair(p, keys):
        ls = slice(p * LANES, (p + 1) * LANES)
        q2 = q_ref[:, ls]
        outs = []
        for e in range(2):
            qm = jnp.where(low if e == 0 else jnp.logical_not(low), q2, zero)
            s_list, v_list = [], []
            for kk, vv, bias in keys(ls, 2 * p + e):
                s = _dot_nt(qm, kk)
                s_list.append(s if bias is None else s + bias)
                v_list.append(vv)
            outs.append(_softmax_pv(s_list, v_list))
        o_ref[:, ls] = jnp.where(low, outs[0], outs[1])

    @pl.when(t < n_ctx_tiles)
    def _ctx():
        for p in range(NA_HEADS // 2):
            pair(p, lambda ls, h: [(k_ref[0:n_ctx, ls], v_ref[0:n_ctx, ls], None)])

    @pl.when(t >= n_ctx_tiles)
    def _latent():
        r0 = (t - n_ctx_tiles) * NA_ROWS_PER_TILE
        rs0 = jnp.clip(r0 - NA_WIN_H // 2, 0, rows - NA_SLAB_ROWS)
        var = (r0 - rs0) // NA_ROWS_PER_TILE
        start = pl.multiple_of(n_ctx + rs0 * GRID_W, GRID_W)
        for p in range(NA_HEADS // 2):
            pair(p, lambda ls, h: [(k_ref[pl.ds(start, n_loc), ls], v_ref[pl.ds(start, n_loc), ls], bias_ref[var, h]),
                                   (k_ref[0:n_ctx, ls], v_ref[0:n_ctx, ls], None)])


def _na_bias_table(rpb, rows):
    rpt, slab = NA_ROWS_PER_TILE, NA_SLAB_ROWS
    cols = jnp.arange(GRID_W)
    cs = jnp.clip(cols - NA_WIN_W // 2, 0, GRID_W - NA_WIN_W)
    kc = jnp.arange(GRID_W)
    in_win = (kc[None, :] >= cs[:, None]) & (kc[None, :] < cs[:, None] + NA_WIN_W)
    col_off = kc[None, :] - cols[:, None] + NA_WIN_W - 1
    diff = (jnp.arange(3) * rpt)[:, None, None]
    i = jnp.arange(rpt)[None, :, None]
    a = jnp.arange(slab)[None, None, :]
    w0 = jnp.clip(diff + i - NA_WIN_H // 2, 0, slab - NA_WIN_H)
    row_ok = (a >= w0) & (a < w0 + NA_WIN_H)
    row_off = a - diff - i + NA_WIN_H - 1
    row_sel = ((row_off[..., None] == jnp.arange(2 * NA_WIN_H - 1)) & row_ok[..., None]).astype(F32)
    col_sel = (col_off[:, :, None] == jnp.arange(2 * NA_WIN_W - 1)).astype(F32)
    tab = jnp.einsum('hrc,viar,qkc->vhiqak', rpb.astype(F32), row_sel, col_sel, precision=lax.Precision.HIGHEST)
    ok = row_ok[:, None, :, None, :, None] & in_win[None, None, None, :, None, :]
    tab = jnp.where(ok, tab, NEG)
    return tab.reshape(3, NA_HEADS, rpt * GRID_W, slab * GRID_W)


def _na(q, k, v, rpb, n_ctx):
    bsz, t, w = q.shape
    rows = (t - n_ctx) // GRID_W
    assert rows >= NA_SLAB_ROWS and rows % NA_ROWS_PER_TILE == 0
    bias = _na_bias_table(rpb, rows)
    return pl.pallas_call(
        functools.partial(_na_kernel, n_ctx=n_ctx, rows=rows, n_ctx_tiles=n_ctx // TM),
        grid=(bsz, t // TM),
        in_specs=[_tok_spec(w),
                  pl.BlockSpec((None, t, w), lambda b, i: (b, 0, 0)),
                  pl.BlockSpec((None, t, w), lambda b, i: (b, 0, 0)),
                  _const_spec(bias.shape)],
        out_specs=_tok_spec(w),
        out_shape=jax.ShapeDtypeStruct((bsz, t, w), F32),
        compiler_params=_cparams(("parallel", "arbitrary")),
        name="natten",
    )(q, k, v, bias)


def _mlprep_kernel(xm_ref, prev_ref, next_ref, cw_ref, cb_ref, wq_ref, wk_ref, q_ref, k_ref,
                   *, n_ctx_tiles, n_tiles):
    t = pl.program_id(1)
    has_prev = jnp.logical_and(t != 0, t != n_ctx_tiles)
    has_next = jnp.logical_and(t != n_ctx_tiles - 1, t != n_tiles - 1)
    prev = jnp.where(has_prev, prev_ref[...], 0.0)
    nxt = jnp.where(has_next, next_ref[...], 0.0)
    ext = jnp.concatenate([prev, xm_ref[...], nxt], axis=0)
    acc = cb_ref[...]
    for j in range(ML_CONV):
        off = SUBLANES + j - ML_CONV // 2
        acc = acc + cw_ref[j:j + 1, :] * ext[off:off + TM, :]
    xc = _silu(acc).astype(BF16)
    scale = ML_HEAD_DIM ** -0.5
    for h in range(ML_HEADS):
        hs = slice(h * ML_HEAD_DIM, (h + 1) * ML_HEAD_DIM)
        q_ref[:, hs] = _dot(xc[:, hs], wq_ref[h]).astype(BF16)
        k_ref[:, hs] = (_dot(xc[:, hs], wk_ref[h]) * scale).astype(BF16)


def _mlprep(xm, conv_w, conv_b, wq, wk, n_ctx):
    bsz, t, w = xm.shape
    nt = t // TM
    per = TM // SUBLANES
    last = t // SUBLANES - 1
    return pl.pallas_call(
        functools.partial(_mlprep_kernel, n_ctx_tiles=n_ctx // TM, n_tiles=nt),
        grid=(bsz, nt),
        in_specs=[_tok_spec(w),
                  pl.BlockSpec((None, SUBLANES, w), lambda b, i: (b, jnp.maximum(i * per - 1, 0), 0)),
                  pl.BlockSpec((None, SUBLANES, w), lambda b, i: (b, jnp.minimum((i + 1) * per, last), 0)),
                  _const_spec(conv_w.shape), _const_spec((1, w)),
                  _const_spec(wq.shape), _const_spec(wk.shape)],
        out_specs=[_tok_spec(w), _tok_spec(w)],
        out_shape=[jax.ShapeDtypeStruct((bsz, t, w), BF16)] * 2,
        compiler_params=_cparams(("parallel", "parallel")),
        name="mlstm_prep",
    )(xm, xm, xm, conv_w, conv_b.reshape(1, w), wq, wk)


def _scan_specs(width, n_chunks, n_ctx_chunks):
    fwd = pl.BlockSpec((None, CHUNK, width), lambda b, i: (b, i, 0))
    bwd = pl.BlockSpec(
        (None, CHUNK, width),
        lambda b, i: (b, jnp.where(i < n_ctx_chunks, n_ctx_chunks - 1 - i, n_chunks + n_ctx_chunks - 1 - i), 0))
    return fwd, bwd


def _split3(x):
    hi = x.astype(BF16)
    r1 = x - hi.astype(F32)
    mid = r1.astype(BF16)
    lo = (r1 - mid.astype(F32)).astype(BF16)
    return hi, mid, lo


def _mlscan_kernel(qf_ref, kf_ref, vf_ref, gf_ref, qb_ref, kb_ref, vb_ref, gb_ref, bias_ref,
                   of_ref, ob_ref, st_ref, m_ref):
    i = pl.program_id(1)
    dk = ML_HEAD_DIM

    @pl.when(i == 0)
    def _init():
        st_ref[...] = jnp.zeros_like(st_ref)
        m_ref[...] = jnp.zeros_like(m_ref)

    tt = lax.broadcasted_iota(jnp.int32, (CHUNK, CHUNK), 0)
    ss = lax.broadcasted_iota(jnp.int32, (CHUNK, CHUNK), 1)
    lane = lax.broadcasted_iota(jnp.int32, (CHUNK, LANES), 1)
    row = lax.broadcasted_iota(jnp.int32, (CHUNK, LANES), 0)
    ones_col = (lane == 0).astype(BF16)

    def running_max(x, reverse):
        sh = 1
        while sh < CHUNK:
            if reverse:
                shifted = jnp.where(row < CHUNK - sh, pltpu.roll(x, CHUNK - sh, axis=0), -jnp.inf)
            else:
                shifted = jnp.where(row >= sh, pltpu.roll(x, sh, axis=0), -jnp.inf)
            x = jnp.maximum(x, shifted)
            sh *= 2
        return x

    for d, (q_ref, k_ref, v_ref, g_ref, o_ref) in enumerate(
            ((qf_ref, kf_ref, vf_ref, gf_ref, of_ref), (qb_ref, kb_ref, vb_ref, gb_ref, ob_ref))):
        causal = (ss <= tt) if d == 0 else (ss >= tt)
        tri = causal.astype(BF16)
        gates = g_ref[...] + bias_ref[...]
        hi, mid, lo = _split3(_log_sigmoid(gates[:, LANES:]))
        bc = _dot(tri, hi) + _dot(tri, mid) + _dot(tri, lo)
        e = gates[:, :LANES] - bc
        e_t = e.T
        m_prev = m_ref[d:d + 1, :]
        g = jnp.maximum(running_max(e, reverse=d == 1), m_prev)
        w_inter = jnp.exp(m_prev - g)
        exp_neg_mt = jnp.exp(-(bc + g))
        last = CHUNK - 1 if d == 0 else 0
        b_last = bc[last:last + 1, :]
        w_log = b_last + e
        m_new = jnp.maximum(b_last + m_prev, w_log.max(axis=0, keepdims=True))
        w_in = jnp.exp(w_log - m_new)
        decay = jnp.exp(b_last + m_prev - m_new)
        m_ref[d:d + 1, :] = m_new
        for h in range(ML_HEADS):
            c = d * ML_HEADS + h
            hs = slice(h * dk, (h + 1) * dk)
            q, k = q_ref[:, hs], k_ref[:, hs]
            v_aug = jnp.concatenate([v_ref[:, hs], ones_col], axis=-1)
            p = jnp.exp(jnp.where(causal, e_t[c:c + 1, :] - g[:, c:c + 1], -jnp.inf))
            s = _dot_nt(q, k) * p
            state = st_ref[c]
            res = _dot(s.astype(BF16), v_aug) + w_inter[:, c:c + 1] * _dot(q, state.astype(BF16))
            inv = 1.0 / jnp.maximum(jnp.abs(res[:, dk:dk + 1]), exp_neg_mt[:, c:c + 1])
            o_ref[:, hs] = res[:, :dk] * inv
            st_ref[c] = decay[:, c:c + 1] * state + _dot_tn(k, (v_aug * w_in[:, c:c + 1]).astype(BF16))


def _mlscan(q, k, v, gates, gate_bias, n_ctx):
    bsz, t, w = q.shape
    nc, ncc = t // CHUNK, n_ctx // CHUNK
    f, bk = _scan_specs(w, nc, ncc)
    gf, gb = _scan_specs(2 * LANES, nc, ncc)
    n_chain = 2 * ML_HEADS
    return pl.pallas_call(
        _mlscan_kernel,
        grid=(bsz, nc),
        in_specs=[f, f, f, gf, bk, bk, bk, gb, _const_spec((1, 2 * LANES))],
        out_specs=[f, bk],
        out_shape=[jax.ShapeDtypeStruct((bsz, t, w), F32)] * 2,
        scratch_shapes=[pltpu.VMEM((n_chain, ML_HEAD_DIM, 2 * ML_HEAD_DIM), F32),
                        pltpu.VMEM((SUBLANES, LANES), F32)],
        compiler_params=_cparams(("parallel", "arbitrary")),
        name="mlstm_scan",
    )(q, k, v, gates, q, k, v, gates, gate_bias)


def _retscan_kernel(qf_ref, kf_ref, vf_ref, qb_ref, kb_ref, vb_ref, logit_ref, of_ref, ob_ref, st_ref):
    i = pl.program_id(1)
    dk = RET_HEAD_DIM

    @pl.when(i == 0)
    def _init():
        st_ref[...] = jnp.zeros_like(st_ref)

    tt = lax.broadcasted_iota(jnp.int32, (CHUNK, CHUNK), 0)
    ss = lax.broadcasted_iota(jnp.int32, (CHUNK, CHUNK), 1)
    pos = lax.broadcasted_iota(jnp.int32, (CHUNK, 1), 0).astype(F32)
    log_g = _log_sigmoid(logit_ref[...])

    for d, (q_ref, k_ref, v_ref, o_ref) in enumerate(
            ((qf_ref, kf_ref, vf_ref, of_ref), (qb_ref, kb_ref, vb_ref, ob_ref))):
        dist = (tt - ss) if d == 0 else (ss - tt)
        distf = jnp.maximum(dist, 0).astype(F32)
        step = pos if d == 0 else (CHUNK - 1.0) - pos
        for h in range(RET_HEADS):
            c = d * RET_HEADS + h
            hs = slice(h * dk, (h + 1) * dk)
            lg = log_g[c:c + 1, 0:1]
            dmat = jnp.where(dist >= 0, jnp.exp(distf * lg), 0.0)
            q, k, v = q_ref[:, hs], k_ref[:, hs], v_ref[:, hs]
            state = st_ref[c]
            inner = _dot((_dot_nt(q, k) * dmat).astype(BF16), v)
            cross = jnp.exp((step + 1.0) * lg) * _dot(q, state.astype(BF16))
            o_ref[:, hs] = inner + cross
            k_dec = jnp.exp((CHUNK - 1.0 - step) * lg)
            st_ref[c] = jnp.exp(CHUNK * lg) * state + _dot_tn(k, (v.astype(F32) * k_dec).astype(BF16))


def _retscan(q, k, v, decay_logit, n_ctx):
    bsz, t, w = q.shape
    nc, ncc = t // CHUNK, n_ctx // CHUNK
    f, bk = _scan_specs(w, nc, ncc)
    logit = jnp.broadcast_to(decay_logit.astype(F32).reshape(2 * RET_HEADS, 1), (2 * RET_HEADS, LANES))
    return pl.pallas_call(
        _retscan_kernel,
        grid=(bsz, nc),
        in_specs=[f, f, f, bk, bk, bk, _const_spec((2 * RET_HEADS, LANES))],
        out_specs=[f, bk],
        out_shape=[jax.ShapeDtypeStruct((bsz, t, w), F32)] * 2,
        scratch_shapes=[pltpu.VMEM((2 * RET_HEADS, RET_HEAD_DIM, RET_HEAD_DIM), F32)],
        compiler_params=_cparams(("parallel", "arbitrary")),
        name="retention_scan",
    )(q, k, v, q, k, v, logit)


def _s5_weights(lam_re, lam_im, log_dt, b_re, b_im, c_re, c_im):
    hp = lax.Precision.HIGHEST
    n = S5_SUB
    dt = jnp.exp(log_dt)[..., None]
    zr, zi = lam_re * dt, lam_im * dt
    steps = jnp.arange(n + 1, dtype=F32)[:, None, None, None]
    pmag = jnp.exp(steps * zr)
    ak_re, ak_im = pmag * jnp.cos(steps * zi), pmag * jnp.sin(steps * zi)
    a_re, a_im = ak_re[1], ak_im[1]
    lam_sq = jnp.square(lam_re) + jnp.square(lam_im)
    e_re = ((a_re - 1.0) * lam_re + a_im * lam_im) / lam_sq
    e_im = (a_im * lam_re - (a_re - 1.0) * lam_im) / lam_sq
    bb_re = e_re[..., None] * b_re - e_im[..., None] * b_im
    bb_im = e_re[..., None] * b_im + e_im[..., None] * b_re
    ab_re = ak_re[..., None] * bb_re - ak_im[..., None] * bb_im
    ab_im = ak_re[..., None] * bb_im + ak_im[..., None] * bb_re
    kern = (jnp.einsum('dgqp,kdgpr->kdgqr', c_re, ab_re, precision=hp)
            - jnp.einsum('dgqp,kdgpr->kdgqr', c_im, ab_im, precision=hp))
    sig = jnp.arange(n)
    zeros = jnp.zeros_like(kern[:n])
    toe = jnp.stack([jnp.concatenate([zeros[:sp], kern[:n - sp]], axis=0) for sp in range(n)], axis=0)
    toe = toe.transpose(2, 3, 0, 5, 1, 4)
    vin_re = ab_re[n - 1 - sig].transpose(1, 2, 0, 4, 3)
    vin_im = ab_im[n - 1 - sig].transpose(1, 2, 0, 4, 3)
    ap_re, ap_im = ak_re[1:], ak_im[1:]
    w_re = c_re[None] * ap_re[:, :, :, None, :] - c_im[None] * ap_im[:, :, :, None, :]
    w_im = -(c_re[None] * ap_im[:, :, :, None, :] + c_im[None] * ap_re[:, :, :, None, :])
    w_re = w_re.transpose(1, 2, 4, 0, 3)
    w_im = w_im.transpose(1, 2, 4, 0, 3)

    def orient(a, axes):
        return jnp.stack([a[0], jnp.flip(a[1], axis=tuple(x - 1 for x in axes))], axis=0)

    toe = orient(toe, (2, 4))
    vin = orient(jnp.stack([vin_re, vin_im], axis=4), (2,))
    wout = orient(jnp.stack([w_re, w_im], axis=2), (4,))

    no = lam_re.shape[1] // S5_OCT
    eye = jnp.eye(S5_OCT, dtype=F32)
    octs = lambda a: a.reshape((2, no, S5_OCT) + a.shape[2:])
    toe = octs(toe).transpose(0, 1, 3, 2, 4, 5, 6)
    t_op = toe[:, :, :, :, :, :, None, :] * eye[None, None, None, :, None, None, :, None]
    vin = octs(vin).transpose(0, 1, 3, 2, 4, 5, 6)
    v_op = vin[:, :, :, :, :, :, None, :] * eye[None, None, None, :, None, None, :, None]
    wout = octs(wout).transpose(0, 1, 3, 2, 4, 5, 6)
    w_op = wout[:, :, :, :, :, :, None, :] * eye[None, None, None, :, None, None, :, None]
    nq, ns = n * LANES, S5_OCT * S5_STATE
    t_op = t_op.reshape(2, no, nq, nq).astype(BF16)
    v_op = v_op.reshape(2, no, nq, 2 * ns).astype(BF16)
    w_op = w_op.reshape(2, no, 2 * ns, nq).astype(BF16)
    a_re = ak_re[n].reshape(2, no, 1, ns)
    a_im = ak_im[n].reshape(2, no, 1, ns)
    return t_op, v_op, w_op, a_re, a_im


def _s5_kernel(uf_ref, ub_ref, t_ref, v_ref, w_ref, are_ref, aim_ref, yf_ref, yb_ref, in_ref, xp_ref, st_ref,
               *, bsz, jt):
    ns = S5_OCT * S5_STATE
    rows = bsz * jt

    @pl.when(pl.program_id(1) == 0)
    def _init():
        st_ref[...] = jnp.zeros_like(st_ref)

    ri = lax.broadcasted_iota(jnp.int32, (rows, rows), 0)
    ci = lax.broadcasted_iota(jnp.int32, (rows, rows), 1)
    perm = lambda inner: (ci == (ri % inner) * (rows // inner) + ri // inner).astype(BF16)
    p_sj = perm(jt)
    p_jb = perm(bsz)
    p_bj = perm(jt)

    for d, (u_ref, y_ref) in enumerate(((uf_ref, yf_ref), (ub_ref, yb_ref))):
        u_rows = []
        for b in range(bsz):
            z = _dot(p_sj, u_ref[b].astype(BF16)).astype(BF16)
            u_rows.append(jnp.concatenate([z[s * jt:(s + 1) * jt] for s in range(S5_SUB)], axis=1))
        u = jnp.concatenate(u_rows, axis=0)
        y_intra = _dot(u, t_ref[d])
        u_jb = _dot(p_jb, u).astype(BF16)
        in_ref[d] = _dot(u_jb, v_ref[d])
        a_re, a_im = are_ref[d], aim_ref[d]
        xr, xi = st_ref[d, 0], st_ref[d, 1]
        for jj in range(jt):
            j = jj if d == 0 else jt - 1 - jj
            rows_j = slice(j * bsz, (j + 1) * bsz)
            xp_ref[d, rows_j, :] = jnp.concatenate([xr, xi], axis=1)
            inc = in_ref[d, rows_j, :]
            xr, xi = (a_re * xr - a_im * xi + inc[:, :ns], a_re * xi + a_im * xr + inc[:, ns:])
        st_ref[d, 0] = xr
        st_ref[d, 1] = xi
        xp = _dot(p_bj, xp_ref[d].astype(BF16)).astype(BF16)
        y = y_intra + _dot(xp, w_ref[d])
        for b in range(bsz):
            for s in range(S5_SUB):
                y_ref[b, pl.ds(s, jt, stride=S5_SUB), :] = y[b * jt:(b + 1) * jt, s * LANES:(s + 1) * LANES]


def _s5(u, t_op, v_op, w_op, a_re, a_im, n_ctx):
    bsz, t, width = u.shape
    no = width // LANES
    nt, nct = t // TM, n_ctx // TM
    jt = TM // S5_SUB
    nq, ns = S5_SUB * LANES, S5_OCT * S5_STATE
    fwd = pl.BlockSpec((bsz, TM, LANES), lambda o, i: (0, i, o))
    bwd = pl.BlockSpec((bsz, TM, LANES), lambda o, i: (0, jnp.where(i < nct, nct - 1 - i, nt + nct - 1 - i), o))
    op = lambda r, c: pl.BlockSpec((2, None, r, c), lambda o, i: (0, o, 0, 0))
    return pl.pallas_call(
        functools.partial(_s5_kernel, bsz=bsz, jt=jt),
        grid=(no, nt),
        in_specs=[fwd, bwd, op(nq, nq), op(nq, 2 * ns), op(2 * ns, nq), op(1, ns), op(1, ns)],
        out_specs=[fwd, bwd],
        out_shape=[jax.ShapeDtypeStruct((bsz, t, width), F32)] * 2,
        scratch_shapes=[pltpu.VMEM((2, bsz * jt, 2 * ns), F32), pltpu.VMEM((2, bsz * jt, 2 * ns), F32),
                        pltpu.VMEM((2, 2, bsz, ns), F32)],
        compiler_params=_cparams(("parallel", "arbitrary")),
        name="s5",
    )(u, u, t_op, v_op, w_op, a_re, a_im)


def _out_even_kernel(x_ref, mod_ref, na_ref, hf_ref, hb_ref, zo_ref, gn_ref, w_ref, lng_ref, lnb_ref,
                     o_ref, *, alpha):
    ml = _sigmoid(zo_ref[...]) * _head_norm(hf_ref[...] + hb_ref[...], gn_ref[...], ML_HEADS, ML_HEAD_DIM)
    y = _dot(na_ref[...].astype(BF16), w_ref[0:HALF, :]) + _dot(ml.astype(BF16), w_ref[HALF:, :])
    r = alpha * x_ref[...] + _mod_row(mod_ref, 1, 2) * y
    o_ref[...] = _layer_norm(r, lng_ref[...], lnb_ref[...])


def _out_odd_kernel(x_ref, mod_ref, ysf_ref, ysb_ref, u_ref, dsk_ref, gw_ref, gb_ref, rf_ref, rb_ref, gr_ref,
                    gn_ref, w_ref, lng_ref, lnb_ref, o_ref, *, alpha):
    s = _gelu_tanh(ysf_ref[...] + ysb_ref[...] + dsk_ref[...] * u_ref[...])
    s = s * _sigmoid(_dot(s.astype(BF16), gw_ref[...]) + gb_ref[...])
    ret = _silu(gr_ref[...]) * _head_norm(rf_ref[...] + rb_ref[...], gn_ref[...], RET_HEADS, RET_HEAD_DIM)
    y = _dot(s.astype(BF16), w_ref[0:HALF, :]) + _dot(ret.astype(BF16), w_ref[HALF:, :])
    r = alpha * x_ref[...] + _mod_row(mod_ref, 1, 2) * y
    o_ref[...] = _layer_norm(r, lng_ref[...], lnb_ref[...])


def _out_call(kernel, name, xa, mod, extra, alpha, n_ctx_tiles):
    bsz, t, d = xa.shape
    in_specs = [_tok_spec(d), _mod_block(d, bsz, n_ctx_tiles)]
    args = [xa, mod]
    for kind, a in extra:
        in_specs.append(_tok_spec(a.shape[-1]) if kind == 't' else _const_spec(a.shape))
        args.append(a)
    return pl.pallas_call(
        functools.partial(kernel, alpha=alpha),
        grid=(bsz, t // TM),
        in_specs=in_specs,
        out_specs=_tok_spec(d),
        out_shape=jax.ShapeDtypeStruct((bsz, t, d), F32),
        compiler_params=_cparams(("parallel", "parallel")),
        name=name,
    )(*args)


def _rope_tables(n_ctx, seq):
    nf = RET_HEAD_DIM // 4
    freqs = ROPE_BASE ** (-jnp.arange(nf, dtype=F32) / nf)
    tok = jnp.arange(seq)
    ang_r = (tok // GRID_W).astype(F32)[:, None] * freqs
    ang_c = (tok % GRID_W).astype(F32)[:, None] * freqs
    cos = jnp.concatenate([jnp.cos(ang_r)] * 2 + [jnp.cos(ang_c)] * 2, axis=-1)
    sin = jnp.concatenate([-jnp.sin(ang_r), jnp.sin(ang_r), -jnp.sin(ang_c), jnp.sin(ang_c)], axis=-1)
    cos = jnp.concatenate([jnp.ones((n_ctx, LANES), F32), cos], axis=0)
    sin = jnp.concatenate([jnp.zeros((n_ctx, LANES), F32), sin], axis=0)
    return cos, sin


def _even_layer_mixer(xa, mod, w_in, w_out, rpb, conv_w, conv_b, wq, wk, i_bias, f_bias, gn_w,
                      lng, lnb, alpha, n_ctx):
    d = xa.shape[-1]
    n_ctx_tiles = n_ctx // TM
    ng = 2 * ML_HEADS
    main = 6 * HALF
    w_gates = jnp.zeros((d, 2 * LANES), F32)
    w_gates = w_gates.at[:, 0:ng].set(w_in[:, main:main + ng]).at[:, LANES:LANES + ng].set(w_in[:, main + ng:])
    w_all = jnp.concatenate([w_in[:, :main], w_gates], axis=1).astype(BF16)
    gate_bias = jnp.zeros((1, 2 * LANES), F32)
    gate_bias = gate_bias.at[0, 0:ng].set(i_bias.reshape(ng)).at[0, LANES:LANES + ng].set(f_bias.reshape(ng))
    outs = [(0, HALF, NA_HEAD_DIM ** -0.5, False), (HALF, HALF, 1.0, False), (2 * HALF, HALF, 1.0, False),
            (3 * HALF, HALF, 1.0, False), (4 * HALF, HALF, 1.0, False), (5 * HALF, HALF, 1.0, False),
            (main, 2 * LANES, 1.0, False)]
    dts = [BF16, BF16, BF16, F32, BF16, F32, F32]
    q_na, k_na, v_na, xm, zv, zo, gates = _inproj(xa, mod, w_all, outs, dts, n_ctx_tiles)
    na = _na(q_na, k_na, v_na, rpb, n_ctx)
    q_ml, k_ml = _mlprep(xm, conv_w, conv_b, wq.astype(BF16), wk.astype(BF16), n_ctx)
    hf, hb = _mlscan(q_ml, k_ml, zv, gates, gate_bias, n_ctx)
    return _out_call(
        _out_even_kernel, "out_even", xa, mod,
        [('t', na), ('t', hf), ('t', hb), ('t', zo), ('c', gn_w.reshape(1, HALF)), ('c', w_out.astype(BF16)),
         ('c', lng.reshape(1, d)), ('c', lnb.reshape(1, d))], alpha, n_ctx_tiles)


def _odd_layer_mixer(xa, mod, w_in, w_out, s5_params, d_skip, glu_w, glu_b, decay_logit, gn_w,
                     lng, lnb, alpha, n_ctx, rope_tabs):
    d = xa.shape[-1]
    n_ctx_tiles = n_ctx // TM
    outs = [(0, HALF, 1.0, False), (HALF, HALF, 1.0, True), (2 * HALF, HALF, RET_HEAD_DIM ** -0.5, True),
            (3 * HALF, HALF, 1.0, False), (4 * HALF, HALF, 1.0, False)]
    dts = [F32, BF16, BF16, BF16, F32]
    u, q_r, k_r, v_r, g_r = _inproj(xa, mod, w_in.astype(BF16), outs, dts, n_ctx_tiles, rope_tabs)
    ysf, ysb = _s5(u, *_s5_weights(*s5_params), n_ctx)
    rf, rb = _retscan(q_r, k_r, v_r, decay_logit, n_ctx)
    return _out_call(
        _out_odd_kernel, "out_odd", xa, mod,
        [('t', ysf), ('t', ysb), ('t', u), ('c', d_skip.reshape(1, HALF)), ('c', glu_w.astype(BF16)),
         ('c', glu_b.reshape(1, HALF)), ('t', rf), ('t', rb), ('t', g_r), ('c', gn_w.reshape(1, HALF)),
         ('c', w_out.astype(BF16)), ('c', lng.reshape(1, d)), ('c', lnb.reshape(1, d))],
        alpha, n_ctx_tiles)


def kernel(x, c, ctx, c_ctx, ada_w, ada_b, ffn_w_gate, ffn_w_up, ffn_w_down, ln_g, ln_b, ev_w_in, ev_w_out, na_rpb, ml_conv_w, ml_conv_b, ml_wq, ml_wk, ml_i_bias, ml_f_bias, ml_gn_w, od_w_in, od_w_out, s5_lam_re, s5_lam_im, s5_log_dt, s5_b_re, s5_b_im, s5_c_re, s5_c_im, s5_d, s5_glu_w, s5_glu_b, ret_decay_logit, ret_gn_w):
    bsz, seq, d = x.shape
    n_ctx = ctx.shape[1]
    depth = ada_w.shape[0]
    assert n_ctx % TM == 0 and seq % TM == 0 and seq % GRID_W == 0 and bsz == SUBLANES
    n_ctx_tiles = n_ctx // TM
    alpha = (2.0 * depth) ** 0.25

    rows = bsz + SUBLANES
    c_rows = jnp.zeros((rows, d), F32).at[:bsz].set(c).at[bsz].set(c_ctx)
    mod_all = _mod_table(c_rows, ada_w, ada_b)

    xa = x
    rope_tabs = _rope_tables(n_ctx, seq)
    for l in range(depth):
        e = l // 2
        mod = mod_all[l]
        ffn = lambda j, k, h, **kw: _ffn(h, mod, j, ffn_w_gate[l, k].astype(BF16), ffn_w_up[l, k].astype(BF16),
                                         ffn_w_down[l, k].astype(BF16), ln_g[l, j], ln_b[l, j], alpha,
                                         n_ctx_tiles, **kw)
        xa = ffn(0, 0, xa, ctx=ctx) if l == 0 else ffn(0, 0, xa)
        if l % 2 == 0:
            xa = _even_layer_mixer(xa, mod, ev_w_in[e], ev_w_out[e], na_rpb[e], ml_conv_w[e], ml_conv_b[e],
                                   ml_wq[e], ml_wk[e], ml_i_bias[e], ml_f_bias[e], ml_gn_w[e],
                                   ln_g[l, 1], ln_b[l, 1], alpha, n_ctx)
        else:
            s5_params = (s5_lam_re[e], s5_lam_im[e], s5_log_dt[e], s5_b_re[e], s5_b_im[e], s5_c_re[e], s5_c_im[e])
            xa = _odd_layer_mixer(xa, mod, od_w_in[e], od_w_out[e], s5_params, s5_d[e], s5_glu_w[e], s5_glu_b[e],
                                  ret_decay_logit[e], ret_gn_w[e], ln_g[l, 1], ln_b[l, 1], alpha, n_ctx, rope_tabs)
        xa = ffn(2, 1, xa, latent_only=(l == depth - 1))
    return xa
```

```python
import functools
import math

import jax
import jax.numpy as jnp
from jax import lax
from jax.experimental import pallas as pl
from jax.experimental.pallas import tpu as pltpu

F32 = jnp.float32
BF16 = jnp.bfloat16

GRID_W = 64
LN_EPS = 1e-5
N_MOD = 9
NA_HEADS, NA_HEAD_DIM, NA_WIN_H, NA_WIN_W = 8, 64, 8, 16
ML_HEADS, ML_HEAD_DIM, ML_CONV = 4, 128, 5
S5_GROUP, S5_GROUPS, S5_STATE = 16, 32, 64
RET_HEADS, RET_HEAD_DIM = 4, 128
ROPE_BASE = 10000.0
HALF = 512

LANES = 128
SUBLANES = 8
VMEM_LIMIT = 56 * 1024 * 1024

TM = 256
CHUNK = 256
S5_SUB = 8
S5_OCT = LANES // S5_GROUP
NEG = -1e30


def _cparams(sem):
    return pltpu.CompilerParams(dimension_semantics=sem, vmem_limit_bytes=VMEM_LIMIT)


def _const_spec(shape):
    nd = len(shape)
    return pl.BlockSpec(shape, lambda *_: (0,) * nd, pipeline_mode=pl.Buffered(1))


def _dot(a, b):
    return jnp.dot(a, b, preferred_element_type=F32)


def _dot_nt(a, b):
    return lax.dot_general(a, b, (((1,), (1,)), ((), ())), preferred_element_type=F32)


def _dot_tn(a, b):
    return lax.dot_general(a, b, (((0,), (0,)), ((), ())), preferred_element_type=F32)


def _sigmoid(x):
    return 1.0 / (1.0 + jnp.exp(-x))


def _silu(x):
    return x * _sigmoid(x)


def _log_sigmoid(x):
    return jnp.minimum(x, 0.0) - jnp.log1p(jnp.exp(-jnp.abs(x)))


def _gelu_tanh(x):
    return 0.5 * x * (1.0 + jnp.tanh(math.sqrt(2.0 / math.pi) * (x + 0.044715 * (x * x * x))))


def _layer_norm(r, g, b):
    mu = jnp.mean(r, axis=-1, keepdims=True)
    c = r - mu
    var = jnp.mean(c * c, axis=-1, keepdims=True)
    return c * lax.rsqrt(var + LN_EPS) * g + b


def _head_norm(h, w, n_heads, head_dim):
    parts = []
    for k in range(n_heads):
        hk = h[:, k * head_dim:(k + 1) * head_dim]
        mu = jnp.mean(hk, axis=-1, keepdims=True)
        c = hk - mu
        var = jnp.mean(c * c, axis=-1, keepdims=True)
        parts.append(c * lax.rsqrt(var + LN_EPS))
    return jnp.concatenate(parts, axis=-1) * w


def _mod_row(mod_ref, j, k):
    i = 3 * j + k
    return mod_ref[i:i + 1, :]


def _mod_kernel(c_ref, w_ref, b_ref, o_ref):
    c = c_ref[...]
    s = _silu(c).astype(BF16)
    o_ref[...] = _dot(s, w_ref[...].astype(BF16)) + b_ref[...]


def _mod_table(c_rows, ada_w, ada_b):
    depth, d, n = ada_w.shape
    rows = c_rows.shape[0]
    tn = 1024
    out = pl.pallas_call(
        _mod_kernel,
        grid=(depth, n // tn),
        in_specs=[pl.BlockSpec((rows, d), lambda l, j: (0, 0)),
                  pl.BlockSpec((None, d, tn), lambda l, j: (l, 0, j)),
                  pl.BlockSpec((None, 1, tn), lambda l, j: (l, 0, j))],
        out_specs=pl.BlockSpec((None, rows, tn), lambda l, j: (l, 0, j)),
        out_shape=jax.ShapeDtypeStruct((depth, rows, n), F32),
        compiler_params=_cparams(("parallel", "parallel")),
        name="mod_table",
    )(c_rows, ada_w, ada_b.reshape(depth, 1, n))
    return out.reshape(depth, rows, N_MOD, d)


def _tok_spec(width, tm=TM):
    return pl.BlockSpec((None, tm, width), lambda b, t: (b, t, 0))


def _mod_block(d, batch, n_ctx_tiles):
    return pl.BlockSpec((None, N_MOD, d), lambda b, t: (jnp.where(t < n_ctx_tiles, batch, b), 0, 0))


def _ffn_kernel(*refs, j, alpha, n_ctx_tiles, split_in):
    if split_in:
        ctx_ref, x_ref = refs[:2]
        refs = refs[1:]
        x = jnp.where(pl.program_id(1) < n_ctx_tiles, ctx_ref[...], x_ref[...])
    else:
        x = refs[0][...]
    _, mod_ref, wg_ref, wu_ref, wd_ref, lng_ref, lnb_ref, o_ref = refs
    h = (x * (1.0 + _mod_row(mod_ref, j, 1)) + _mod_row(mod_ref, j, 0)).astype(BF16)
    g = _dot(h, wg_ref[...])
    u = _dot(h, wu_ref[...])
    a = (_silu(g) * u).astype(BF16)
    y = _dot(a, wd_ref[...])
    r = alpha * x + (0.5 * _mod_row(mod_ref, j, 2)) * y
    o_ref[...] = _layer_norm(r, lng_ref[...], lnb_ref[...])


def _ffn(xa, mod, j, wg, wu, wd, lng, lnb, alpha, n_ctx_tiles, ctx=None, latent_only=False):
    bsz, t, d = xa.shape
    dff = wg.shape[1]
    nct = n_ctx_tiles
    weights = [_const_spec((d, dff)), _const_spec((d, dff)), _const_spec((dff, d)),
               _const_spec((1, d)), _const_spec((1, d))]
    wargs = (wg, wu, wd, lng.reshape(1, d), lnb.reshape(1, d))
    if ctx is not None:
        t_out = t + ctx.shape[1]
        grid = (bsz, t_out // TM)
        in_specs = [pl.BlockSpec((None, TM, d), lambda b, i: (b, jnp.minimum(i, nct - 1), 0)),
                    pl.BlockSpec((None, TM, d), lambda b, i: (b, jnp.maximum(i - nct, 0), 0)),
                    _mod_block(d, bsz, nct)]
        args = (ctx, xa, mod)
        out_spec = _tok_spec(d)
    elif latent_only:
        t_out = t - nct * TM
        grid = (bsz, t_out // TM)
        in_specs = [pl.BlockSpec((None, TM, d), lambda b, i: (b, i + nct, 0)),
                    pl.BlockSpec((None, N_MOD, d), lambda b, i: (b, 0, 0))]
        args = (xa, mod)
        out_spec = _tok_spec(d)
    else:
        t_out = t
        grid = (bsz, t // TM)
        in_specs = [_tok_spec(d), _mod_block(d, bsz, nct)]
        args = (xa, mod)
        out_spec = _tok_spec(d)
    return pl.pallas_call(
        functools.partial(_ffn_kernel, j=j, alpha=alpha, n_ctx_tiles=nct, split_in=ctx is not None),
        grid=grid,
        in_specs=in_specs + weights,
        out_specs=out_spec,
        out_shape=jax.ShapeDtypeStruct((bsz, t_out, d), F32),
        compiler_params=_cparams(("parallel", "parallel")),
        name="ffn",
    )(*args, *wargs)


def _rope(z, cos, sin):
    lane = lax.broadcasted_iota(jnp.int32, (1, LANES), 1)
    first = (lane % 64) < 32
    parts = []
    for k in range(z.shape[1] // LANES):
        zk = z[:, k * LANES:(k + 1) * LANES]
        partner = jnp.where(first, pltpu.roll(zk, LANES - 32, axis=1), pltpu.roll(zk, 32, axis=1))
        parts.append(zk * cos + partner * sin)
    return jnp.concatenate(parts, axis=-1)


def _inproj_kernel(*refs, outs, use_rope):
    x_ref, mod_ref, w_ref = refs[:3]
    k = 3
    if use_rope:
        cos_ref, sin_ref = refs[3:5]
        k = 5
    out_refs = refs[k:]
    h = (x_ref[...] * (1.0 + _mod_row(mod_ref, 1, 1)) + _mod_row(mod_ref, 1, 0)).astype(BF16)
    for (off, width, scale, rope), o_ref in zip(outs, out_refs):
        z = _dot(h, w_ref[:, off:off + width])
        if rope:
            z = _rope(z, cos_ref[...], sin_ref[...])
        if scale != 1.0:
            z = z * scale
        o_ref[...] = z.astype(o_ref.dtype)


def _inproj(xa, mod, w, outs, dtypes, n_ctx_tiles, rope_tabs=None):
    bsz, t, d = xa.shape
    in_specs = [_tok_spec(d), _mod_block(d, bsz, n_ctx_tiles), _const_spec(w.shape)]
    args = [xa, mod, w]
    if rope_tabs is not None:
        in_specs += [pl.BlockSpec((TM, LANES), lambda b, t: (t, 0))] * 2
        args += list(rope_tabs)
    return pl.pallas_call(
        functools.partial(_inproj_kernel, outs=tuple(outs), use_rope=rope_tabs is not None),
        grid=(bsz, t // TM),
        in_specs=in_specs,
        out_specs=[_tok_spec(o[1]) for o in outs],
        out_shape=[jax.ShapeDtypeStruct((bsz, t, o[1]), dt) for o, dt in zip(outs, dtypes)],
        compiler_params=_cparams(("parallel", "parallel")),
        name="inproj",
    )(*args)


NA_ROWS_PER_TILE = TM // GRID_W
NA_SLAB_ROWS = NA_WIN_H + NA_ROWS_PER_TILE


def _softmax_pv(s_list, v_list):
    m = s_list[0].max(axis=-1, keepdims=True)
    for s in s_list[1:]:
        m = jnp.maximum(m, s.max(axis=-1, keepdims=True))
    acc, den = None, None
    for s, v in zip(s_list, v_list):
        p = jnp.exp(s - m)
        l = p.sum(axis=-1, keepdims=True)
        o = _dot(p.astype(BF16), v)
        acc = o if acc is None else acc + o
        den = l if den is None else den + l
    return acc / den


def _na_kernel(q_ref, k_ref, v_ref, bias_ref, o_ref, *, n_ctx, rows, n_ctx_tiles):
    t = pl.program_id(1)
    n_loc = NA_SLAB_ROWS * GRID_W
    lane = lax.broadcasted_iota(jnp.int32, (1, LANES), 1)
    low = lane < NA_HEAD_DIM
    zero = jnp.zeros((), BF16)

    def pair(p, keys):
        ls = slice(p * LANES, (p + 1) * LANES)
        q2 = q_ref[:, ls]
        outs = []
        for e in range(2):
            qm = jnp.where(low if e == 0 else jnp.logical_not(low), q2, zero)
            s_list, v_list = [], []
            for kk, vv, bias in keys(ls, 2 * p + e):
                s = _dot_nt(qm, kk)
                s_list.append(s if bias is None else s + bias)
                v_list.append(vv)
            outs.append(_softmax_pv(s_list, v_list))
        o_ref[:, ls] = jnp.where(low, outs[0], outs[1]).astype(o_ref.dtype)

    @pl.when(t < n_ctx_tiles)
    def _ctx():
        for p in range(NA_HEADS // 2):
            pair(p, lambda ls, h: [(k_ref[0:n_ctx, ls], v_ref[0:n_ctx, ls], None)])

    @pl.when(t >= n_ctx_tiles)
    def _latent():
        r0 = (t - n_ctx_tiles) * NA_ROWS_PER_TILE
        rs0 = jnp.clip(r0 - NA_WIN_H // 2, 0, rows - NA_SLAB_ROWS)
        var = (r0 - rs0) // NA_ROWS_PER_TILE
        start = pl.multiple_of(n_ctx + rs0 * GRID_W, GRID_W)
        for p in range(NA_HEADS // 2):
            pair(p, lambda ls, h: [(k_ref[pl.ds(start, n_loc), ls], v_ref[pl.ds(start, n_loc), ls], bias_ref[var, h]),
                                   (k_ref[0:n_ctx, ls], v_ref[0:n_ctx, ls], None)])


def _na_bias_table(rpb, rows):
    rpt, slab = NA_ROWS_PER_TILE, NA_SLAB_ROWS
    cols = jnp.arange(GRID_W)
    cs = jnp.clip(cols - NA_WIN_W // 2, 0, GRID_W - NA_WIN_W)
    kc = jnp.arange(GRID_W)
    in_win = (kc[None, :] >= cs[:, None]) & (kc[None, :] < cs[:, None] + NA_WIN_W)
    col_off = kc[None, :] - cols[:, None] + NA_WIN_W - 1
    diff = (jnp.arange(3) * rpt)[:, None, None]
    i = jnp.arange(rpt)[None, :, None]
    a = jnp.arange(slab)[None, None, :]
    w0 = jnp.clip(diff + i - NA_WIN_H // 2, 0, slab - NA_WIN_H)
    row_ok = (a >= w0) & (a < w0 + NA_WIN_H)
    row_off = a - diff - i + NA_WIN_H - 1
    row_sel = ((row_off[..., None] == jnp.arange(2 * NA_WIN_H - 1)) & row_ok[..., None]).astype(F32)
    col_sel = (col_off[:, :, None] == jnp.arange(2 * NA_WIN_W - 1)).astype(F32)
    tab = jnp.einsum('hrc,viar,qkc->vhiqak', rpb.astype(F32), row_sel, col_sel, precision=lax.Precision.HIGHEST)
    ok = row_ok[:, None, :, None, :, None] & in_win[None, None, None, :, None, :]
    tab = jnp.where(ok, tab, NEG)
    return tab.reshape(3, NA_HEADS, rpt * GRID_W, slab * GRID_W)


def _na(q, k, v, rpb, n_ctx):
    bsz, t, w = q.shape
    rows = (t - n_ctx) // GRID_W
    assert rows >= NA_SLAB_ROWS and rows % NA_ROWS_PER_TILE == 0
    bias = _na_bias_table(rpb, rows)
    return pl.pallas_call(
        functools.partial(_na_kernel, n_ctx=n_ctx, rows=rows, n_ctx_tiles=n_ctx // TM),
        grid=(bsz, t // TM),
        in_specs=[_tok_spec(w),
                  pl.BlockSpec((None, t, w), lambda b, i: (b, 0, 0)),
                  pl.BlockSpec((None, t, w), lambda b, i: (b, 0, 0)),
                  _const_spec(bias.shape)],
        out_specs=_tok_spec(w),
        out_shape=jax.ShapeDtypeStruct((bsz, t, w), BF16),
        compiler_params=_cparams(("parallel", "arbitrary")),
        name="natten",
    )(q, k, v, bias)


def _mlprep_kernel(xm_ref, prev_ref, next_ref, cw_ref, cb_ref, wq_ref, wk_ref, q_ref, k_ref,
                   *, n_ctx_tiles, n_tiles):
    t = pl.program_id(1)
    has_prev = jnp.logical_and(t != 0, t != n_ctx_tiles)
    has_next = jnp.logical_and(t != n_ctx_tiles - 1, t != n_tiles - 1)
    prev = jnp.where(has_prev, prev_ref[...], 0.0)
    nxt = jnp.where(has_next, next_ref[...], 0.0)
    ext = jnp.concatenate([prev, xm_ref[...], nxt], axis=0)
    acc = cb_ref[...]
    for j in range(ML_CONV):
        off = SUBLANES + j - ML_CONV // 2
        acc = acc + cw_ref[j:j + 1, :] * ext[off:off + TM, :]
    xc = _silu(acc).astype(BF16)
    scale = ML_HEAD_DIM ** -0.5
    for h in range(ML_HEADS):
        hs = slice(h * ML_HEAD_DIM, (h + 1) * ML_HEAD_DIM)
        q_ref[:, hs] = _dot(xc[:, hs], wq_ref[h]).astype(BF16)
        k_ref[:, hs] = (_dot(xc[:, hs], wk_ref[h]) * scale).astype(BF16)


def _mlprep(xm, conv_w, conv_b, wq, wk, n_ctx):
    bsz, t, w = xm.shape
    nt = t // TM
    per = TM // SUBLANES
    last = t // SUBLANES - 1
    return pl.pallas_call(
        functools.partial(_mlprep_kernel, n_ctx_tiles=n_ctx // TM, n_tiles=nt),
        grid=(bsz, nt),
        in_specs=[_tok_spec(w),
                  pl.BlockSpec((None, SUBLANES, w), lambda b, i: (b, jnp.maximum(i * per - 1, 0), 0)),
                  pl.BlockSpec((None, SUBLANES, w), lambda b, i: (b, jnp.minimum((i + 1) * per, last), 0)),
                  _const_spec(conv_w.shape), _const_spec((1, w)),
                  _const_spec(wq.shape), _const_spec(wk.shape)],
        out_specs=[_tok_spec(w), _tok_spec(w)],
        out_shape=[jax.ShapeDtypeStruct((bsz, t, w), BF16)] * 2,
        compiler_params=_cparams(("parallel", "parallel")),
        name="mlstm_prep",
    )(xm, xm, xm, conv_w, conv_b.reshape(1, w), wq, wk)


def _scan_specs(width, n_chunks, n_ctx_chunks):
    fwd = pl.BlockSpec((None, CHUNK, width), lambda b, i: (b, i, 0))
    bwd = pl.BlockSpec(
        (None, CHUNK, width),
        lambda b, i: (b, jnp.where(i < n_ctx_chunks, n_ctx_chunks - 1 - i, n_chunks + n_ctx_chunks - 1 - i), 0))
    return fwd, bwd


def _split3(x):
    hi = x.astype(BF16)
    r1 = x - hi.astype(F32)
    mid = r1.astype(BF16)
    lo = (r1 - mid.astype(F32)).astype(BF16)
    return hi, mid, lo


def _mlscan_kernel(qf_ref, kf_ref, vf_ref, gf_ref, qb_ref, kb_ref, vb_ref, gb_ref, bias_ref,
                   of_ref, ob_ref, st_ref, m_ref):
    i = pl.program_id(1)
    dk = ML_HEAD_DIM

    @pl.when(i == 0)
    def _init():
        st_ref[...] = jnp.zeros_like(st_ref)
        m_ref[...] = jnp.zeros_like(m_ref)

    tt = lax.broadcasted_iota(jnp.int32, (CHUNK, CHUNK), 0)
    ss = lax.broadcasted_iota(jnp.int32, (CHUNK, CHUNK), 1)
    lane = lax.broadcasted_iota(jnp.int32, (CHUNK, LANES), 1)
    row = lax.broadcasted_iota(jnp.int32, (CHUNK, LANES), 0)
    ones_col = (lane == 0).astype(BF16)

    def running_max(x, reverse):
        sh = 1
        while sh < CHUNK:
            if reverse:
                shifted = jnp.where(row < CHUNK - sh, pltpu.roll(x, CHUNK - sh, axis=0), -jnp.inf)
            else:
                shifted = jnp.where(row >= sh, pltpu.roll(x, sh, axis=0), -jnp.inf)
            x = jnp.maximum(x, shifted)
            sh *= 2
        return x

    for d, (q_ref, k_ref, v_ref, g_ref, o_ref) in enumerate(
            ((qf_ref, kf_ref, vf_ref, gf_ref, of_ref), (qb_ref, kb_ref, vb_ref, gb_ref, ob_ref))):
        causal = (ss <= tt) if d == 0 else (ss >= tt)
        tri = causal.astype(BF16)
        gates = g_ref[...] + bias_ref[...]
        hi, mid, lo = _split3(_log_sigmoid(gates[:, LANES:]))
        bc = _dot(tri, hi) + _dot(tri, mid) + _dot(tri, lo)
        e = gates[:, :LANES] - bc
        e_t = e.T
        m_prev = m_ref[d:d + 1, :]
        g = jnp.maximum(running_max(e, reverse=d == 1), m_prev)
        w_inter = jnp.exp(m_prev - g)
        exp_neg_mt = jnp.exp(-(bc + g))
        last = CHUNK - 1 if d == 0 else 0
        b_last = bc[last:last + 1, :]
        w_log = b_last + e
        m_new = jnp.maximum(b_last + m_prev, w_log.max(axis=0, keepdims=True))
        w_in = jnp.exp(w_log - m_new)
        decay = jnp.exp(b_last + m_prev - m_new)
        m_ref[d:d + 1, :] = m_new
        for h in range(ML_HEADS):
            c = d * ML_HEADS + h
            hs = slice(h * dk, (h + 1) * dk)
            q, k = q_ref[:, hs], k_ref[:, hs]
            v_aug = jnp.concatenate([v_ref[:, hs], ones_col], axis=-1)
            p = jnp.exp(jnp.where(causal, e_t[c:c + 1, :] - g[:, c:c + 1], -jnp.inf))
            s = _dot_nt(q, k) * p
            state = st_ref[c]
            res = _dot(s.astype(BF16), v_aug) + w_inter[:, c:c + 1] * _dot(q, state.astype(BF16))
            inv = 1.0 / jnp.maximum(jnp.abs(res[:, dk:dk + 1]), exp_neg_mt[:, c:c + 1])
            o_ref[:, hs] = res[:, :dk] * inv
            st_ref[c] = decay[:, c:c + 1] * state + _dot_tn(k, (v_aug * w_in[:, c:c + 1]).astype(BF16))


def _mlscan(q, k, v, gates, gate_bias, n_ctx):
    bsz, t, w = q.shape
    nc, ncc = t // CHUNK, n_ctx // CHUNK
    f, bk = _scan_specs(w, nc, ncc)
    gf, gb = _scan_specs(2 * LANES, nc, ncc)
    n_chain = 2 * ML_HEADS
    return pl.pallas_call(
        _mlscan_kernel,
        grid=(bsz, nc),
        in_specs=[f, f, f, gf, bk, bk, bk, gb, _const_spec((1, 2 * LANES))],
        out_specs=[f, bk],
        out_shape=[jax.ShapeDtypeStruct((bsz, t, w), F32)] * 2,
        scratch_shapes=[pltpu.VMEM((n_chain, ML_HEAD_DIM, 2 * ML_HEAD_DIM), F32),
                        pltpu.VMEM((SUBLANES, LANES), F32)],
        compiler_params=_cparams(("parallel", "arbitrary")),
        name="mlstm_scan",
    )(q, k, v, gates, q, k, v, gates, gate_bias)


def _retscan_kernel(qf_ref, kf_ref, vf_ref, qb_ref, kb_ref, vb_ref, logit_ref, of_ref, ob_ref, st_ref):
    i = pl.program_id(1)
    dk = RET_HEAD_DIM

    @pl.when(i == 0)
    def _init():
        st_ref[...] = jnp.zeros_like(st_ref)

    tt = lax.broadcasted_iota(jnp.int32, (CHUNK, CHUNK), 0)
    ss = lax.broadcasted_iota(jnp.int32, (CHUNK, CHUNK), 1)
    pos = lax.broadcasted_iota(jnp.int32, (CHUNK, 1), 0).astype(F32)
    log_g = _log_sigmoid(logit_ref[...])

    for d, (q_ref, k_ref, v_ref, o_ref) in enumerate(
            ((qf_ref, kf_ref, vf_ref, of_ref), (qb_ref, kb_ref, vb_ref, ob_ref))):
        dist = (tt - ss) if d == 0 else (ss - tt)
        distf = jnp.maximum(dist, 0).astype(F32)
        step = pos if d == 0 else (CHUNK - 1.0) - pos
        for h in range(RET_HEADS):
            c = d * RET_HEADS + h
            hs = slice(h * dk, (h + 1) * dk)
            lg = log_g[c:c + 1, 0:1]
            dmat = jnp.where(dist >= 0, jnp.exp(distf * lg), 0.0)
            q, k, v = q_ref[:, hs], k_ref[:, hs], v_ref[:, hs]
            state = st_ref[c]
            inner = _dot((_dot_nt(q, k) * dmat).astype(BF16), v)
            cross = jnp.exp((step + 1.0) * lg) * _dot(q, state.astype(BF16))
            o_ref[:, hs] = inner + cross
            k_dec = jnp.exp((CHUNK - 1.0 - step) * lg)
            st_ref[c] = jnp.exp(CHUNK * lg) * state + _dot_tn(k, (v.astype(F32) * k_dec).astype(BF16))


def _retscan(q, k, v, decay_logit, n_ctx):
    bsz, t, w = q.shape
    nc, ncc = t // CHUNK, n_ctx // CHUNK
    f, bk = _scan_specs(w, nc, ncc)
    logit = jnp.broadcast_to(decay_logit.astype(F32).reshape(2 * RET_HEADS, 1), (2 * RET_HEADS, LANES))
    return pl.pallas_call(
        _retscan_kernel,
        grid=(bsz, nc),
        in_specs=[f, f, f, bk, bk, bk, _const_spec((2 * RET_HEADS, LANES))],
        out_specs=[f, bk],
        out_shape=[jax.ShapeDtypeStruct((bsz, t, w), F32)] * 2,
        scratch_shapes=[pltpu.VMEM((2 * RET_HEADS, RET_HEAD_DIM, RET_HEAD_DIM), F32)],
        compiler_params=_cparams(("parallel", "arbitrary")),
        name="retention_scan",
    )(q, k, v, q, k, v, logit)


def _s5_weights(lam_re, lam_im, log_dt, b_re, b_im, c_re, c_im):
    hp = lax.Precision.HIGHEST
    n = S5_SUB
    dt = jnp.exp(log_dt)[..., None]
    zr, zi = lam_re * dt, lam_im * dt
    steps = jnp.arange(n + 1, dtype=F32)[:, None, None, None]
    pmag = jnp.exp(steps * zr)
    ak_re, ak_im = pmag * jnp.cos(steps * zi), pmag * jnp.sin(steps * zi)
    a_re, a_im = ak_re[1], ak_im[1]
    lam_sq = jnp.square(lam_re) + jnp.square(lam_im)
    e_re = ((a_re - 1.0) * lam_re + a_im * lam_im) / lam_sq
    e_im = (a_im * lam_re - (a_re - 1.0) * lam_im) / lam_sq
    bb_re = e_re[..., None] * b_re - e_im[..., None] * b_im
    bb_im = e_re[..., None] * b_im + e_im[..., None] * b_re
    ab_re = ak_re[..., None] * bb_re - ak_im[..., None] * bb_im
    ab_im = ak_re[..., None] * bb_im + ak_im[..., None] * bb_re
    kern = (jnp.einsum('dgqp,kdgpr->kdgqr', c_re, ab_re, precision=hp)
            - jnp.einsum('dgqp,kdgpr->kdgqr', c_im, ab_im, precision=hp))
    sig = jnp.arange(n)
    zeros = jnp.zeros_like(kern[:n])
    toe = jnp.stack([jnp.concatenate([zeros[:sp], kern[:n - sp]], axis=0) for sp in range(n)], axis=0)
    toe = toe.transpose(2, 3, 0, 5, 1, 4)
    vin_re = ab_re[n - 1 - sig].transpose(1, 2, 0, 4, 3)
    vin_im = ab_im[n - 1 - sig].transpose(1, 2, 0, 4, 3)
    ap_re, ap_im = ak_re[1:], ak_im[1:]
    w_re = c_re[None] * ap_re[:, :, :, None, :] - c_im[None] * ap_im[:, :, :, None, :]
    w_im = -(c_re[None] * ap_im[:, :, :, None, :] + c_im[None] * ap_re[:, :, :, None, :])
    w_re = w_re.transpose(1, 2, 4, 0, 3)
    w_im = w_im.transpose(1, 2, 4, 0, 3)

    def orient(a, axes):
        return jnp.stack([a[0], jnp.flip(a[1], axis=tuple(x - 1 for x in axes))], axis=0)

    toe = orient(toe, (2, 4))
    vin = orient(jnp.stack([vin_re, vin_im], axis=4), (2,))
    wout = orient(jnp.stack([w_re, w_im], axis=2), (4,))

    no = lam_re.shape[1] // S5_OCT
    ns = S5_OCT * S5_STATE

    def compact(a):
        a = a.reshape((2, no, S5_OCT) + a.shape[2:4] + (LANES,))
        return a.transpose(0, 1, 3, 2, 4, 5).reshape(2, no, -1, LANES).astype(BF16)

    t_c = compact(toe.reshape(toe.shape[:4] + (LANES,)))
    v_c = compact(vin.reshape(vin.shape[:4] + (LANES,)))
    w_c = compact(wout.reshape(wout.shape[:4] + (LANES,)))
    a_re = ak_re[n].reshape(2, no, 1, ns)
    a_im = ak_im[n].reshape(2, no, 1, ns)
    return t_c, v_c, w_c, a_re, a_im


def _s5_expand(c_ref, o_ref, row_blk, col_blk):
    n_rows, n_cols = o_ref.shape[1:]
    ent = lax.broadcasted_iota(jnp.int32, (LANES, n_cols), 0)
    col = lax.broadcasted_iota(jnp.int32, (LANES, n_cols), 1)
    spread = (ent == (col // (S5_OCT * col_blk)) * col_blk + col % col_blk).astype(BF16)
    rg = (lax.broadcasted_iota(jnp.int32, (n_rows, n_cols), 0) // row_blk) % S5_OCT
    cg = (lax.broadcasted_iota(jnp.int32, (n_rows, n_cols), 1) // col_blk) % S5_OCT
    for d in range(2):
        o_ref[d] = jnp.where(rg == cg, _dot(c_ref[d], spread), 0.0).astype(BF16)


def _s5_kernel(uf_ref, ub_ref, tc_ref, vc_ref, wc_ref, are_ref, aim_ref, yf_ref, yb_ref,
               t_ref, v_ref, w_ref, in_ref, xp_ref, st_ref, *, bsz, jt):
    ns = S5_OCT * S5_STATE
    rows = bsz * jt

    @pl.when(pl.program_id(1) == 0)
    def _init():
        st_ref[...] = jnp.zeros_like(st_ref)
        _s5_expand(tc_ref, t_ref, S5_GROUP, S5_GROUP)
        _s5_expand(vc_ref, v_ref, S5_GROUP, S5_STATE)
        _s5_expand(wc_ref, w_ref, S5_STATE, S5_GROUP)

    ri = lax.broadcasted_iota(jnp.int32, (rows, rows), 0)
    ci = lax.broadcasted_iota(jnp.int32, (rows, rows), 1)
    perm = lambda inner: (ci == (ri % inner) * (rows // inner) + ri // inner).astype(BF16)
    p_sj = perm(jt)
    p_jb = perm(bsz)
    p_bj = perm(jt)

    for d, (u_ref, y_ref) in enumerate(((uf_ref, yf_ref), (ub_ref, yb_ref))):
        u_rows = []
        for b in range(bsz):
            z = _dot(p_sj, u_ref[b].astype(BF16)).astype(BF16)
            u_rows.append(jnp.concatenate([z[s * jt:(s + 1) * jt] for s in range(S5_SUB)], axis=1))
        u = jnp.concatenate(u_rows, axis=0)
        y_intra = _dot(u, t_ref[d])
        u_jb = _dot(p_jb, u).astype(BF16)
        in_ref[d] = _dot(u_jb, v_ref[d])
        a_re, a_im = are_ref[d], aim_ref[d]
        xr, xi = st_ref[d, 0], st_ref[d, 1]
        for jj in range(jt):
            j = jj if d == 0 else jt - 1 - jj
            rows_j = slice(j * bsz, (j + 1) * bsz)
            xp_ref[d, rows_j, :] = jnp.concatenate([xr, xi], axis=1)
            inc = in_ref[d, rows_j, :]
            xr, xi = (a_re * xr - a_im * xi + inc[:, :ns], a_re * xi + a_im * xr + inc[:, ns:])
        st_ref[d, 0] = xr
        st_ref[d, 1] = xi
        xp = _dot(p_bj, xp_ref[d].astype(BF16)).astype(BF16)
        y = y_intra + _dot(xp, w_ref[d])
        for b in range(bsz):
            for s in range(S5_SUB):
                y_ref[b, pl.ds(s, jt, stride=S5_SUB), :] = y[b * jt:(b + 1) * jt, s * LANES:(s + 1) * LANES]


def _s5(u, t_c, v_c, w_c, a_re, a_im, n_ctx):
    bsz, t, width = u.shape
    assert bsz == S5_SUB
    no = width // LANES
    nt, nct = t // TM, n_ctx // TM
    jt = TM // S5_SUB
    nq, ns = S5_SUB * LANES, S5_OCT * S5_STATE
    fwd = pl.BlockSpec((bsz, TM, LANES), lambda o, i: (0, i, o))
    bwd = pl.BlockSpec((bsz, TM, LANES), lambda o, i: (0, jnp.where(i < nct, nct - 1 - i, nt + nct - 1 - i), o))
    op = lambda r, c: pl.BlockSpec((2, None, r, c), lambda o, i: (0, o, 0, 0))
    return pl.pallas_call(
        functools.partial(_s5_kernel, bsz=bsz, jt=jt),
        grid=(no, nt),
        in_specs=[fwd, bwd, op(nq, LANES), op(nq, LANES), op(2 * ns, LANES), op(1, ns), op(1, ns)],
        out_specs=[fwd, bwd],
        out_shape=[jax.ShapeDtypeStruct((bsz, t, width), F32)] * 2,
        scratch_shapes=[pltpu.VMEM((2, nq, nq), BF16), pltpu.VMEM((2, nq, 2 * ns), BF16),
                        pltpu.VMEM((2, 2 * ns, nq), BF16),
                        pltpu.VMEM((2, bsz * jt, 2 * ns), F32), pltpu.VMEM((2, bsz * jt, 2 * ns), F32),
                        pltpu.VMEM((2, 2, bsz, ns), F32)],
        compiler_params=_cparams(("parallel", "arbitrary")),
        name="s5",
    )(u, u, t_c, v_c, w_c, a_re, a_im)


def _out_even_kernel(x_ref, mod_ref, na_ref, hf_ref, hb_ref, zo_ref, gn_ref, w_ref, lng_ref, lnb_ref,
                     o_ref, *, alpha):
    ml = _sigmoid(zo_ref[...]) * _head_norm(hf_ref[...] + hb_ref[...], gn_ref[...], ML_HEADS, ML_HEAD_DIM)
    y = _dot(na_ref[...].astype(BF16), w_ref[0:HALF, :]) + _dot(ml.astype(BF16), w_ref[HALF:, :])
    r = alpha * x_ref[...] + _mod_row(mod_ref, 1, 2) * y
    o_ref[...] = _layer_norm(r, lng_ref[...], lnb_ref[...])


def _out_odd_kernel(x_ref, mod_ref, ysf_ref, ysb_ref, u_ref, dsk_ref, gw_ref, gb_ref, rf_ref, rb_ref, gr_ref,
                    gn_ref, w_ref, lng_ref, lnb_ref, o_ref, *, alpha):
    s = _gelu_tanh(ysf_ref[...] + ysb_ref[...] + dsk_ref[...] * u_ref[...])
    s = s * _sigmoid(_dot(s.astype(BF16), gw_ref[...]) + gb_ref[...])
    ret = _silu(gr_ref[...]) * _head_norm(rf_ref[...] + rb_ref[...], gn_ref[...], RET_HEADS, RET_HEAD_DIM)
    y = _dot(s.astype(BF16), w_ref[0:HALF, :]) + _dot(ret.astype(BF16), w_ref[HALF:, :])
    r = alpha * x_ref[...] + _mod_row(mod_ref, 1, 2) * y
    o_ref[...] = _layer_norm(r, lng_ref[...], lnb_ref[...])


def _out_call(kernel, name, xa, mod, extra, alpha, n_ctx_tiles):
    bsz, t, d = xa.shape
    in_specs = [_tok_spec(d), _mod_block(d, bsz, n_ctx_tiles)]
    args = [xa, mod]
    for kind, a in extra:
        in_specs.append(_tok_spec(a.shape[-1]) if kind == 't' else _const_spec(a.shape))
        args.append(a)
    return pl.pallas_call(
        functools.partial(kernel, alpha=alpha),
        grid=(bsz, t // TM),
        in_specs=in_specs,
        out_specs=_tok_spec(d),
        out_shape=jax.ShapeDtypeStruct((bsz, t, d), F32),
        compiler_params=_cparams(("parallel", "parallel")),
        name=name,
    )(*args)


def _rope_tables(n_ctx, seq):
    nf = RET_HEAD_DIM // 4
    freqs = ROPE_BASE ** (-jnp.arange(nf, dtype=F32) / nf)
    tok = jnp.arange(seq)
    ang_r = (tok // GRID_W).astype(F32)[:, None] * freqs
    ang_c = (tok % GRID_W).astype(F32)[:, None] * freqs
    cos = jnp.concatenate([jnp.cos(ang_r)] * 2 + [jnp.cos(ang_c)] * 2, axis=-1)
    sin = jnp.concatenate([-jnp.sin(ang_r), jnp.sin(ang_r), -jnp.sin(ang_c), jnp.sin(ang_c)], axis=-1)
    cos = jnp.concatenate([jnp.ones((n_ctx, LANES), F32), cos], axis=0)
    sin = jnp.concatenate([jnp.zeros((n_ctx, LANES), F32), sin], axis=0)
    return cos, sin


def _even_layer_mixer(xa, mod, w_in, w_out, rpb, conv_w, conv_b, wq, wk, i_bias, f_bias, gn_w,
                      lng, lnb, alpha, n_ctx):
    d = xa.shape[-1]
    n_ctx_tiles = n_ctx // TM
    ng = 2 * ML_HEADS
    main = 6 * HALF
    w_gates = jnp.zeros((d, 2 * LANES), F32)
    w_gates = w_gates.at[:, 0:ng].set(w_in[:, main:main + ng]).at[:, LANES:LANES + ng].set(w_in[:, main + ng:])
    w_all = jnp.concatenate([w_in[:, :main], w_gates], axis=1).astype(BF16)
    gate_bias = jnp.zeros((1, 2 * LANES), F32)
    gate_bias = gate_bias.at[0, 0:ng].set(i_bias.reshape(ng)).at[0, LANES:LANES + ng].set(f_bias.reshape(ng))
    outs = [(0, HALF, NA_HEAD_DIM ** -0.5, False), (HALF, HALF, 1.0, False), (2 * HALF, HALF, 1.0, False),
            (3 * HALF, HALF, 1.0, False), (4 * HALF, HALF, 1.0, False), (5 * HALF, HALF, 1.0, False),
            (main, 2 * LANES, 1.0, False)]
    dts = [BF16, BF16, BF16, F32, BF16, F32, F32]
    q_na, k_na, v_na, xm, zv, zo, gates = _inproj(xa, mod, w_all, outs, dts, n_ctx_tiles)
    na = _na(q_na, k_na, v_na, rpb, n_ctx)
    q_ml, k_ml = _mlprep(xm, conv_w, conv_b, wq.astype(BF16), wk.astype(BF16), n_ctx)
    hf, hb = _mlscan(q_ml, k_ml, zv, gates, gate_bias, n_ctx)
    return _out_call(
        _out_even_kernel, "out_even", xa, mod,
        [('t', na), ('t', hf), ('t', hb), ('t', zo), ('c', gn_w.reshape(1, HALF)), ('c', w_out.astype(BF16)),
         ('c', lng.reshape(1, d)), ('c', lnb.reshape(1, d))], alpha, n_ctx_tiles)


def _odd_layer_mixer(xa, mod, w_in, w_out, s5_params, d_skip, glu_w, glu_b, decay_logit, gn_w,
                     lng, lnb, alpha, n_ctx, rope_tabs):
    d = xa.shape[-1]
    n_ctx_tiles = n_ctx // TM
    outs = [(0, HALF, 1.0, False), (HALF, HALF, 1.0, True), (2 * HALF, HALF, RET_HEAD_DIM ** -0.5, True),
            (3 * HALF, HALF, 1.0, False), (4 * HALF, HALF, 1.0, False)]
    dts = [F32, BF16, BF16, BF16, F32]
    u, q_r, k_r, v_r, g_r = _inproj(xa, mod, w_in.astype(BF16), outs, dts, n_ctx_tiles, rope_tabs)
    ysf, ysb = _s5(u, *_s5_weights(*s5_params), n_ctx)
    rf, rb = _retscan(q_r, k_r, v_r, decay_logit, n_ctx)
    return _out_call(
        _out_odd_kernel, "out_odd", xa, mod,
        [('t', ysf), ('t', ysb), ('t', u), ('c', d_skip.reshape(1, HALF)), ('c', glu_w.astype(BF16)),
         ('c', glu_b.reshape(1, HALF)), ('t', rf), ('t', rb), ('t', g_r), ('c', gn_w.reshape(1, HALF)),
         ('c', w_out.astype(BF16)), ('c', lng.reshape(1, d)), ('c', lnb.reshape(1, d))],
        alpha, n_ctx_tiles)


def kernel(x, c, ctx, c_ctx, ada_w, ada_b, ffn_w_gate, ffn_w_up, ffn_w_down, ln_g, ln_b, ev_w_in, ev_w_out, na_rpb, ml_conv_w, ml_conv_b, ml_wq, ml_wk, ml_i_bias, ml_f_bias, ml_gn_w, od_w_in, od_w_out, s5_lam_re, s5_lam_im, s5_log_dt, s5_b_re, s5_b_im, s5_c_re, s5_c_im, s5_d, s5_glu_w, s5_glu_b, ret_decay_logit, ret_gn_w):
    bsz, seq, d = x.shape
    n_ctx = ctx.shape[1]
    depth = ada_w.shape[0]
    assert n_ctx % TM == 0 and seq % TM == 0 and seq % GRID_W == 0 and bsz == SUBLANES
    n_ctx_tiles = n_ctx // TM
    alpha = (2.0 * depth) ** 0.25

    rows = bsz + SUBLANES
    c_rows = jnp.zeros((rows, d), F32).at[:bsz].set(c).at[bsz].set(c_ctx)
    mod_all = _mod_table(c_rows, ada_w, ada_b)

    xa = x
    rope_tabs = _rope_tables(n_ctx, seq)
    for l in range(depth):
        e = l // 2
        mod = mod_all[l]
        ffn = lambda j, k, h, **kw: _ffn(h, mod, j, ffn_w_gate[l, k].astype(BF16), ffn_w_up[l, k].astype(BF16),
                                         ffn_w_down[l, k].astype(BF16), ln_g[l, j], ln_b[l, j], alpha,
                                         n_ctx_tiles, **kw)
        xa = ffn(0, 0, xa, ctx=ctx) if l == 0 else ffn(0, 0, xa)
        if l % 2 == 0:
            xa = _even_layer_mixer(xa, mod, ev_w_in[e], ev_w_out[e], na_rpb[e], ml_conv_w[e], ml_conv_b[e],
                                   ml_wq[e], ml_wk[e], ml_i_bias[e], ml_f_bias[e], ml_gn_w[e],
                                   ln_g[l, 1], ln_b[l, 1], alpha, n_ctx)
        else:
            s5_params = (s5_lam_re[e], s5_lam_im[e], s5_log_dt[e], s5_b_re[e], s5_b_im[e], s5_c_re[e], s5_c_im[e])
            xa = _odd_layer_mixer(xa, mod, od_w_in[e], od_w_out[e], s5_params, s5_d[e], s5_glu_w[e], s5_glu_b[e],
                                  ret_decay_logit[e], ret_gn_w[e], ln_g[l, 1], ln_b[l, 1], alpha, n_ctx, rope_tabs)
        xa = ffn(2, 1, xa, latent_only=(l == depth - 1))
    return xa
```

```python
import functools
import math

import jax
import jax.numpy as jnp
from jax import lax
from jax.experimental import pallas as pl
from jax.experimental.pallas import tpu as pltpu

F32 = jnp.float32
BF16 = jnp.bfloat16

GRID_W = 64
LN_EPS = 1e-5
N_MOD = 9
NA_HEADS, NA_HEAD_DIM, NA_WIN_H, NA_WIN_W = 8, 64, 8, 16
ML_HEADS, ML_HEAD_DIM, ML_CONV = 4, 128, 5
S5_GROUP, S5_GROUPS, S5_STATE = 16, 32, 64
RET_HEADS, RET_HEAD_DIM = 4, 128
ROPE_BASE = 10000.0
HALF = 512

LANES = 128
SUBLANES = 8
VMEM_LIMIT = 56 * 1024 * 1024

TM = 256
CHUNK = 256
S5_SUB = 8
S5_OCT = LANES // S5_GROUP
NEG = -1e30


def _cparams(sem):
    return pltpu.CompilerParams(dimension_semantics=sem, vmem_limit_bytes=VMEM_LIMIT)


def _const_spec(shape):
    nd = len(shape)
    return pl.BlockSpec(shape, lambda *_: (0,) * nd, pipeline_mode=pl.Buffered(1))


def _dot(a, b):
    return jnp.dot(a, b, preferred_element_type=F32)


def _dot_nt(a, b):
    return lax.dot_general(a, b, (((1,), (1,)), ((), ())), preferred_element_type=F32)


def _dot_tn(a, b):
    return lax.dot_general(a, b, (((0,), (0,)), ((), ())), preferred_element_type=F32)


def _sigmoid(x):
    return 1.0 / (1.0 + jnp.exp(-x))


def _silu(x):
    return x * _sigmoid(x)


def _log_sigmoid(x):
    return jnp.minimum(x, 0.0) - jnp.log1p(jnp.exp(-jnp.abs(x)))


def _gelu_tanh(x):
    return 0.5 * x * (1.0 + jnp.tanh(math.sqrt(2.0 / math.pi) * (x + 0.044715 * (x * x * x))))


def _layer_norm(r, g, b):
    mu = jnp.mean(r, axis=-1, keepdims=True)
    c = r - mu
    var = jnp.mean(c * c, axis=-1, keepdims=True)
    return c * lax.rsqrt(var + LN_EPS) * g + b


def _head_norm(h, w, n_heads, head_dim):
    parts = []
    for k in range(n_heads):
        hk = h[:, k * head_dim:(k + 1) * head_dim]
        mu = jnp.mean(hk, axis=-1, keepdims=True)
        c = hk - mu
        var = jnp.mean(c * c, axis=-1, keepdims=True)
        parts.append(c * lax.rsqrt(var + LN_EPS))
    return jnp.concatenate(parts, axis=-1) * w


def _mod_row(mod_ref, j, k):
    i = 3 * j + k
    return mod_ref[i:i + 1, :]


def _mod_kernel(c_ref, w_ref, b_ref, o_ref):
    c = c_ref[...]
    s = _silu(c).astype(BF16)
    o_ref[...] = _dot(s, w_ref[...].astype(BF16)) + b_ref[...]


def _mod_table(c_rows, ada_w, ada_b):
    depth, d, n = ada_w.shape
    rows = c_rows.shape[0]
    tn = 1024
    out = pl.pallas_call(
        _mod_kernel,
        grid=(depth, n // tn),
        in_specs=[pl.BlockSpec((rows, d), lambda l, j: (0, 0)),
                  pl.BlockSpec((None, d, tn), lambda l, j: (l, 0, j)),
                  pl.BlockSpec((None, 1, tn), lambda l, j: (l, 0, j))],
        out_specs=pl.BlockSpec((None, rows, tn), lambda l, j: (l, 0, j)),
        out_shape=jax.ShapeDtypeStruct((depth, rows, n), F32),
        compiler_params=_cparams(("parallel", "parallel")),
        name="mod_table",
    )(c_rows, ada_w, ada_b.reshape(depth, 1, n))
    return out.reshape(depth, rows, N_MOD, d)


PAIR = 2


def _tok_spec(width, tm=TM):
    return pl.BlockSpec((None, tm, width), lambda b, t: (b, t, 0))


def _tile_index(h, nt, nct, bsz, latent_only):
    if latent_only:
        per = (nt - nct) // PAIR

        def index(i):
            b = i // per
            return b, nct + PAIR * (i % per) + h, b
    else:
        def index(i):
            g = PAIR * i + h
            b, t = g // nt, g % nt
            return b, t, jnp.where(t < nct, bsz, b)
    return index


def _token_call(body, name, tok, mod, tables, consts, out_widths, out_dtypes, nt, nct, latent_only=False):
    bsz, d = tok[0][0].shape[0], mod.shape[-1]
    n_tiles = bsz * ((nt - nct) if latent_only else nt)
    assert n_tiles % PAIR == 0 and (not latent_only or (nt - nct) % PAIR == 0)
    index = [_tile_index(h, nt, nct, bsz, latent_only) for h in range(PAIR)]
    per = TM // SUBLANES
    last = nt * per - 1
    tile_of = {'s': lambda t: t, 'c': lambda t: jnp.minimum(t, nct - 1), 'l': lambda t: jnp.maximum(t - nct, 0),
               'p': lambda t: jnp.maximum(t * per - 1, 0), 'n': lambda t: jnp.minimum((t + 1) * per, last)}
    in_specs, args = [], []
    for h in range(PAIR):
        for arr, kind in tok:
            in_specs.append(pl.BlockSpec(
                (None, SUBLANES if kind in 'pn' else TM, arr.shape[-1]),
                lambda i, f=index[h], k=tile_of[kind]: (f(i)[0], k(f(i)[1]), 0)))
            args.append(arr)
        in_specs.append(pl.BlockSpec((None, N_MOD, d), lambda i, f=index[h]: (f(i)[2], 0, 0)))
        args.append(mod)
        for tab in tables:
            in_specs.append(pl.BlockSpec((TM, tab.shape[-1]), lambda i, f=index[h]: (f(i)[1], 0)))
            args.append(tab)
    in_specs += [_const_spec(c.shape) for c in consts]
    args += list(consts)
    n_half = len(tok) + 1 + len(tables)

    def kern(*refs):
        const_refs = refs[PAIR * n_half:PAIR * n_half + len(consts)]
        out_refs = refs[PAIR * n_half + len(consts):]
        for h in range(PAIR):
            hr = refs[h * n_half:(h + 1) * n_half]
            tile = index[h](pl.program_id(0))[1]
            res = body(tile, hr[:len(tok)], hr[len(tok)], hr[len(tok) + 1:], const_refs)
            for o_ref, r in zip(out_refs, res):
                o_ref[h * TM:(h + 1) * TM, :] = r.astype(o_ref.dtype)

    outs = pl.pallas_call(
        kern,
        grid=(n_tiles // PAIR,),
        in_specs=in_specs,
        out_specs=[pl.BlockSpec((PAIR * TM, w), lambda i: (i, 0)) for w in out_widths],
        out_shape=[jax.ShapeDtypeStruct((n_tiles * TM, w), dt) for w, dt in zip(out_widths, out_dtypes)],
        compiler_params=_cparams(("parallel",)),
        name=name,
    )(*args)
    return [o.reshape(bsz, -1, o.shape[-1]) for o in outs]


def _ffn_body(tile, tok, mod_ref, tables, consts, *, j, alpha, nct):
    wg_ref, wu_ref, wd_ref, lng_ref, lnb_ref = consts
    if len(tok) == 2:
        x = jnp.where(tile < nct, tok[0][...], tok[1][...])
    else:
        x = tok[0][...]
    h = (x * (1.0 + _mod_row(mod_ref, j, 1)) + _mod_row(mod_ref, j, 0)).astype(BF16)
    g = _dot(h, wg_ref[...])
    u = _dot(h, wu_ref[...])
    a = (_silu(g) * u).astype(BF16)
    y = _dot(a, wd_ref[...])
    r = alpha * x + (0.5 * _mod_row(mod_ref, j, 2)) * y
    return [_layer_norm(r, lng_ref[...], lnb_ref[...])]


def _ffn(xa, mod, j, wg, wu, wd, lng, lnb, alpha, n_ctx_tiles, ctx=None, latent_only=False):
    d = xa.shape[-1]
    nct = n_ctx_tiles
    tok = [(xa, 's')] if ctx is None else [(ctx, 'c'), (xa, 'l')]
    nt = xa.shape[1] // TM + (0 if ctx is None else nct)
    return _token_call(
        functools.partial(_ffn_body, j=j, alpha=alpha, nct=nct), "ffn", tok, mod, [],
        [wg, wu, wd, lng.reshape(1, d), lnb.reshape(1, d)], [d], [F32], nt, nct, latent_only)[0]


def _rope(z, cos, sin):
    lane = lax.broadcasted_iota(jnp.int32, (1, LANES), 1)
    first = (lane % 64) < 32
    parts = []
    for k in range(z.shape[1] // LANES):
        zk = z[:, k * LANES:(k + 1) * LANES]
        partner = jnp.where(first, pltpu.roll(zk, LANES - 32, axis=1), pltpu.roll(zk, 32, axis=1))
        parts.append(zk * cos + partner * sin)
    return jnp.concatenate(parts, axis=-1)


def _inproj_body(tile, tok, mod_ref, tables, consts, *, outs, conv):
    w_ref = consts[0]
    modulate = lambda x: (x * (1.0 + _mod_row(mod_ref, 1, 1)) + _mod_row(mod_ref, 1, 0)).astype(BF16)
    h = modulate(tok[0][...])
    res = []
    for off, width, scale, rope in outs:
        z = _dot(h, w_ref[:, off:off + width])
        if rope:
            z = _rope(z, tables[0][...], tables[1][...])
        if scale != 1.0:
            z = z * scale
        res.append(z)
    if conv is not None:
        off, nct, nt = conv
        cw_ref, cb_ref, wq_ref, wk_ref = consts[1:]
        wx = w_ref[:, off:off + HALF]
        has_prev = jnp.logical_and(tile != 0, tile != nct)
        has_next = jnp.logical_and(tile != nct - 1, tile != nt - 1)
        prev = jnp.where(has_prev, _dot(modulate(tok[1][...]), wx), 0.0)
        nxt = jnp.where(has_next, _dot(modulate(tok[2][...]), wx), 0.0)
        ext = jnp.concatenate([prev, _dot(h, wx), nxt], axis=0)
        acc = cb_ref[...]
        for j in range(ML_CONV):
            o = SUBLANES + j - ML_CONV // 2
            acc = acc + cw_ref[j:j + 1, :] * ext[o:o + TM, :]
        xc = _silu(acc).astype(BF16)
        heads = [slice(k * ML_HEAD_DIM, (k + 1) * ML_HEAD_DIM) for k in range(ML_HEADS)]
        res.append(jnp.concatenate([_dot(xc[:, hs], wq_ref[k]) for k, hs in enumerate(heads)], axis=-1))
        res.append(jnp.concatenate([_dot(xc[:, hs], wk_ref[k]) for k, hs in enumerate(heads)], axis=-1)
                   * ML_HEAD_DIM ** -0.5)
    return res


def _inproj(xa, mod, w, outs, dtypes, n_ctx_tiles, rope_tabs=(), conv=None):
    nt = xa.shape[1] // TM
    tok, consts, widths, conv_static = [(xa, 's')], [w], [o[1] for o in outs], None
    if conv is not None:
        tok += [(xa, 'p'), (xa, 'n')]
        consts += list(conv[1:])
        widths += [HALF, HALF]
        conv_static = (conv[0], n_ctx_tiles, nt)
    return _token_call(
        functools.partial(_inproj_body, outs=tuple(outs), conv=conv_static), "inproj", tok, mod, list(rope_tabs),
        consts, widths, dtypes, nt, n_ctx_tiles)


NA_ROWS_PER_TILE = TM // GRID_W
NA_SLAB_ROWS = NA_WIN_H + NA_ROWS_PER_TILE


def _softmax_pv(s_list, v_list):
    m = s_list[0].max(axis=-1, keepdims=True)
    for s in s_list[1:]:
        m = jnp.maximum(m, s.max(axis=-1, keepdims=True))
    acc, den = None, None
    for s, v in zip(s_list, v_list):
        p = jnp.exp(s - m)
        l = p.sum(axis=-1, keepdims=True)
        o = _dot(p.astype(BF16), v)
        acc = o if acc is None else acc + o
        den = l if den is None else den + l
    return acc / den


def _na_kernel(q_ref, k_ref, v_ref, bias_ref, o_ref, *, n_ctx, rows, n_ctx_tiles):
    t = pl.program_id(1)
    n_loc = NA_SLAB_ROWS * GRID_W
    lane = lax.broadcasted_iota(jnp.int32, (1, LANES), 1)
    low = lane < NA_HEAD_DIM
    zero = jnp.zeros((), BF16)

    def pair(p, keys):
        ls = slice(p * LANES, (p + 1) * LANES)
        q2 = q_ref[:, ls]
        outs = []
        for e in range(2):
            qm = jnp.where(low if e == 0 else jnp.logical_not(low), q2, zero)
            s_list, v_list = [], []
            for kk, vv, bias in keys(ls, 2 * p + e):
                s = _dot_nt(qm, kk)
                s_list.append(s if bias is None else s + bias)
                v_list.append(vv)
            outs.append(_softmax_pv(s_list, v_list))
        o_ref[:, ls] = jnp.where(low, outs[0], outs[1]).astype(o_ref.dtype)

    @pl.when(t < n_ctx_tiles)
    def _ctx():
        for p in range(NA_HEADS // 2):
            pair(p, lambda ls, h: [(k_ref[0:n_ctx, ls], v_ref[0:n_ctx, ls], None)])

    @pl.when(t >= n_ctx_tiles)
    def _latent():
        r0 = (t - n_ctx_tiles) * NA_ROWS_PER_TILE
        rs0 = jnp.clip(r0 - NA_WIN_H // 2, 0, rows - NA_SLAB_ROWS)
        var = (r0 - rs0) // NA_ROWS_PER_TILE
        start = pl.multiple_of(n_ctx + rs0 * GRID_W, GRID_W)
        for p in range(NA_HEADS // 2):
            pair(p, lambda ls, h: [(k_ref[pl.ds(start, n_loc), ls], v_ref[pl.ds(start, n_loc), ls], bias_ref[var, h]),
                                   (k_ref[0:n_ctx, ls], v_ref[0:n_ctx, ls], None)])


def _na_bias_table(rpb, rows):
    rpt, slab = NA_ROWS_PER_TILE, NA_SLAB_ROWS
    cols = jnp.arange(GRID_W)
    cs = jnp.clip(cols - NA_WIN_W // 2, 0, GRID_W - NA_WIN_W)
    kc = jnp.arange(GRID_W)
    in_win = (kc[None, :] >= cs[:, None]) & (kc[None, :] < cs[:, None] + NA_WIN_W)
    col_off = kc[None, :] - cols[:, None] + NA_WIN_W - 1
    diff = (jnp.arange(3) * rpt)[:, None, None]
    i = jnp.arange(rpt)[None, :, None]
    a = jnp.arange(slab)[None, None, :]
    w0 = jnp.clip(diff + i - NA_WIN_H // 2, 0, slab - NA_WIN_H)
    row_ok = (a >= w0) & (a < w0 + NA_WIN_H)
    row_off = a - diff - i + NA_WIN_H - 1
    row_sel = ((row_off[..., None] == jnp.arange(2 * NA_WIN_H - 1)) & row_ok[..., None]).astype(F32)
    col_sel = (col_off[:, :, None] == jnp.arange(2 * NA_WIN_W - 1)).astype(F32)
    tab = jnp.einsum('hrc,viar,qkc->vhiqak', rpb.astype(F32), row_sel, col_sel, precision=lax.Precision.HIGHEST)
    ok = row_ok[:, None, :, None, :, None] & in_win[None, None, None, :, None, :]
    tab = jnp.where(ok, tab, NEG)
    return tab.reshape(3, NA_HEADS, rpt * GRID_W, slab * GRID_W)


def _na(q, k, v, rpb, n_ctx):
    bsz, t, w = q.shape
    rows = (t - n_ctx) // GRID_W
    assert rows >= NA_SLAB_ROWS and rows % NA_ROWS_PER_TILE == 0
    bias = _na_bias_table(rpb, rows)
    return pl.pallas_call(
        functools.partial(_na_kernel, n_ctx=n_ctx, rows=rows, n_ctx_tiles=n_ctx // TM),
        grid=(bsz, t // TM),
        in_specs=[_tok_spec(w),
                  pl.BlockSpec((None, t, w), lambda b, i: (b, 0, 0)),
                  pl.BlockSpec((None, t, w), lambda b, i: (b, 0, 0)),
                  _const_spec(bias.shape)],
        out_specs=_tok_spec(w),
        out_shape=jax.ShapeDtypeStruct((bsz, t, w), BF16),
        compiler_params=_cparams(("parallel", "arbitrary")),
        name="natten",
    )(q, k, v, bias)


def _scan_specs(width, n_chunks, n_ctx_chunks):
    fwd = pl.BlockSpec((None, CHUNK, width), lambda b, i: (b, i, 0))
    bwd = pl.BlockSpec(
        (None, CHUNK, width),
        lambda b, i: (b, jnp.where(i < n_ctx_chunks, n_ctx_chunks - 1 - i, n_chunks + n_ctx_chunks - 1 - i), 0))
    return fwd, bwd


def _split3(x):
    hi = x.astype(BF16)
    r1 = x - hi.astype(F32)
    mid = r1.astype(BF16)
    lo = (r1 - mid.astype(F32)).astype(BF16)
    return hi, mid, lo


def _mlscan_kernel(qf_ref, kf_ref, vf_ref, gf_ref, qb_ref, kb_ref, vb_ref, gb_ref, bias_ref,
                   of_ref, ob_ref, st_ref, m_ref):
    i = pl.program_id(1)
    dk = ML_HEAD_DIM

    @pl.when(i == 0)
    def _init():
        st_ref[...] = jnp.zeros_like(st_ref)
        m_ref[...] = jnp.zeros_like(m_ref)

    tt = lax.broadcasted_iota(jnp.int32, (CHUNK, CHUNK), 0)
    ss = lax.broadcasted_iota(jnp.int32, (CHUNK, CHUNK), 1)
    lane = lax.broadcasted_iota(jnp.int32, (CHUNK, LANES), 1)
    row = lax.broadcasted_iota(jnp.int32, (CHUNK, LANES), 0)
    ones_col = (lane == 0).astype(BF16)

    def running_max(x, reverse):
        sh = 1
        while sh < CHUNK:
            if reverse:
                shifted = jnp.where(row < CHUNK - sh, pltpu.roll(x, CHUNK - sh, axis=0), -jnp.inf)
            else:
                shifted = jnp.where(row >= sh, pltpu.roll(x, sh, axis=0), -jnp.inf)
            x = jnp.maximum(x, shifted)
            sh *= 2
        return x

    for d, (q_ref, k_ref, v_ref, g_ref, o_ref) in enumerate(
            ((qf_ref, kf_ref, vf_ref, gf_ref, of_ref), (qb_ref, kb_ref, vb_ref, gb_ref, ob_ref))):
        causal = (ss <= tt) if d == 0 else (ss >= tt)
        tri = causal.astype(BF16)
        gates = g_ref[...] + bias_ref[...]
        hi, mid, lo = _split3(_log_sigmoid(gates[:, LANES:]))
        bc = _dot(tri, hi) + _dot(tri, mid) + _dot(tri, lo)
        e = gates[:, :LANES] - bc
        e_t = e.T
        m_prev = m_ref[d:d + 1, :]
        g = jnp.maximum(running_max(e, reverse=d == 1), m_prev)
        w_inter = jnp.exp(m_prev - g)
        exp_neg_mt = jnp.exp(-(bc + g))
        last = CHUNK - 1 if d == 0 else 0
        b_last = bc[last:last + 1, :]
        w_log = b_last + e
        m_new = jnp.maximum(b_last + m_prev, w_log.max(axis=0, keepdims=True))
        w_in = jnp.exp(w_log - m_new)
        decay = jnp.exp(b_last + m_prev - m_new)
        m_ref[d:d + 1, :] = m_new
        for h in range(ML_HEADS):
            c = d * ML_HEADS + h
            hs = slice(h * dk, (h + 1) * dk)
            q, k = q_ref[:, hs], k_ref[:, hs]
            v_aug = jnp.concatenate([v_ref[:, hs], ones_col], axis=-1)
            p = jnp.exp(jnp.where(causal, e_t[c:c + 1, :] - g[:, c:c + 1], -jnp.inf))
            s = _dot_nt(q, k) * p
            state = st_ref[c]
            res = _dot(s.astype(BF16), v_aug) + w_inter[:, c:c + 1] * _dot(q, state.astype(BF16))
            inv = 1.0 / jnp.maximum(jnp.abs(res[:, dk:dk + 1]), exp_neg_mt[:, c:c + 1])
            o_ref[:, hs] = res[:, :dk] * inv
            st_ref[c] = decay[:, c:c + 1] * state + _dot_tn(k, (v_aug * w_in[:, c:c + 1]).astype(BF16))


def _mlscan(q, k, v, gates, gate_bias, n_ctx):
    bsz, t, w = q.shape
    nc, ncc = t // CHUNK, n_ctx // CHUNK
    f, bk = _scan_specs(w, nc, ncc)
    gf, gb = _scan_specs(2 * LANES, nc, ncc)
    n_chain = 2 * ML_HEADS
    return pl.pallas_call(
        _mlscan_kernel,
        grid=(bsz, nc),
        in_specs=[f, f, f, gf, bk, bk, bk, gb, _const_spec((1, 2 * LANES))],
        out_specs=[f, bk],
        out_shape=[jax.ShapeDtypeStruct((bsz, t, w), F32)] * 2,
        scratch_shapes=[pltpu.VMEM((n_chain, ML_HEAD_DIM, 2 * ML_HEAD_DIM), F32),
                        pltpu.VMEM((SUBLANES, LANES), F32)],
        compiler_params=_cparams(("parallel", "arbitrary")),
        name="mlstm_scan",
    )(q, k, v, gates, q, k, v, gates, gate_bias)


def _retscan_kernel(qf_ref, kf_ref, vf_ref, qb_ref, kb_ref, vb_ref, logit_ref, of_ref, ob_ref, st_ref):
    i = pl.program_id(1)
    dk = RET_HEAD_DIM

    @pl.when(i == 0)
    def _init():
        st_ref[...] = jnp.zeros_like(st_ref)

    tt = lax.broadcasted_iota(jnp.int32, (CHUNK, CHUNK), 0)
    ss = lax.broadcasted_iota(jnp.int32, (CHUNK, CHUNK), 1)
    pos = lax.broadcasted_iota(jnp.int32, (CHUNK, 1), 0).astype(F32)
    log_g = _log_sigmoid(logit_ref[...])

    for d, (q_ref, k_ref, v_ref, o_ref) in enumerate(
            ((qf_ref, kf_ref, vf_ref, of_ref), (qb_ref, kb_ref, vb_ref, ob_ref))):
        dist = (tt - ss) if d == 0 else (ss - tt)
        distf = jnp.maximum(dist, 0).astype(F32)
        step = pos if d == 0 else (CHUNK - 1.0) - pos
        for h in range(RET_HEADS):
            c = d * RET_HEADS + h
            hs = slice(h * dk, (h + 1) * dk)
            lg = log_g[c:c + 1, 0:1]
            dmat = jnp.where(dist >= 0, jnp.exp(distf * lg), 0.0)
            q, k, v = q_ref[:, hs], k_ref[:, hs], v_ref[:, hs]
            state = st_ref[c]
            inner = _dot((_dot_nt(q, k) * dmat).astype(BF16), v)
            cross = jnp.exp((step + 1.0) * lg) * _dot(q, state.astype(BF16))
            o_ref[:, hs] = inner + cross
            k_dec = jnp.exp((CHUNK - 1.0 - step) * lg)
            st_ref[c] = jnp.exp(CHUNK * lg) * state + _dot_tn(k, (v.astype(F32) * k_dec).astype(BF16))


def _retscan(q, k, v, decay_logit, n_ctx):
    bsz, t, w = q.shape
    nc, ncc = t // CHUNK, n_ctx // CHUNK
    f, bk = _scan_specs(w, nc, ncc)
    logit = jnp.broadcast_to(decay_logit.astype(F32).reshape(2 * RET_HEADS, 1), (2 * RET_HEADS, LANES))
    return pl.pallas_call(
        _retscan_kernel,
        grid=(bsz, nc),
        in_specs=[f, f, f, bk, bk, bk, _const_spec((2 * RET_HEADS, LANES))],
        out_specs=[f, bk],
        out_shape=[jax.ShapeDtypeStruct((bsz, t, w), F32)] * 2,
        scratch_shapes=[pltpu.VMEM((2 * RET_HEADS, RET_HEAD_DIM, RET_HEAD_DIM), F32)],
        compiler_params=_cparams(("parallel", "arbitrary")),
        name="retention_scan",
    )(q, k, v, q, k, v, logit)


def _s5_weights(lam_re, lam_im, log_dt, b_re, b_im, c_re, c_im):
    hp = lax.Precision.HIGHEST
    n = S5_SUB
    dt = jnp.exp(log_dt)[..., None]
    zr, zi = lam_re * dt, lam_im * dt
    steps = jnp.arange(n + 1, dtype=F32)[:, None, None, None]
    pmag = jnp.exp(steps * zr)
    ak_re, ak_im = pmag * jnp.cos(steps * zi), pmag * jnp.sin(steps * zi)
    a_re, a_im = ak_re[1], ak_im[1]
    lam_sq = jnp.square(lam_re) + jnp.square(lam_im)
    e_re = ((a_re - 1.0) * lam_re + a_im * lam_im) / lam_sq
    e_im = (a_im * lam_re - (a_re - 1.0) * lam_im) / lam_sq
    bb_re = e_re[..., None] * b_re - e_im[..., None] * b_im
    bb_im = e_re[..., None] * b_im + e_im[..., None] * b_re
    ab_re = ak_re[..., None] * bb_re - ak_im[..., None] * bb_im
    ab_im = ak_re[..., None] * bb_im + ak_im[..., None] * bb_re
    kern = (jnp.einsum('dgqp,kdgpr->kdgqr', c_re, ab_re, precision=hp)
            - jnp.einsum('dgqp,kdgpr->kdgqr', c_im, ab_im, precision=hp))
    sig = jnp.arange(n)
    zeros = jnp.zeros_like(kern[:n])
    toe = jnp.stack([jnp.concatenate([zeros[:sp], kern[:n - sp]], axis=0) for sp in range(n)], axis=0)
    toe = toe.transpose(2, 3, 0, 5, 1, 4)
    vin_re = ab_re[n - 1 - sig].transpose(1, 2, 0, 4, 3)
    vin_im = ab_im[n - 1 - sig].transpose(1, 2, 0, 4, 3)
    ap_re, ap_im = ak_re[1:], ak_im[1:]
    w_re = c_re[None] * ap_re[:, :, :, None, :] - c_im[None] * ap_im[:, :, :, None, :]
    w_im = -(c_re[None] * ap_im[:, :, :, None, :] + c_im[None] * ap_re[:, :, :, None, :])
    w_re = w_re.transpose(1, 2, 4, 0, 3)
    w_im = w_im.transpose(1, 2, 4, 0, 3)

    def orient(a, axes):
        return jnp.stack([a[0], jnp.flip(a[1], axis=tuple(x - 1 for x in axes))], axis=0)

    toe = orient(toe, (2, 4))
    vin = orient(jnp.stack([vin_re, vin_im], axis=4), (2,))
    wout = orient(jnp.stack([w_re, w_im], axis=2), (4,))

    no = lam_re.shape[1] // S5_OCT
    ns = S5_OCT * S5_STATE

    def compact(a):
        a = a.reshape((2, no, S5_OCT) + a.shape[2:4] + (LANES,))
        return a.transpose(0, 1, 3, 2, 4, 5).reshape(2, no, -1, LANES).astype(BF16)

    t_c = compact(toe.reshape(toe.shape[:4] + (LANES,)))
    v_c = compact(vin.reshape(vin.shape[:4] + (LANES,)))
    w_c = compact(wout.reshape(wout.shape[:4] + (LANES,)))
    a_re = ak_re[n].reshape(2, no, 1, ns)
    a_im = ak_im[n].reshape(2, no, 1, ns)
    return t_c, v_c, w_c, a_re, a_im


def _s5_expand(c_ref, o_ref, row_blk, col_blk):
    n_rows, n_cols = o_ref.shape[1:]
    ent = lax.broadcasted_iota(jnp.int32, (LANES, n_cols), 0)
    col = lax.broadcasted_iota(jnp.int32, (LANES, n_cols), 1)
    spread = (ent == (col // (S5_OCT * col_blk)) * col_blk + col % col_blk).astype(BF16)
    rg = (lax.broadcasted_iota(jnp.int32, (n_rows, n_cols), 0) // row_blk) % S5_OCT
    cg = (lax.broadcasted_iota(jnp.int32, (n_rows, n_cols), 1) // col_blk) % S5_OCT
    for d in range(2):
        o_ref[d] = jnp.where(rg == cg, _dot(c_ref[d], spread), 0.0).astype(BF16)


def _s5_kernel(uf_ref, ub_ref, tc_ref, vc_ref, wc_ref, are_ref, aim_ref, yf_ref, yb_ref,
               t_ref, v_ref, w_ref, in_ref, xp_ref, st_ref, *, bsz, jt):
    ns = S5_OCT * S5_STATE
    rows = bsz * jt

    @pl.when(pl.program_id(1) == 0)
    def _init():
        st_ref[...] = jnp.zeros_like(st_ref)
        _s5_expand(tc_ref, t_ref, S5_GROUP, S5_GROUP)
        _s5_expand(vc_ref, v_ref, S5_GROUP, S5_STATE)
        _s5_expand(wc_ref, w_ref, S5_STATE, S5_GROUP)

    ri = lax.broadcasted_iota(jnp.int32, (rows, rows), 0)
    ci = lax.broadcasted_iota(jnp.int32, (rows, rows), 1)
    perm = lambda inner: (ci == (ri % inner) * (rows // inner) + ri // inner).astype(BF16)
    p_sj = perm(jt)
    p_jb = perm(bsz)
    p_bj = perm(jt)

    for d, (u_ref, y_ref) in enumerate(((uf_ref, yf_ref), (ub_ref, yb_ref))):
        u_rows = []
        for b in range(bsz):
            z = _dot(p_sj, u_ref[b].astype(BF16)).astype(BF16)
            u_rows.append(jnp.concatenate([z[s * jt:(s + 1) * jt] for s in range(S5_SUB)], axis=1))
        u = jnp.concatenate(u_rows, axis=0)
        y_intra = _dot(u, t_ref[d])
        u_jb = _dot(p_jb, u).astype(BF16)
        in_ref[d] = _dot(u_jb, v_ref[d])
        a_re, a_im = are_ref[d], aim_ref[d]
        xr, xi = st_ref[d, 0], st_ref[d, 1]
        for jj in range(jt):
            j = jj if d == 0 else jt - 1 - jj
            rows_j = slice(j * bsz, (j + 1) * bsz)
            xp_ref[d, rows_j, :] = jnp.concatenate([xr, xi], axis=1)
            inc = in_ref[d, rows_j, :]
            xr, xi = (a_re * xr - a_im * xi + inc[:, :ns], a_re * xi + a_im * xr + inc[:, ns:])
        st_ref[d, 0] = xr
        st_ref[d, 1] = xi
        xp = _dot(p_bj, xp_ref[d].astype(BF16)).astype(BF16)
        y = y_intra + _dot(xp, w_ref[d])
        for b in range(bsz):
            for s in range(S5_SUB):
                y_ref[b, pl.ds(s, jt, stride=S5_SUB), :] = y[b * jt:(b + 1) * jt, s * LANES:(s + 1) * LANES]


def _s5(u, t_c, v_c, w_c, a_re, a_im, n_ctx):
    bsz, t, width = u.shape
    assert bsz == S5_SUB
    no = width // LANES
    nt, nct = t // TM, n_ctx // TM
    jt = TM // S5_SUB
    nq, ns = S5_SUB * LANES, S5_OCT * S5_STATE
    fwd = pl.BlockSpec((bsz, TM, LANES), lambda o, i: (0, i, o))
    bwd = pl.BlockSpec((bsz, TM, LANES), lambda o, i: (0, jnp.where(i < nct, nct - 1 - i, nt + nct - 1 - i), o))
    op = lambda r, c: pl.BlockSpec((2, None, r, c), lambda o, i: (0, o, 0, 0))
    return pl.pallas_call(
        functools.partial(_s5_kernel, bsz=bsz, jt=jt),
        grid=(no, nt),
        in_specs=[fwd, bwd, op(nq, LANES), op(nq, LANES), op(2 * ns, LANES), op(1, ns), op(1, ns)],
        out_specs=[fwd, bwd],
        out_shape=[jax.ShapeDtypeStruct((bsz, t, width), F32)] * 2,
        scratch_shapes=[pltpu.VMEM((2, nq, nq), BF16), pltpu.VMEM((2, nq, 2 * ns), BF16),
                        pltpu.VMEM((2, 2 * ns, nq), BF16),
                        pltpu.VMEM((2, bsz * jt, 2 * ns), F32), pltpu.VMEM((2, bsz * jt, 2 * ns), F32),
                        pltpu.VMEM((2, 2, bsz, ns), F32)],
        compiler_params=_cparams(("parallel", "arbitrary")),
        name="s5",
    )(u, u, t_c, v_c, w_c, a_re, a_im)


def _out_even_body(tile, tok, mod_ref, tables, consts, *, alpha):
    x_ref, na_ref, hf_ref, hb_ref, zo_ref = tok
    gn_ref, w_ref, lng_ref, lnb_ref = consts
    ml = _sigmoid(zo_ref[...]) * _head_norm(hf_ref[...] + hb_ref[...], gn_ref[...], ML_HEADS, ML_HEAD_DIM)
    y = _dot(na_ref[...].astype(BF16), w_ref[0:HALF, :]) + _dot(ml.astype(BF16), w_ref[HALF:, :])
    r = alpha * x_ref[...] + _mod_row(mod_ref, 1, 2) * y
    return [_layer_norm(r, lng_ref[...], lnb_ref[...])]


def _out_odd_body(tile, tok, mod_ref, tables, consts, *, alpha):
    x_ref, ysf_ref, ysb_ref, u_ref, rf_ref, rb_ref, gr_ref = tok
    dsk_ref, gw_ref, gb_ref, gn_ref, w_ref, lng_ref, lnb_ref = consts
    s = _gelu_tanh(ysf_ref[...] + ysb_ref[...] + dsk_ref[...] * u_ref[...])
    s = s * _sigmoid(_dot(s.astype(BF16), gw_ref[...]) + gb_ref[...])
    ret = _silu(gr_ref[...]) * _head_norm(rf_ref[...] + rb_ref[...], gn_ref[...], RET_HEADS, RET_HEAD_DIM)
    y = _dot(s.astype(BF16), w_ref[0:HALF, :]) + _dot(ret.astype(BF16), w_ref[HALF:, :])
    r = alpha * x_ref[...] + _mod_row(mod_ref, 1, 2) * y
    return [_layer_norm(r, lng_ref[...], lnb_ref[...])]


def _out_call(body, name, xa, mod, tok, consts, alpha, n_ctx_tiles):
    d = xa.shape[-1]
    return _token_call(functools.partial(body, alpha=alpha), name, [(xa, 's')] + [(a, 's') for a in tok], mod,
                       [], consts, [d], [F32], xa.shape[1] // TM, n_ctx_tiles)[0]


def _rope_tables(n_ctx, seq):
    nf = RET_HEAD_DIM // 4
    freqs = ROPE_BASE ** (-jnp.arange(nf, dtype=F32) / nf)
    tok = jnp.arange(seq)
    ang_r = (tok // GRID_W).astype(F32)[:, None] * freqs
    ang_c = (tok % GRID_W).astype(F32)[:, None] * freqs
    cos = jnp.concatenate([jnp.cos(ang_r)] * 2 + [jnp.cos(ang_c)] * 2, axis=-1)
    sin = jnp.concatenate([-jnp.sin(ang_r), jnp.sin(ang_r), -jnp.sin(ang_c), jnp.sin(ang_c)], axis=-1)
    cos = jnp.concatenate([jnp.ones((n_ctx, LANES), F32), cos], axis=0)
    sin = jnp.concatenate([jnp.zeros((n_ctx, LANES), F32), sin], axis=0)
    return cos, sin


def _even_layer_mixer(xa, mod, w_in, w_out, rpb, conv_w, conv_b, wq, wk, i_bias, f_bias, gn_w,
                      lng, lnb, alpha, n_ctx):
    d = xa.shape[-1]
    n_ctx_tiles = n_ctx // TM
    ng = 2 * ML_HEADS
    main = 6 * HALF
    w_gates = jnp.zeros((d, 2 * LANES), F32)
    w_gates = w_gates.at[:, 0:ng].set(w_in[:, main:main + ng]).at[:, LANES:LANES + ng].set(w_in[:, main + ng:])
    w_all = jnp.concatenate([w_in[:, :main], w_gates], axis=1).astype(BF16)
    gate_bias = jnp.zeros((1, 2 * LANES), F32)
    gate_bias = gate_bias.at[0, 0:ng].set(i_bias.reshape(ng)).at[0, LANES:LANES + ng].set(f_bias.reshape(ng))
    outs = [(0, HALF, NA_HEAD_DIM ** -0.5, False), (HALF, HALF, 1.0, False), (2 * HALF, HALF, 1.0, False),
            (4 * HALF, HALF, 1.0, False), (5 * HALF, HALF, 1.0, False), (main, 2 * LANES, 1.0, False)]
    dts = [BF16, BF16, BF16, BF16, F32, F32, BF16, BF16]
    conv = (3 * HALF, conv_w, conv_b.reshape(1, HALF), wq.astype(BF16), wk.astype(BF16))
    q_na, k_na, v_na, zv, zo, gates, q_ml, k_ml = _inproj(xa, mod, w_all, outs, dts, n_ctx_tiles, conv=conv)
    na = _na(q_na, k_na, v_na, rpb, n_ctx)
    hf, hb = _mlscan(q_ml, k_ml, zv, gates, gate_bias, n_ctx)
    return _out_call(
        _out_even_body, "out_even", xa, mod, [na, hf, hb, zo],
        [gn_w.reshape(1, HALF), w_out.astype(BF16), lng.reshape(1, d), lnb.reshape(1, d)], alpha, n_ctx_tiles)


def _odd_layer_mixer(xa, mod, w_in, w_out, s5_params, d_skip, glu_w, glu_b, decay_logit, gn_w,
                     lng, lnb, alpha, n_ctx, rope_tabs):
    d = xa.shape[-1]
    n_ctx_tiles = n_ctx // TM
    outs = [(0, HALF, 1.0, False), (HALF, HALF, 1.0, True), (2 * HALF, HALF, RET_HEAD_DIM ** -0.5, True),
            (3 * HALF, HALF, 1.0, False), (4 * HALF, HALF, 1.0, False)]
    dts = [F32, BF16, BF16, BF16, F32]
    u, q_r, k_r, v_r, g_r = _inproj(xa, mod, w_in.astype(BF16), outs, dts, n_ctx_tiles, rope_tabs)
    ysf, ysb = _s5(u, *_s5_weights(*s5_params), n_ctx)
    rf, rb = _retscan(q_r, k_r, v_r, decay_logit, n_ctx)
    return _out_call(
        _out_odd_body, "out_odd", xa, mod, [ysf, ysb, u, rf, rb, g_r],
        [d_skip.reshape(1, HALF), glu_w.astype(BF16), glu_b.reshape(1, HALF), gn_w.reshape(1, HALF),
         w_out.astype(BF16), lng.reshape(1, d), lnb.reshape(1, d)], alpha, n_ctx_tiles)


def kernel(x, c, ctx, c_ctx, ada_w, ada_b, ffn_w_gate, ffn_w_up, ffn_w_down, ln_g, ln_b, ev_w_in, ev_w_out, na_rpb, ml_conv_w, ml_conv_b, ml_wq, ml_wk, ml_i_bias, ml_f_bias, ml_gn_w, od_w_in, od_w_out, s5_lam_re, s5_lam_im, s5_log_dt, s5_b_re, s5_b_im, s5_c_re, s5_c_im, s5_d, s5_glu_w, s5_glu_b, ret_decay_logit, ret_gn_w):
    bsz, seq, d = x.shape
    n_ctx = ctx.shape[1]
    depth = ada_w.shape[0]
    assert n_ctx % TM == 0 and seq % TM == 0 and seq % GRID_W == 0 and bsz == SUBLANES
    n_ctx_tiles = n_ctx // TM
    alpha = (2.0 * depth) ** 0.25

    rows = bsz + SUBLANES
    c_rows = jnp.zeros((rows, d), F32).at[:bsz].set(c).at[bsz].set(c_ctx)
    mod_all = _mod_table(c_rows, ada_w, ada_b)

    xa = x
    rope_tabs = _rope_tables(n_ctx, seq)
    for l in range(depth):
        e = l // 2
        mod = mod_all[l]
        ffn = lambda j, k, h, **kw: _ffn(h, mod, j, ffn_w_gate[l, k].astype(BF16), ffn_w_up[l, k].astype(BF16),
                                         ffn_w_down[l, k].astype(BF16), ln_g[l, j], ln_b[l, j], alpha,
                                         n_ctx_tiles, **kw)
        xa = ffn(0, 0, xa, ctx=ctx) if l == 0 else ffn(0, 0, xa)
        if l % 2 == 0:
            xa = _even_layer_mixer(xa, mod, ev_w_in[e], ev_w_out[e], na_rpb[e], ml_conv_w[e], ml_conv_b[e],
                                   ml_wq[e], ml_wk[e], ml_i_bias[e], ml_f_bias[e], ml_gn_w[e],
                                   ln_g[l, 1], ln_b[l, 1], alpha, n_ctx)
        else:
            s5_params = (s5_lam_re[e], s5_lam_im[e], s5_log_dt[e], s5_b_re[e], s5_b_im[e], s5_c_re[e], s5_c_im[e])
            xa = _odd_layer_mixer(xa, mod, od_w_in[e], od_w_out[e], s5_params, s5_d[e], s5_glu_w[e], s5_glu_b[e],
                                  ret_decay_logit[e], ret_gn_w[e], ln_g[l, 1], ln_b[l, 1], alpha, n_ctx, rope_tabs)
        xa = ffn(2, 1, xa, latent_only=(l == depth - 1))
    return xa
```

```python
import functools
import math

import jax
import jax.numpy as jnp
from jax import lax
from jax.experimental import pallas as pl
from jax.experimental.pallas import tpu as pltpu

F32 = jnp.float32
BF16 = jnp.bfloat16

GRID_W = 64
LN_EPS = 1e-5
N_MOD = 9
NA_HEADS, NA_HEAD_DIM, NA_WIN_H, NA_WIN_W = 8, 64, 8, 16
ML_HEADS, ML_HEAD_DIM, ML_CONV = 4, 128, 5
S5_GROUP, S5_GROUPS, S5_STATE = 16, 32, 64
RET_HEADS, RET_HEAD_DIM = 4, 128
ROPE_BASE = 10000.0
HALF = 512

LANES = 128
SUBLANES = 8
VMEM_LIMIT = 56 * 1024 * 1024

TM = 256
CHUNK = 256
ML_NB, RET_NB = 1, 2
S5_SUB = 8
S5_OCT = LANES // S5_GROUP
NEG = -1e30


def _cparams(sem):
    return pltpu.CompilerParams(dimension_semantics=sem, vmem_limit_bytes=VMEM_LIMIT)


def _const_spec(shape):
    nd = len(shape)
    return pl.BlockSpec(shape, lambda *_: (0,) * nd, pipeline_mode=pl.Buffered(1))


def _dot(a, b):
    return jnp.dot(a, b, preferred_element_type=F32)


def _dot_nt(a, b):
    return lax.dot_general(a, b, (((1,), (1,)), ((), ())), preferred_element_type=F32)


def _dot_tn(a, b):
    return lax.dot_general(a, b, (((0,), (0,)), ((), ())), preferred_element_type=F32)


def _sigmoid(x):
    return 1.0 / (1.0 + jnp.exp(-x))


def _silu(x):
    return x * _sigmoid(x)


def _log_sigmoid(x):
    return jnp.minimum(x, 0.0) - jnp.log1p(jnp.exp(-jnp.abs(x)))


def _gelu_tanh(x):
    return 0.5 * x * (1.0 + jnp.tanh(math.sqrt(2.0 / math.pi) * (x + 0.044715 * (x * x * x))))


def _layer_norm(r, g, b):
    mu = jnp.mean(r, axis=-1, keepdims=True)
    c = r - mu
    var = jnp.mean(c * c, axis=-1, keepdims=True)
    return c * lax.rsqrt(var + LN_EPS) * g + b


def _head_norm(h, w, n_heads, head_dim):
    parts = []
    for k in range(n_heads):
        hk = h[:, k * head_dim:(k + 1) * head_dim]
        mu = jnp.mean(hk, axis=-1, keepdims=True)
        c = hk - mu
        var = jnp.mean(c * c, axis=-1, keepdims=True)
        parts.append(c * lax.rsqrt(var + LN_EPS))
    return jnp.concatenate(parts, axis=-1) * w


def _mod_row(mod_ref, j, k):
    i = 3 * j + k
    return mod_ref[i:i + 1, :]


def _mod_kernel(c_ref, w_ref, b_ref, o_ref):
    c = c_ref[...]
    s = _silu(c).astype(BF16)
    o_ref[...] = _dot(s, w_ref[...].astype(BF16)) + b_ref[...]


def _mod_table(c_rows, ada_w, ada_b):
    depth, d, n = ada_w.shape
    rows = c_rows.shape[0]
    tn = 1024
    out = pl.pallas_call(
        _mod_kernel,
        grid=(depth, n // tn),
        in_specs=[pl.BlockSpec((rows, d), lambda l, j: (0, 0)),
                  pl.BlockSpec((None, d, tn), lambda l, j: (l, 0, j)),
                  pl.BlockSpec((None, 1, tn), lambda l, j: (l, 0, j))],
        out_specs=pl.BlockSpec((None, rows, tn), lambda l, j: (l, 0, j)),
        out_shape=jax.ShapeDtypeStruct((depth, rows, n), F32),
        compiler_params=_cparams(("parallel", "parallel")),
        name="mod_table",
    )(c_rows, ada_w, ada_b.reshape(depth, 1, n))
    return out.reshape(depth, rows, N_MOD, d)


PAIR = 4


def _tok_spec(width, tm=TM):
    return pl.BlockSpec((None, tm, width), lambda b, t: (b, t, 0))


def _tile_index(h, nt, nct, bsz, latent_only):
    if latent_only:
        per = (nt - nct) // PAIR

        def index(i):
            b = i // per
            return b, nct + PAIR * (i % per) + h, b
    else:
        def index(i):
            g = PAIR * i + h
            b, t = g // nt, g % nt
            return b, t, jnp.where(t < nct, bsz, b)
    return index


def _token_call(body, name, tok, mod, tables, consts, out_widths, out_dtypes, nt, nct, latent_only=False):
    bsz, d = tok[0][0].shape[0], mod.shape[-1]
    n_tiles = bsz * ((nt - nct) if latent_only else nt)
    assert n_tiles % PAIR == 0 and (not latent_only or (nt - nct) % PAIR == 0)
    index = [_tile_index(h, nt, nct, bsz, latent_only) for h in range(PAIR)]
    per = TM // SUBLANES
    last = nt * per - 1
    tile_of = {'s': lambda t: t, 'c': lambda t: jnp.minimum(t, nct - 1), 'l': lambda t: jnp.maximum(t - nct, 0),
               'p': lambda t: jnp.maximum(t * per - 1, 0), 'n': lambda t: jnp.minimum((t + 1) * per, last)}
    in_specs, args = [], []
    for h in range(PAIR):
        for arr, kind in tok:
            in_specs.append(pl.BlockSpec(
                (None, SUBLANES if kind in 'pn' else TM, arr.shape[-1]),
                lambda i, f=index[h], k=tile_of[kind]: (f(i)[0], k(f(i)[1]), 0)))
            args.append(arr)
        in_specs.append(pl.BlockSpec((None, N_MOD, d), lambda i, f=index[h]: (f(i)[2], 0, 0)))
        args.append(mod)
        for tab in tables:
            in_specs.append(pl.BlockSpec((TM, tab.shape[-1]), lambda i, f=index[h]: (f(i)[1], 0)))
            args.append(tab)
    in_specs += [_const_spec(c.shape) for c in consts]
    args += list(consts)
    n_half = len(tok) + 1 + len(tables)

    def kern(*refs):
        const_refs = refs[PAIR * n_half:PAIR * n_half + len(consts)]
        out_refs = refs[PAIR * n_half + len(consts):]
        for h in range(PAIR):
            hr = refs[h * n_half:(h + 1) * n_half]
            tile = index[h](pl.program_id(0))[1]
            res = body(tile, hr[:len(tok)], hr[len(tok)], hr[len(tok) + 1:], const_refs)
            for o_ref, r in zip(out_refs, res):
                o_ref[h * TM:(h + 1) * TM, :] = r.astype(o_ref.dtype)

    outs = pl.pallas_call(
        kern,
        grid=(n_tiles // PAIR,),
        in_specs=in_specs,
        out_specs=[pl.BlockSpec((PAIR * TM, w), lambda i: (i, 0)) for w in out_widths],
        out_shape=[jax.ShapeDtypeStruct((n_tiles * TM, w), dt) for w, dt in zip(out_widths, out_dtypes)],
        compiler_params=_cparams(("parallel",)),
        name=name,
    )(*args)
    return [o.reshape(bsz, -1, o.shape[-1]) for o in outs]


def _ffn_body(tile, tok, mod_ref, tables, consts, *, j, alpha, nct):
    wg_ref, wu_ref, wd_ref, lng_ref, lnb_ref = consts
    if len(tok) == 2:
        x = jnp.where(tile < nct, tok[0][...], tok[1][...])
    else:
        x = tok[0][...]
    h = (x * (1.0 + _mod_row(mod_ref, j, 1)) + _mod_row(mod_ref, j, 0)).astype(BF16)
    g = _dot(h, wg_ref[...])
    u = _dot(h, wu_ref[...])
    a = (_silu(g) * u).astype(BF16)
    y = _dot(a, wd_ref[...])
    r = alpha * x + (0.5 * _mod_row(mod_ref, j, 2)) * y
    return [_layer_norm(r, lng_ref[...], lnb_ref[...])]


def _ffn(xa, mod, j, wg, wu, wd, lng, lnb, alpha, n_ctx_tiles, ctx=None, latent_only=False):
    d = xa.shape[-1]
    nct = n_ctx_tiles
    tok = [(xa, 's')] if ctx is None else [(ctx, 'c'), (xa, 'l')]
    nt = xa.shape[1] // TM + (0 if ctx is None else nct)
    return _token_call(
        functools.partial(_ffn_body, j=j, alpha=alpha, nct=nct), "ffn", tok, mod, [],
        [wg, wu, wd, lng.reshape(1, d), lnb.reshape(1, d)], [d], [F32], nt, nct, latent_only)[0]


def _rope(z, cos, sin):
    lane = lax.broadcasted_iota(jnp.int32, (1, LANES), 1)
    first = (lane % 64) < 32
    parts = []
    for k in range(z.shape[1] // LANES):
        zk = z[:, k * LANES:(k + 1) * LANES]
        partner = jnp.where(first, pltpu.roll(zk, LANES - 32, axis=1), pltpu.roll(zk, 32, axis=1))
        parts.append(zk * cos + partner * sin)
    return jnp.concatenate(parts, axis=-1)


def _inproj_body(tile, tok, mod_ref, tables, consts, *, outs, conv):
    w_ref = consts[0]
    modulate = lambda x: (x * (1.0 + _mod_row(mod_ref, 1, 1)) + _mod_row(mod_ref, 1, 0)).astype(BF16)
    h = modulate(tok[0][...])
    res = []
    for off, width, scale, rope in outs:
        z = _dot(h, w_ref[:, off:off + width])
        if rope:
            z = _rope(z, tables[0][...], tables[1][...])
        if scale != 1.0:
            z = z * scale
        res.append(z)
    if conv is not None:
        off, nct, nt = conv
        cw_ref, cb_ref, wq_ref, wk_ref = consts[1:]
        wx = w_ref[:, off:off + HALF]
        has_prev = jnp.logical_and(tile != 0, tile != nct)
        has_next = jnp.logical_and(tile != nct - 1, tile != nt - 1)
        prev = jnp.where(has_prev, _dot(modulate(tok[1][...]), wx), 0.0)
        nxt = jnp.where(has_next, _dot(modulate(tok[2][...]), wx), 0.0)
        ext = jnp.concatenate([prev, _dot(h, wx), nxt], axis=0)
        acc = cb_ref[...]
        for j in range(ML_CONV):
            o = SUBLANES + j - ML_CONV // 2
            acc = acc + cw_ref[j:j + 1, :] * ext[o:o + TM, :]
        xc = _silu(acc).astype(BF16)
        heads = [slice(k * ML_HEAD_DIM, (k + 1) * ML_HEAD_DIM) for k in range(ML_HEADS)]
        res.append(jnp.concatenate([_dot(xc[:, hs], wq_ref[k]) for k, hs in enumerate(heads)], axis=-1))
        res.append(jnp.concatenate([_dot(xc[:, hs], wk_ref[k]) for k, hs in enumerate(heads)], axis=-1)
                   * ML_HEAD_DIM ** -0.5)
    return res


def _inproj(xa, mod, w, outs, dtypes, n_ctx_tiles, rope_tabs=(), conv=None):
    nt = xa.shape[1] // TM
    tok, consts, widths, conv_static = [(xa, 's')], [w], [o[1] for o in outs], None
    if conv is not None:
        tok += [(xa, 'p'), (xa, 'n')]
        consts += list(conv[1:])
        widths += [HALF, HALF]
        conv_static = (conv[0], n_ctx_tiles, nt)
    return _token_call(
        functools.partial(_inproj_body, outs=tuple(outs), conv=conv_static), "inproj", tok, mod, list(rope_tabs),
        consts, widths, dtypes, nt, n_ctx_tiles)


NA_ROWS_PER_TILE = TM // GRID_W
NA_SLAB_ROWS = NA_WIN_H + NA_ROWS_PER_TILE


def _softmax_pv(s_list, v_list):
    m = s_list[0].max(axis=-1, keepdims=True)
    for s in s_list[1:]:
        m = jnp.maximum(m, s.max(axis=-1, keepdims=True))
    acc, den = None, None
    for s, v in zip(s_list, v_list):
        p = jnp.exp(s - m)
        l = p.sum(axis=-1, keepdims=True)
        o = _dot(p.astype(BF16), v)
        acc = o if acc is None else acc + o
        den = l if den is None else den + l
    return acc / den


def _na_kernel(q_ref, k_ref, v_ref, bias_ref, o_ref, *, n_ctx, rows, n_ctx_tiles):
    t = pl.program_id(1)
    n_loc = NA_SLAB_ROWS * GRID_W
    lane = lax.broadcasted_iota(jnp.int32, (1, LANES), 1)
    low = lane < NA_HEAD_DIM
    zero = jnp.zeros((), BF16)

    def pair(p, keys):
        ls = slice(p * LANES, (p + 1) * LANES)
        q2 = q_ref[:, ls]
        outs = []
        for e in range(2):
            qm = jnp.where(low if e == 0 else jnp.logical_not(low), q2, zero)
            s_list, v_list = [], []
            for kk, vv, bias in keys(ls, 2 * p + e):
                s = _dot_nt(qm, kk)
                s_list.append(s if bias is None else s + bias)
                v_list.append(vv)
            outs.append(_softmax_pv(s_list, v_list))
        o_ref[:, ls] = jnp.where(low, outs[0], outs[1]).astype(o_ref.dtype)

    @pl.when(t < n_ctx_tiles)
    def _ctx():
        for p in range(NA_HEADS // 2):
            pair(p, lambda ls, h: [(k_ref[0:n_ctx, ls], v_ref[0:n_ctx, ls], None)])

    @pl.when(t >= n_ctx_tiles)
    def _latent():
        r0 = (t - n_ctx_tiles) * NA_ROWS_PER_TILE
        rs0 = jnp.clip(r0 - NA_WIN_H // 2, 0, rows - NA_SLAB_ROWS)
        var = (r0 - rs0) // NA_ROWS_PER_TILE
        start = pl.multiple_of(n_ctx + rs0 * GRID_W, GRID_W)
        for p in range(NA_HEADS // 2):
            pair(p, lambda ls, h: [(k_ref[pl.ds(start, n_loc), ls], v_ref[pl.ds(start, n_loc), ls], bias_ref[var, h]),
                                   (k_ref[0:n_ctx, ls], v_ref[0:n_ctx, ls], None)])


def _na_bias_table(rpb, rows):
    rpt, slab = NA_ROWS_PER_TILE, NA_SLAB_ROWS
    cols = jnp.arange(GRID_W)
    cs = jnp.clip(cols - NA_WIN_W // 2, 0, GRID_W - NA_WIN_W)
    kc = jnp.arange(GRID_W)
    in_win = (kc[None, :] >= cs[:, None]) & (kc[None, :] < cs[:, None] + NA_WIN_W)
    col_off = kc[None, :] - cols[:, None] + NA_WIN_W - 1
    diff = (jnp.arange(3) * rpt)[:, None, None]
    i = jnp.arange(rpt)[None, :, None]
    a = jnp.arange(slab)[None, None, :]
    w0 = jnp.clip(diff + i - NA_WIN_H // 2, 0, slab - NA_WIN_H)
    row_ok = (a >= w0) & (a < w0 + NA_WIN_H)
    row_off = a - diff - i + NA_WIN_H - 1
    row_sel = ((row_off[..., None] == jnp.arange(2 * NA_WIN_H - 1)) & row_ok[..., None]).astype(F32)
    col_sel = (col_off[:, :, None] == jnp.arange(2 * NA_WIN_W - 1)).astype(F32)
    tab = jnp.einsum('hrc,viar,qkc->vhiqak', rpb.astype(F32), row_sel, col_sel, precision=lax.Precision.HIGHEST)
    ok = row_ok[:, None, :, None, :, None] & in_win[None, None, None, :, None, :]
    tab = jnp.where(ok, tab, NEG)
    return tab.reshape(3, NA_HEADS, rpt * GRID_W, slab * GRID_W)


def _na(q, k, v, rpb, n_ctx):
    bsz, t, w = q.shape
    rows = (t - n_ctx) // GRID_W
    assert rows >= NA_SLAB_ROWS and rows % NA_ROWS_PER_TILE == 0
    bias = _na_bias_table(rpb, rows)
    return pl.pallas_call(
        functools.partial(_na_kernel, n_ctx=n_ctx, rows=rows, n_ctx_tiles=n_ctx // TM),
        grid=(bsz, t // TM),
        in_specs=[_tok_spec(w),
                  pl.BlockSpec((None, t, w), lambda b, i: (b, 0, 0)),
                  pl.BlockSpec((None, t, w), lambda b, i: (b, 0, 0)),
                  _const_spec(bias.shape)],
        out_specs=_tok_spec(w),
        out_shape=jax.ShapeDtypeStruct((bsz, t, w), BF16),
        compiler_params=_cparams(("parallel", "arbitrary")),
        name="natten",
    )(q, k, v, bias)


def _scan_specs(width, n_chunks, n_ctx_chunks, nb):
    fwd = pl.BlockSpec((nb, CHUNK, width), lambda b, i: (b, i, 0))
    bwd = pl.BlockSpec(
        (nb, CHUNK, width),
        lambda b, i: (b, jnp.where(i < n_ctx_chunks, n_ctx_chunks - 1 - i, n_chunks + n_ctx_chunks - 1 - i), 0))
    return fwd, bwd


def _split3(x):
    hi = x.astype(BF16)
    r1 = x - hi.astype(F32)
    mid = r1.astype(BF16)
    lo = (r1 - mid.astype(F32)).astype(BF16)
    return hi, mid, lo


def _mlscan_kernel(qf_ref, kf_ref, vf_ref, gf_ref, qb_ref, kb_ref, vb_ref, gb_ref, bias_ref,
                   of_ref, ob_ref, st_ref, m_ref):
    i = pl.program_id(1)
    dk = ML_HEAD_DIM

    @pl.when(i == 0)
    def _init():
        st_ref[...] = jnp.zeros_like(st_ref)
        m_ref[...] = jnp.zeros_like(m_ref)

    tt = lax.broadcasted_iota(jnp.int32, (CHUNK, CHUNK), 0)
    ss = lax.broadcasted_iota(jnp.int32, (CHUNK, CHUNK), 1)
    lane = lax.broadcasted_iota(jnp.int32, (CHUNK, LANES), 1)
    row = lax.broadcasted_iota(jnp.int32, (CHUNK, LANES), 0)
    ones_col = (lane == 0).astype(BF16)

    def running_max(x, reverse):
        sh = 1
        while sh < CHUNK:
            if reverse:
                shifted = jnp.where(row < CHUNK - sh, pltpu.roll(x, CHUNK - sh, axis=0), -jnp.inf)
            else:
                shifted = jnp.where(row >= sh, pltpu.roll(x, sh, axis=0), -jnp.inf)
            x = jnp.maximum(x, shifted)
            sh *= 2
        return x

    n_chain = 2 * ML_HEADS
    for bi, d in [(bi, d) for bi in range(ML_NB) for d in range(2)]:
        q_ref, k_ref, v_ref, g_ref, o_ref =((qf_ref, kf_ref, vf_ref, gf_ref, of_ref),
                                             (qb_ref, kb_ref, vb_ref, gb_ref, ob_ref))[d]
        causal = (ss <= tt) if d == 0 else (ss >= tt)
        tri = causal.astype(BF16)
        gates = g_ref[bi] + bias_ref[...]
        hi, mid, lo = _split3(_log_sigmoid(gates[:, LANES:]))
        bc = _dot(tri, hi) + _dot(tri, mid) + _dot(tri, lo)
        e = gates[:, :LANES] - bc
        e_t = e.T
        m_row = bi * 2 + d
        m_prev = m_ref[m_row:m_row + 1, :]
        g = jnp.maximum(running_max(e, reverse=d == 1), m_prev)
        w_inter = jnp.exp(m_prev - g)
        exp_neg_mt = jnp.exp(-(bc + g))
        last = CHUNK - 1 if d == 0 else 0
        b_last = bc[last:last + 1, :]
        w_log = b_last + e
        m_new = jnp.maximum(b_last + m_prev, w_log.max(axis=0, keepdims=True))
        w_in = jnp.exp(w_log - m_new)
        decay = jnp.exp(b_last + m_prev - m_new)
        m_ref[m_row:m_row + 1, :] = m_new
        for h in range(ML_HEADS):
            c = d * ML_HEADS + h
            sc = bi * n_chain + c
            hs = slice(h * dk, (h + 1) * dk)
            q, k = q_ref[bi, :, hs], k_ref[bi, :, hs]
            v_aug = jnp.concatenate([v_ref[bi, :, hs], ones_col], axis=-1)
            p = jnp.exp(jnp.where(causal, e_t[c:c + 1, :] - g[:, c:c + 1], -jnp.inf))
            s = _dot_nt(q, k) * p
            state = st_ref[sc]
            res = _dot(s.astype(BF16), v_aug) + w_inter[:, c:c + 1] * _dot(q, state.astype(BF16))
            inv = 1.0 / jnp.maximum(jnp.abs(res[:, dk:dk + 1]), exp_neg_mt[:, c:c + 1])
            o_ref[bi, :, hs] = res[:, :dk] * inv
            st_ref[sc] = decay[:, c:c + 1] * state + _dot_tn(k, (v_aug * w_in[:, c:c + 1]).astype(BF16))


def _mlscan(q, k, v, gates, gate_bias, n_ctx):
    bsz, t, w = q.shape
    nc, ncc = t // CHUNK, n_ctx // CHUNK
    f, bk = _scan_specs(w, nc, ncc, ML_NB)
    gf, gb = _scan_specs(2 * LANES, nc, ncc, ML_NB)
    n_chain = 2 * ML_HEADS
    assert bsz % ML_NB == 0 and 2 * ML_NB <= SUBLANES
    return pl.pallas_call(
        _mlscan_kernel,
        grid=(bsz // ML_NB, nc),
        in_specs=[f, f, f, gf, bk, bk, bk, gb, _const_spec((1, 2 * LANES))],
        out_specs=[f, bk],
        out_shape=[jax.ShapeDtypeStruct((bsz, t, w), F32)] * 2,
        scratch_shapes=[pltpu.VMEM((ML_NB * n_chain, ML_HEAD_DIM, 2 * ML_HEAD_DIM), F32),
                        pltpu.VMEM((SUBLANES, LANES), F32)],
        compiler_params=_cparams(("parallel", "arbitrary")),
        name="mlstm_scan",
    )(q, k, v, gates, q, k, v, gates, gate_bias)


def _retscan_kernel(qf_ref, kf_ref, vf_ref, qb_ref, kb_ref, vb_ref, logit_ref, of_ref, ob_ref,
                    st_ref, dmat_ref, dec_ref):
    i = pl.program_id(1)
    dk = RET_HEAD_DIM
    n_chain = 2 * RET_HEADS

    @pl.when(i == 0)
    def _init():
        st_ref[...] = jnp.zeros_like(st_ref)
        tt = lax.broadcasted_iota(jnp.int32, (CHUNK, CHUNK), 0)
        ss = lax.broadcasted_iota(jnp.int32, (CHUNK, CHUNK), 1)
        pos = lax.broadcasted_iota(jnp.int32, (CHUNK, LANES), 0).astype(F32)
        log_g = _log_sigmoid(logit_ref[...])
        for d in range(2):
            dist = (tt - ss) if d == 0 else (ss - tt)
            distf = jnp.maximum(dist, 0).astype(F32)
            step = pos if d == 0 else (CHUNK - 1.0) - pos
            for h in range(RET_HEADS):
                c = d * RET_HEADS + h
                dmat_ref[c] = jnp.where(dist >= 0, jnp.exp(distf * log_g[c:c + 1, 0:1]), 0.0)
                lg = log_g[c:c + 1, :]
                dec_ref[c, 0] = jnp.exp((step + 1.0) * lg)
                dec_ref[c, 1] = jnp.exp((CHUNK - 1.0 - step) * lg)
                dec_ref[c, 2] = jnp.broadcast_to(jnp.exp(CHUNK * lg), (CHUNK, LANES))

    for bi, d in [(bi, d) for bi in range(RET_NB) for d in range(2)]:
        q_ref, k_ref, v_ref, o_ref =((qf_ref, kf_ref, vf_ref, of_ref), (qb_ref, kb_ref, vb_ref, ob_ref))[d]
        for h in range(RET_HEADS):
            c = d * RET_HEADS + h
            sc = bi * n_chain + c
            hs = slice(h * dk, (h + 1) * dk)
            q, k, v = q_ref[bi, :, hs], k_ref[bi, :, hs], v_ref[bi, :, hs]
            state = st_ref[sc]
            inner = _dot((_dot_nt(q, k) * dmat_ref[c]).astype(BF16), v)
            o_ref[bi, :, hs] = inner + dec_ref[c, 0] * _dot(q, state.astype(BF16))
            st_ref[sc] = dec_ref[c, 2, 0:dk, :] * state + _dot_tn(k, (v.astype(F32) * dec_ref[c, 1]).astype(BF16))


def _retscan(q, k, v, decay_logit, n_ctx):
    bsz, t, w = q.shape
    nc, ncc = t // CHUNK, n_ctx // CHUNK
    f, bk = _scan_specs(w, nc, ncc, RET_NB)
    logit = jnp.broadcast_to(decay_logit.astype(F32).reshape(2 * RET_HEADS, 1), (2 * RET_HEADS, LANES))
    assert bsz % RET_NB == 0 and RET_HEAD_DIM == LANES
    n_chain = 2 * RET_HEADS
    return pl.pallas_call(
        _retscan_kernel,
        grid=(bsz // RET_NB, nc),
        in_specs=[f, f, f, bk, bk, bk, _const_spec((n_chain, LANES))],
        out_specs=[f, bk],
        out_shape=[jax.ShapeDtypeStruct((bsz, t, w), F32)] * 2,
        scratch_shapes=[pltpu.VMEM((RET_NB * n_chain, RET_HEAD_DIM, RET_HEAD_DIM), F32),
                        pltpu.VMEM((n_chain, CHUNK, CHUNK), F32),
                        pltpu.VMEM((n_chain, 3, CHUNK, LANES), F32)],
        compiler_params=_cparams(("parallel", "arbitrary")),
        name="retention_scan",
    )(q, k, v, q, k, v, logit)


def _s5_weights(lam_re, lam_im, log_dt, b_re, b_im, c_re, c_im):
    hp = lax.Precision.HIGHEST
    n = S5_SUB
    dt = jnp.exp(log_dt)[..., None]
    zr, zi = lam_re * dt, lam_im * dt
    steps = jnp.arange(n + 1, dtype=F32)[:, None, None, None]
    pmag = jnp.exp(steps * zr)
    ak_re, ak_im = pmag * jnp.cos(steps * zi), pmag * jnp.sin(steps * zi)
    a_re, a_im = ak_re[1], ak_im[1]
    lam_sq = jnp.square(lam_re) + jnp.square(lam_im)
    e_re = ((a_re - 1.0) * lam_re + a_im * lam_im) / lam_sq
    e_im = (a_im * lam_re - (a_re - 1.0) * lam_im) / lam_sq
    bb_re = e_re[..., None] * b_re - e_im[..., None] * b_im
    bb_im = e_re[..., None] * b_im + e_im[..., None] * b_re
    ab_re = ak_re[..., None] * bb_re - ak_im[..., None] * bb_im
    ab_im = ak_re[..., None] * bb_im + ak_im[..., None] * bb_re
    kern = (jnp.einsum('dgqp,kdgpr->kdgqr', c_re, ab_re, precision=hp)
            - jnp.einsum('dgqp,kdgpr->kdgqr', c_im, ab_im, precision=hp))
    sig = jnp.arange(n)
    zeros = jnp.zeros_like(kern[:n])
    toe = jnp.stack([jnp.concatenate([zeros[:sp], kern[:n - sp]], axis=0) for sp in range(n)], axis=0)
    toe = toe.transpose(2, 3, 0, 5, 1, 4)
    vin_re = ab_re[n - 1 - sig].transpose(1, 2, 0, 4, 3)
    vin_im = ab_im[n - 1 - sig].transpose(1, 2, 0, 4, 3)
    ap_re, ap_im = ak_re[1:], ak_im[1:]
    w_re = c_re[None] * ap_re[:, :, :, None, :] - c_im[None] * ap_im[:, :, :, None, :]
    w_im = -(c_re[None] * ap_im[:, :, :, None, :] + c_im[None] * ap_re[:, :, :, None, :])
    w_re = w_re.transpose(1, 2, 4, 0, 3)
    w_im = w_im.transpose(1, 2, 4, 0, 3)

    def orient(a, axes):
        return jnp.stack([a[0], jnp.flip(a[1], axis=tuple(x - 1 for x in axes))], axis=0)

    toe = orient(toe, (2, 4))
    vin = orient(jnp.stack([vin_re, vin_im], axis=4), (2,))
    wout = orient(jnp.stack([w_re, w_im], axis=2), (4,))

    no = lam_re.shape[1] // S5_OCT
    ns = S5_OCT * S5_STATE

    def compact(a):
        a = a.reshape((2, no, S5_OCT) + a.shape[2:4] + (LANES,))
        return a.transpose(0, 1, 3, 2, 4, 5).reshape(2, no, -1, LANES).astype(BF16)

    t_c = compact(toe.reshape(toe.shape[:4] + (LANES,)))
    v_c = compact(vin.reshape(vin.shape[:4] + (LANES,)))
    w_c = compact(wout.reshape(wout.shape[:4] + (LANES,)))
    a_re = ak_re[n].reshape(2, no, 1, ns)
    a_im = ak_im[n].reshape(2, no, 1, ns)
    return t_c, v_c, w_c, a_re, a_im


def _s5_expand(c_ref, o_ref, row_blk, col_blk):
    n_rows, n_cols = o_ref.shape[1:]
    ent = lax.broadcasted_iota(jnp.int32, (LANES, n_cols), 0)
    col = lax.broadcasted_iota(jnp.int32, (LANES, n_cols), 1)
    spread = (ent == (col // (S5_OCT * col_blk)) * col_blk + col % col_blk).astype(BF16)
    rg = (lax.broadcasted_iota(jnp.int32, (n_rows, n_cols), 0) // row_blk) % S5_OCT
    cg = (lax.broadcasted_iota(jnp.int32, (n_rows, n_cols), 1) // col_blk) % S5_OCT
    for d in range(2):
        o_ref[d] = jnp.where(rg == cg, _dot(c_ref[d], spread), 0.0).astype(BF16)


def _s5_kernel(uf_ref, ub_ref, tc_ref, vc_ref, wc_ref, are_ref, aim_ref, yf_ref, yb_ref,
               t_ref, v_ref, w_ref, in_ref, xp_ref, st_ref, *, bsz, jt):
    ns = S5_OCT * S5_STATE
    rows = bsz * jt

    @pl.when(pl.program_id(1) == 0)
    def _init():
        st_ref[...] = jnp.zeros_like(st_ref)
        _s5_expand(tc_ref, t_ref, S5_GROUP, S5_GROUP)
        _s5_expand(vc_ref, v_ref, S5_GROUP, S5_STATE)
        _s5_expand(wc_ref, w_ref, S5_STATE, S5_GROUP)

    ri = lax.broadcasted_iota(jnp.int32, (rows, rows), 0)
    ci = lax.broadcasted_iota(jnp.int32, (rows, rows), 1)
    perm = lambda inner: (ci == (ri % inner) * (rows // inner) + ri // inner).astype(BF16)
    p_sj = perm(jt)
    p_jb = perm(bsz)
    p_bj = perm(jt)

    for d, (u_ref, y_ref) in enumerate(((uf_ref, yf_ref), (ub_ref, yb_ref))):
        u_rows = []
        for b in range(bsz):
            z = _dot(p_sj, u_ref[b].astype(BF16)).astype(BF16)
            u_rows.append(jnp.concatenate([z[s * jt:(s + 1) * jt] for s in range(S5_SUB)], axis=1))
        u = jnp.concatenate(u_rows, axis=0)
        y_intra = _dot(u, t_ref[d])
        u_jb = _dot(p_jb, u).astype(BF16)
        in_ref[d] = _dot(u_jb, v_ref[d])
        a_re, a_im = are_ref[d], aim_ref[d]
        xr, xi = st_ref[d, 0], st_ref[d, 1]
        for jj in range(jt):
            j = jj if d == 0 else jt - 1 - jj
            rows_j = slice(j * bsz, (j + 1) * bsz)
            xp_ref[d, rows_j, :] = jnp.concatenate([xr, xi], axis=1)
            inc = in_ref[d, rows_j, :]
            xr, xi = (a_re * xr - a_im * xi + inc[:, :ns], a_re * xi + a_im * xr + inc[:, ns:])
        st_ref[d, 0] = xr
        st_ref[d, 1] = xi
        xp = _dot(p_bj, xp_ref[d].astype(BF16)).astype(BF16)
        y = y_intra + _dot(xp, w_ref[d])
        for b in range(bsz):
            for s in range(S5_SUB):
                y_ref[b, pl.ds(s, jt, stride=S5_SUB), :] = y[b * jt:(b + 1) * jt, s * LANES:(s + 1) * LANES]


def _s5(u, t_c, v_c, w_c, a_re, a_im, n_ctx):
    bsz, t, width = u.shape
    assert bsz == S5_SUB
    no = width // LANES
    nt, nct = t // TM, n_ctx // TM
    jt = TM // S5_SUB
    nq, ns = S5_SUB * LANES, S5_OCT * S5_STATE
    fwd = pl.BlockSpec((bsz, TM, LANES), lambda o, i: (0, i, o))
    bwd = pl.BlockSpec((bsz, TM, LANES), lambda o, i: (0, jnp.where(i < nct, nct - 1 - i, nt + nct - 1 - i), o))
    op = lambda r, c: pl.BlockSpec((2, None, r, c), lambda o, i: (0, o, 0, 0))
    return pl.pallas_call(
        functools.partial(_s5_kernel, bsz=bsz, jt=jt),
        grid=(no, nt),
        in_specs=[fwd, bwd, op(nq, LANES), op(nq, LANES), op(2 * ns, LANES), op(1, ns), op(1, ns)],
        out_specs=[fwd, bwd],
        out_shape=[jax.ShapeDtypeStruct((bsz, t, width), F32)] * 2,
        scratch_shapes=[pltpu.VMEM((2, nq, nq), BF16), pltpu.VMEM((2, nq, 2 * ns), BF16),
                        pltpu.VMEM((2, 2 * ns, nq), BF16),
                        pltpu.VMEM((2, bsz * jt, 2 * ns), F32), pltpu.VMEM((2, bsz * jt, 2 * ns), F32),
                        pltpu.VMEM((2, 2, bsz, ns), F32)],
        compiler_params=_cparams(("parallel", "arbitrary")),
        name="s5",
    )(u, u, t_c, v_c, w_c, a_re, a_im)


def _out_even_body(tile, tok, mod_ref, tables, consts, *, alpha):
    x_ref, na_ref, hf_ref, hb_ref, zo_ref = tok
    gn_ref, w_ref, lng_ref, lnb_ref = consts
    ml = _sigmoid(zo_ref[...]) * _head_norm(hf_ref[...] + hb_ref[...], gn_ref[...], ML_HEADS, ML_HEAD_DIM)
    y = _dot(na_ref[...].astype(BF16), w_ref[0:HALF, :]) + _dot(ml.astype(BF16), w_ref[HALF:, :])
    r = alpha * x_ref[...] + _mod_row(mod_ref, 1, 2) * y
    return [_layer_norm(r, lng_ref[...], lnb_ref[...])]


def _out_odd_body(tile, tok, mod_ref, tables, consts, *, alpha):
    x_ref, ysf_ref, ysb_ref, u_ref, rf_ref, rb_ref, gr_ref = tok
    dsk_ref, gw_ref, gb_ref, gn_ref, w_ref, lng_ref, lnb_ref = consts
    s = _gelu_tanh(ysf_ref[...] + ysb_ref[...] + dsk_ref[...] * u_ref[...])
    s = s * _sigmoid(_dot(s.astype(BF16), gw_ref[...]) + gb_ref[...])
    ret = _silu(gr_ref[...]) * _head_norm(rf_ref[...] + rb_ref[...], gn_ref[...], RET_HEADS, RET_HEAD_DIM)
    y = _dot(s.astype(BF16), w_ref[0:HALF, :]) + _dot(ret.astype(BF16), w_ref[HALF:, :])
    r = alpha * x_ref[...] + _mod_row(mod_ref, 1, 2) * y
    return [_layer_norm(r, lng_ref[...], lnb_ref[...])]


def _out_call(body, name, xa, mod, tok, consts, alpha, n_ctx_tiles):
    d = xa.shape[-1]
    return _token_call(functools.partial(body, alpha=alpha), name, [(xa, 's')] + [(a, 's') for a in tok], mod,
                       [], consts, [d], [F32], xa.shape[1] // TM, n_ctx_tiles)[0]


def _rope_tables(n_ctx, seq):
    nf = RET_HEAD_DIM // 4
    freqs = ROPE_BASE ** (-jnp.arange(nf, dtype=F32) / nf)
    tok = jnp.arange(seq)
    ang_r = (tok // GRID_W).astype(F32)[:, None] * freqs
    ang_c = (tok % GRID_W).astype(F32)[:, None] * freqs
    cos = jnp.concatenate([jnp.cos(ang_r)] * 2 + [jnp.cos(ang_c)] * 2, axis=-1)
    sin = jnp.concatenate([-jnp.sin(ang_r), jnp.sin(ang_r), -jnp.sin(ang_c), jnp.sin(ang_c)], axis=-1)
    cos = jnp.concatenate([jnp.ones((n_ctx, LANES), F32), cos], axis=0)
    sin = jnp.concatenate([jnp.zeros((n_ctx, LANES), F32), sin], axis=0)
    return cos, sin


def _even_layer_mixer(xa, mod, w_in, w_out, rpb, conv_w, conv_b, wq, wk, i_bias, f_bias, gn_w,
                      lng, lnb, alpha, n_ctx):
    d = xa.shape[-1]
    n_ctx_tiles = n_ctx // TM
    ng = 2 * ML_HEADS
    main = 6 * HALF
    w_gates = jnp.zeros((d, 2 * LANES), F32)
    w_gates = w_gates.at[:, 0:ng].set(w_in[:, main:main + ng]).at[:, LANES:LANES + ng].set(w_in[:, main + ng:])
    w_all = jnp.concatenate([w_in[:, :main], w_gates], axis=1).astype(BF16)
    gate_bias = jnp.zeros((1, 2 * LANES), F32)
    gate_bias = gate_bias.at[0, 0:ng].set(i_bias.reshape(ng)).at[0, LANES:LANES + ng].set(f_bias.reshape(ng))
    outs = [(0, HALF, NA_HEAD_DIM ** -0.5, False), (HALF, HALF, 1.0, False), (2 * HALF, HALF, 1.0, False),
            (4 * HALF, HALF, 1.0, False), (5 * HALF, HALF, 1.0, False), (main, 2 * LANES, 1.0, False)]
    dts = [BF16, BF16, BF16, BF16, F32, F32, BF16, BF16]
    conv = (3 * HALF, conv_w, conv_b.reshape(1, HALF), wq.astype(BF16), wk.astype(BF16))
    q_na, k_na, v_na, zv, zo, gates, q_ml, k_ml = _inproj(xa, mod, w_all, outs, dts, n_ctx_tiles, conv=conv)
    na = _na(q_na, k_na, v_na, rpb, n_ctx)
    hf, hb = _mlscan(q_ml, k_ml, zv, gates, gate_bias, n_ctx)
    return _out_call(
        _out_even_body, "out_even", xa, mod, [na, hf, hb, zo],
        [gn_w.reshape(1, HALF), w_out.astype(BF16), lng.reshape(1, d), lnb.reshape(1, d)], alpha, n_ctx_tiles)


def _odd_layer_mixer(xa, mod, w_in, w_out, s5_params, d_skip, glu_w, glu_b, decay_logit, gn_w,
                     lng, lnb, alpha, n_ctx, rope_tabs):
    d = xa.shape[-1]
    n_ctx_tiles = n_ctx // TM
    outs = [(0, HALF, 1.0, False), (HALF, HALF, 1.0, True), (2 * HALF, HALF, RET_HEAD_DIM ** -0.5, True),
            (3 * HALF, HALF, 1.0, False), (4 * HALF, HALF, 1.0, False)]
    dts = [F32, BF16, BF16, BF16, F32]
    u, q_r, k_r, v_r, g_r = _inproj(xa, mod, w_in.astype(BF16), outs, dts, n_ctx_tiles, rope_tabs)
    ysf, ysb = _s5(u, *_s5_weights(*s5_params), n_ctx)
    rf, rb = _retscan(q_r, k_r, v_r, decay_logit, n_ctx)
    return _out_call(
        _out_odd_body, "out_odd", xa, mod, [ysf, ysb, u, rf, rb, g_r],
        [d_skip.reshape(1, HALF), glu_w.astype(BF16), glu_b.reshape(1, HALF), gn_w.reshape(1, HALF),
         w_out.astype(BF16), lng.reshape(1, d), lnb.reshape(1, d)], alpha, n_ctx_tiles)


def kernel(x, c, ctx, c_ctx, ada_w, ada_b, ffn_w_gate, ffn_w_up, ffn_w_down, ln_g, ln_b, ev_w_in, ev_w_out, na_rpb, ml_conv_w, ml_conv_b, ml_wq, ml_wk, ml_i_bias, ml_f_bias, ml_gn_w, od_w_in, od_w_out, s5_lam_re, s5_lam_im, s5_log_dt, s5_b_re, s5_b_im, s5_c_re, s5_c_im, s5_d, s5_glu_w, s5_glu_b, ret_decay_logit, ret_gn_w):
    bsz, seq, d = x.shape
    n_ctx = ctx.shape[1]
    depth = ada_w.shape[0]
    assert n_ctx % TM == 0 and seq % TM == 0 and seq % GRID_W == 0 and bsz == SUBLANES
    n_ctx_tiles = n_ctx // TM
    alpha = (2.0 * depth) ** 0.25

    rows = bsz + SUBLANES
    c_rows = jnp.zeros((rows, d), F32).at[:bsz].set(c).at[bsz].set(c_ctx)
    mod_all = _mod_table(c_rows, ada_w, ada_b)

    xa = x
    rope_tabs = _rope_tables(n_ctx, seq)
    for l in range(depth):
        e = l // 2
        mod = mod_all[l]
        ffn = lambda j, k, h, **kw: _ffn(h, mod, j, ffn_w_gate[l, k].astype(BF16), ffn_w_up[l, k].astype(BF16),
                                         ffn_w_down[l, k].astype(BF16), ln_g[l, j], ln_b[l, j], alpha,
                                         n_ctx_tiles, **kw)
        xa = ffn(0, 0, xa, ctx=ctx) if l == 0 else ffn(0, 0, xa)
        if l % 2 == 0:
            xa = _even_layer_mixer(xa, mod, ev_w_in[e], ev_w_out[e], na_rpb[e], ml_conv_w[e], ml_conv_b[e],
                                   ml_wq[e], ml_wk[e], ml_i_bias[e], ml_f_bias[e], ml_gn_w[e],
                                   ln_g[l, 1], ln_b[l, 1], alpha, n_ctx)
        else:
            s5_params = (s5_lam_re[e], s5_lam_im[e], s5_log_dt[e], s5_b_re[e], s5_b_im[e], s5_c_re[e], s5_c_im[e])
            xa = _odd_layer_mixer(xa, mod, od_w_in[e], od_w_out[e], s5_params, s5_d[e], s5_glu_w[e], s5_glu_b[e],
                                  ret_decay_logit[e], ret_gn_w[e], ln_g[l, 1], ln_b[l, 1], alpha, n_ctx, rope_tabs)
        xa = ffn(2, 1, xa, latent_only=(l == depth - 1))
    return xa
```

```python
import functools
import math

import jax
import jax.numpy as jnp
from jax import lax
from jax.experimental import pallas as pl
from jax.experimental.pallas import tpu as pltpu

F32 = jnp.float32
BF16 = jnp.bfloat16

GRID_W = 64
LN_EPS = 1e-5
N_MOD = 9
NA_HEADS, NA_HEAD_DIM, NA_WIN_H, NA_WIN_W = 8, 64, 8, 16
ML_HEADS, ML_HEAD_DIM, ML_CONV = 4, 128, 5
S5_GROUP, S5_GROUPS, S5_STATE = 16, 32, 64
RET_HEADS, RET_HEAD_DIM = 4, 128
ROPE_BASE = 10000.0
HALF = 512

LANES = 128
SUBLANES = 8
VMEM_LIMIT = 56 * 1024 * 1024

TM = 256
CHUNK = 256
ML_NB, RET_NB = 1, 2
S5_SUB = 8
S5_OCT = LANES // S5_GROUP
NEG = -1e30


def _cparams(sem):
    return pltpu.CompilerParams(dimension_semantics=sem, vmem_limit_bytes=VMEM_LIMIT)


def _const_spec(shape):
    nd = len(shape)
    return pl.BlockSpec(shape, lambda *_: (0,) * nd, pipeline_mode=pl.Buffered(1))


def _dot(a, b):
    return jnp.dot(a, b, preferred_element_type=F32)


def _dot_nt(a, b):
    return lax.dot_general(a, b, (((1,), (1,)), ((), ())), preferred_element_type=F32)


def _dot_tn(a, b):
    return lax.dot_general(a, b, (((0,), (0,)), ((), ())), preferred_element_type=F32)


def _sigmoid(x):
    return 1.0 / (1.0 + jnp.exp(-x))


def _silu(x):
    return x * _sigmoid(x)


def _log_sigmoid(x):
    return jnp.minimum(x, 0.0) - jnp.log1p(jnp.exp(-jnp.abs(x)))


def _gelu_tanh(x):
    return 0.5 * x * (1.0 + jnp.tanh(math.sqrt(2.0 / math.pi) * (x + 0.044715 * (x * x * x))))


def _layer_norm(r, g, b):
    mu = jnp.mean(r, axis=-1, keepdims=True)
    c = r - mu
    var = jnp.mean(c * c, axis=-1, keepdims=True)
    return c * lax.rsqrt(var + LN_EPS) * g + b


def _head_norm(h, w, n_heads, head_dim):
    parts = []
    for k in range(n_heads):
        hk = h[:, k * head_dim:(k + 1) * head_dim]
        mu = jnp.mean(hk, axis=-1, keepdims=True)
        c = hk - mu
        var = jnp.mean(c * c, axis=-1, keepdims=True)
        parts.append(c * lax.rsqrt(var + LN_EPS))
    return jnp.concatenate(parts, axis=-1) * w


def _mod_row(mod_ref, j, k):
    i = 3 * j + k
    return mod_ref[i:i + 1, :]


def _mod_kernel(c_ref, w_ref, b_ref, o_ref):
    c = c_ref[...]
    s = _silu(c).astype(BF16)
    o_ref[...] = _dot(s, w_ref[...].astype(BF16)) + b_ref[...]


def _mod_table(c_rows, ada_w, ada_b):
    depth, d, n = ada_w.shape
    rows = c_rows.shape[0]
    tn = n // 4 if n % (4 * LANES) == 0 else LANES
    out = pl.pallas_call(
        _mod_kernel,
        grid=(depth, n // tn),
        in_specs=[pl.BlockSpec((rows, d), lambda l, j: (0, 0)),
                  pl.BlockSpec((None, d, tn), lambda l, j: (l, 0, j)),
                  pl.BlockSpec((None, 1, tn), lambda l, j: (l, 0, j))],
        out_specs=pl.BlockSpec((None, rows, tn), lambda l, j: (l, 0, j)),
        out_shape=jax.ShapeDtypeStruct((depth, rows, n), F32),
        compiler_params=_cparams(("parallel", "parallel")),
        name="mod_table",
    )(c_rows, ada_w, ada_b.reshape(depth, 1, n))
    return out.reshape(depth, rows, N_MOD, d)


PAIR = 4


def _tok_spec(width, tm=TM):
    return pl.BlockSpec((None, tm, width), lambda b, t: (b, t, 0))


def _tile_index(h, nt, nct, bsz, latent_only):
    if latent_only:
        per = (nt - nct) // PAIR

        def index(i):
            b = i // per
            return b, nct + PAIR * (i % per) + h, b
    else:
        def index(i):
            g = PAIR * i + h
            b, t = g // nt, g % nt
            return b, t, jnp.where(t < nct, bsz, b)
    return index


def _token_call(body, name, tok, mod, tables, consts, out_widths, out_dtypes, nt, nct, latent_only=False):
    bsz, d = tok[0][0].shape[0], mod.shape[-1]
    n_tiles = bsz * ((nt - nct) if latent_only else nt)
    assert n_tiles % PAIR == 0 and (not latent_only or (nt - nct) % PAIR == 0)
    index = [_tile_index(h, nt, nct, bsz, latent_only) for h in range(PAIR)]
    per = TM // SUBLANES
    last = nt * per - 1
    tile_of = {'s': lambda t: t, 'c': lambda t: jnp.minimum(t, nct - 1), 'l': lambda t: jnp.maximum(t - nct, 0),
               'p': lambda t: jnp.maximum(t * per - 1, 0), 'n': lambda t: jnp.minimum((t + 1) * per, last)}
    in_specs, args = [], []
    for h in range(PAIR):
        for arr, kind in tok:
            in_specs.append(pl.BlockSpec(
                (None, SUBLANES if kind in 'pn' else TM, arr.shape[-1]),
                lambda i, f=index[h], k=tile_of[kind]: (f(i)[0], k(f(i)[1]), 0)))
            args.append(arr)
        in_specs.append(pl.BlockSpec((None, N_MOD, d), lambda i, f=index[h]: (f(i)[2], 0, 0)))
        args.append(mod)
        for tab in tables:
            in_specs.append(pl.BlockSpec((TM, tab.shape[-1]), lambda i, f=index[h]: (f(i)[1], 0)))
            args.append(tab)
    in_specs += [_const_spec(c.shape) for c in consts]
    args += list(consts)
    n_half = len(tok) + 1 + len(tables)

    def kern(*refs):
        const_refs = refs[PAIR * n_half:PAIR * n_half + len(consts)]
        out_refs = refs[PAIR * n_half + len(consts):]
        for h in range(PAIR):
            hr = refs[h * n_half:(h + 1) * n_half]
            tile = index[h](pl.program_id(0))[1]
            res = body(tile, hr[:len(tok)], hr[len(tok)], hr[len(tok) + 1:], const_refs)
            for o_ref, r in zip(out_refs, res):
                o_ref[h * TM:(h + 1) * TM, :] = r.astype(o_ref.dtype)

    outs = pl.pallas_call(
        kern,
        grid=(n_tiles // PAIR,),
        in_specs=in_specs,
        out_specs=[pl.BlockSpec((PAIR * TM, w), lambda i: (i, 0)) for w in out_widths],
        out_shape=[jax.ShapeDtypeStruct((n_tiles * TM, w), dt) for w, dt in zip(out_widths, out_dtypes)],
        compiler_params=_cparams(("parallel",)),
        name=name,
    )(*args)
    return [o.reshape(bsz, -1, o.shape[-1]) for o in outs]


def _ffn_body(tile, tok, mod_ref, tables, consts, *, j, alpha, nct):
    wg_ref, wu_ref, wd_ref, lng_ref, lnb_ref = consts
    if len(tok) == 2:
        x = jnp.where(tile < nct, tok[0][...], tok[1][...])
    else:
        x = tok[0][...]
    h = (x * (1.0 + _mod_row(mod_ref, j, 1)) + _mod_row(mod_ref, j, 0)).astype(BF16)
    g = _dot(h, wg_ref[...])
    u = _dot(h, wu_ref[...])
    a = (_silu(g) * u).astype(BF16)
    y = _dot(a, wd_ref[...])
    r = alpha * x + (0.5 * _mod_row(mod_ref, j, 2)) * y
    return [_layer_norm(r, lng_ref[...], lnb_ref[...])]


def _ffn(xa, mod, j, wg, wu, wd, lng, lnb, alpha, n_ctx_tiles, ctx=None, latent_only=False):
    d = xa.shape[-1]
    nct = n_ctx_tiles
    tok = [(xa, 's')] if ctx is None else [(ctx, 'c'), (xa, 'l')]
    nt = xa.shape[1] // TM + (0 if ctx is None else nct)
    return _token_call(
        functools.partial(_ffn_body, j=j, alpha=alpha, nct=nct), "ffn", tok, mod, [],
        [wg, wu, wd, lng.reshape(1, d), lnb.reshape(1, d)], [d], [F32], nt, nct, latent_only)[0]


def _rope(z, cos, sin):
    lane = lax.broadcasted_iota(jnp.int32, (1, LANES), 1)
    first = (lane % 64) < 32
    parts = []
    for k in range(z.shape[1] // LANES):
        zk = z[:, k * LANES:(k + 1) * LANES]
        partner = jnp.where(first, pltpu.roll(zk, LANES - 32, axis=1), pltpu.roll(zk, 32, axis=1))
        parts.append(zk * cos + partner * sin)
    return jnp.concatenate(parts, axis=-1)


def _inproj_body(tile, tok, mod_ref, tables, consts, *, outs, conv):
    w_ref = consts[0]
    modulate = lambda x: (x * (1.0 + _mod_row(mod_ref, 1, 1)) + _mod_row(mod_ref, 1, 0)).astype(BF16)
    h = modulate(tok[0][...])
    res, tail = [], []
    if conv is not None:
        off, nct, nt = conv
        cw_ref, cb_ref, wq_ref, wk_ref = consts[1:]
        wx = w_ref[:, off:off + HALF]
        has_prev = jnp.logical_and(tile != 0, tile != nct)
        has_next = jnp.logical_and(tile != nct - 1, tile != nt - 1)
        prev = jnp.where(has_prev, _dot(modulate(tok[1][...]), wx), 0.0)
        nxt = jnp.where(has_next, _dot(modulate(tok[2][...]), wx), 0.0)
        ext = jnp.concatenate([prev, _dot(h, wx), nxt], axis=0)
        acc = cb_ref[...]
        for j in range(ML_CONV):
            o = SUBLANES + j - ML_CONV // 2
            acc = acc + cw_ref[j:j + 1, :] * ext[o:o + TM, :]
        xc = _silu(acc).astype(BF16)
        heads = [slice(k * ML_HEAD_DIM, (k + 1) * ML_HEAD_DIM) for k in range(ML_HEADS)]
        tail.append(jnp.concatenate([_dot(xc[:, hs], wq_ref[k]) for k, hs in enumerate(heads)], axis=-1))
        tail.append(jnp.concatenate([_dot(xc[:, hs], wk_ref[k]) for k, hs in enumerate(heads)], axis=-1)
                    * ML_HEAD_DIM ** -0.5)
    for off, width, scale, rope in outs:
        z = _dot(h, w_ref[:, off:off + width])
        if rope:
            z = _rope(z, tables[0][...], tables[1][...])
        if scale != 1.0:
            z = z * scale
        res.append(z)
    return res + tail


def _inproj(xa, mod, w, outs, dtypes, n_ctx_tiles, rope_tabs=(), conv=None):
    nt = xa.shape[1] // TM
    tok, consts, widths, conv_static = [(xa, 's')], [w], [o[1] for o in outs], None
    if conv is not None:
        tok += [(xa, 'p'), (xa, 'n')]
        consts += list(conv[1:])
        widths += [HALF, HALF]
        conv_static = (conv[0], n_ctx_tiles, nt)
    return _token_call(
        functools.partial(_inproj_body, outs=tuple(outs), conv=conv_static), "inproj", tok, mod, list(rope_tabs),
        consts, widths, dtypes, nt, n_ctx_tiles)


NA_ROWS_PER_TILE = TM // GRID_W
NA_SLAB_ROWS = NA_WIN_H + NA_ROWS_PER_TILE


def _softmax_pv(s_list, v_list):
    m = s_list[0].max(axis=-1, keepdims=True)
    for s in s_list[1:]:
        m = jnp.maximum(m, s.max(axis=-1, keepdims=True))
    acc, den = None, None
    for s, v in zip(s_list, v_list):
        p = jnp.exp(s - m)
        l = p.sum(axis=-1, keepdims=True)
        o = _dot(p.astype(BF16), v)
        acc = o if acc is None else acc + o
        den = l if den is None else den + l
    return acc / den


def _na_kernel(q_ref, k_ref, v_ref, bias_ref, o_ref, *, n_ctx, rows, n_ctx_tiles):
    t = pl.program_id(1)
    n_loc = NA_SLAB_ROWS * GRID_W
    lane = lax.broadcasted_iota(jnp.int32, (1, LANES), 1)
    low = lane < NA_HEAD_DIM
    zero = jnp.zeros((), BF16)

    def pair(p, keys):
        ls = slice(p * LANES, (p + 1) * LANES)
        q2 = q_ref[:, ls]
        outs = []
        for e in range(2):
            qm = jnp.where(low if e == 0 else jnp.logical_not(low), q2, zero)
            s_list, v_list = [], []
            for kk, vv, bias in keys(ls, 2 * p + e):
                s = _dot_nt(qm, kk)
                s_list.append(s if bias is None else s + bias)
                v_list.append(vv)
            outs.append(_softmax_pv(s_list, v_list))
        o_ref[:, ls] = jnp.where(low, outs[0], outs[1]).astype(o_ref.dtype)

    @pl.when(t < n_ctx_tiles)
    def _ctx():
        for p in range(NA_HEADS // 2):
            pair(p, lambda ls, h: [(k_ref[0:n_ctx, ls], v_ref[0:n_ctx, ls], None)])

    @pl.when(t >= n_ctx_tiles)
    def _latent():
        r0 = (t - n_ctx_tiles) * NA_ROWS_PER_TILE
        rs0 = jnp.clip(r0 - NA_WIN_H // 2, 0, rows - NA_SLAB_ROWS)
        var = (r0 - rs0) // NA_ROWS_PER_TILE
        start = pl.multiple_of(n_ctx + rs0 * GRID_W, GRID_W)
        for p in range(NA_HEADS // 2):
            pair(p, lambda ls, h: [(k_ref[pl.ds(start, n_loc), ls], v_ref[pl.ds(start, n_loc), ls], bias_ref[var, h]),
                                   (k_ref[0:n_ctx, ls], v_ref[0:n_ctx, ls], None)])


def _na_bias_table(rpb, rows):
    rpt, slab = NA_ROWS_PER_TILE, NA_SLAB_ROWS
    cols = jnp.arange(GRID_W)
    cs = jnp.clip(cols - NA_WIN_W // 2, 0, GRID_W - NA_WIN_W)
    kc = jnp.arange(GRID_W)
    in_win = (kc[None, :] >= cs[:, None]) & (kc[None, :] < cs[:, None] + NA_WIN_W)
    col_off = kc[None, :] - cols[:, None] + NA_WIN_W - 1
    diff = (jnp.arange(3) * rpt)[:, None, None]
    i = jnp.arange(rpt)[None, :, None]
    a = jnp.arange(slab)[None, None, :]
    w0 = jnp.clip(diff + i - NA_WIN_H // 2, 0, slab - NA_WIN_H)
    row_ok = (a >= w0) & (a < w0 + NA_WIN_H)
    row_off = a - diff - i + NA_WIN_H - 1
    row_sel = ((row_off[..., None] == jnp.arange(2 * NA_WIN_H - 1)) & row_ok[..., None]).astype(F32)
    col_sel = (col_off[:, :, None] == jnp.arange(2 * NA_WIN_W - 1)).astype(F32)
    tab = jnp.einsum('hrc,viar,qkc->vhiqak', rpb.astype(F32), row_sel, col_sel, precision=lax.Precision.HIGHEST)
    ok = row_ok[:, None, :, None, :, None] & in_win[None, None, None, :, None, :]
    tab = jnp.where(ok, tab, NEG)
    return tab.reshape(3, NA_HEADS, rpt * GRID_W, slab * GRID_W)


def _na(q, k, v, rpb, n_ctx):
    bsz, t, w = q.shape
    rows = (t - n_ctx) // GRID_W
    assert rows >= NA_SLAB_ROWS and rows % NA_ROWS_PER_TILE == 0
    bias = _na_bias_table(rpb, rows)
    return pl.pallas_call(
        functools.partial(_na_kernel, n_ctx=n_ctx, rows=rows, n_ctx_tiles=n_ctx // TM),
        grid=(bsz, t // TM),
        in_specs=[_tok_spec(w),
                  pl.BlockSpec((None, t, w), lambda b, i: (b, 0, 0)),
                  pl.BlockSpec((None, t, w), lambda b, i: (b, 0, 0)),
                  _const_spec(bias.shape)],
        out_specs=_tok_spec(w),
        out_shape=jax.ShapeDtypeStruct((bsz, t, w), BF16),
        compiler_params=_cparams(("parallel", "arbitrary")),
        name="natten",
    )(q, k, v, bias)


def _scan_specs(width, n_chunks, n_ctx_chunks, nb):
    fwd = pl.BlockSpec((nb, CHUNK, width), lambda b, i: (b, i, 0))
    bwd = pl.BlockSpec(
        (nb, CHUNK, width),
        lambda b, i: (b, jnp.where(i < n_ctx_chunks, n_ctx_chunks - 1 - i, n_chunks + n_ctx_chunks - 1 - i), 0))
    return fwd, bwd


def _split3(x):
    hi = x.astype(BF16)
    r1 = x - hi.astype(F32)
    mid = r1.astype(BF16)
    lo = (r1 - mid.astype(F32)).astype(BF16)
    return hi, mid, lo


def _mlscan_kernel(qf_ref, kf_ref, vf_ref, gf_ref, qb_ref, kb_ref, vb_ref, gb_ref, bias_ref,
                   of_ref, ob_ref, st_ref, m_ref):
    i = pl.program_id(1)
    dk = ML_HEAD_DIM

    @pl.when(i == 0)
    def _init():
        st_ref[...] = jnp.zeros_like(st_ref)
        m_ref[...] = jnp.zeros_like(m_ref)

    tt = lax.broadcasted_iota(jnp.int32, (CHUNK, CHUNK), 0)
    ss = lax.broadcasted_iota(jnp.int32, (CHUNK, CHUNK), 1)
    lane = lax.broadcasted_iota(jnp.int32, (CHUNK, LANES), 1)
    row = lax.broadcasted_iota(jnp.int32, (CHUNK, LANES), 0)
    ones_col = (lane == 0).astype(BF16)

    def running_max(x, reverse):
        sh = 1
        while sh < CHUNK:
            if reverse:
                shifted = jnp.where(row < CHUNK - sh, pltpu.roll(x, CHUNK - sh, axis=0), -jnp.inf)
            else:
                shifted = jnp.where(row >= sh, pltpu.roll(x, sh, axis=0), -jnp.inf)
            x = jnp.maximum(x, shifted)
            sh *= 2
        return x

    n_chain = 2 * ML_HEADS
    for bi, d in [(bi, d) for bi in range(ML_NB) for d in range(2)]:
        q_ref, k_ref, v_ref, g_ref, o_ref =((qf_ref, kf_ref, vf_ref, gf_ref, of_ref),
                                             (qb_ref, kb_ref, vb_ref, gb_ref, ob_ref))[d]
        causal = (ss <= tt) if d == 0 else (ss >= tt)
        tri = causal.astype(BF16)
        gates = g_ref[bi] + bias_ref[...]
        hi, mid, lo = _split3(_log_sigmoid(gates[:, LANES:]))
        bc = _dot(tri, hi) + _dot(tri, mid) + _dot(tri, lo)
        e = gates[:, :LANES] - bc
        e_t = e.T
        m_row = bi * 2 + d
        m_prev = m_ref[m_row:m_row + 1, :]
        g = jnp.maximum(running_max(e, reverse=d == 1), m_prev)
        w_inter = jnp.exp(m_prev - g)
        exp_neg_mt = jnp.exp(-(bc + g))
        last = CHUNK - 1 if d == 0 else 0
        b_last = bc[last:last + 1, :]
        w_log = b_last + e
        m_new = jnp.maximum(b_last + m_prev, w_log.max(axis=0, keepdims=True))
        w_in = jnp.exp(w_log - m_new)
        decay = jnp.exp(b_last + m_prev - m_new)
        m_ref[m_row:m_row + 1, :] = m_new
        for h in range(ML_HEADS):
            c = d * ML_HEADS + h
            sc = bi * n_chain + c
            hs = slice(h * dk, (h + 1) * dk)
            q, k = q_ref[bi, :, hs], k_ref[bi, :, hs]
            v_aug = jnp.concatenate([v_ref[bi, :, hs], ones_col], axis=-1)
            p = jnp.exp(jnp.where(causal, e_t[c:c + 1, :] - g[:, c:c + 1], -jnp.inf))
            s = _dot_nt(q, k) * p
            state = st_ref[sc]
            res = _dot(s.astype(BF16), v_aug) + w_inter[:, c:c + 1] * _dot(q, state.astype(BF16))
            inv = 1.0 / jnp.maximum(jnp.abs(res[:, dk:dk + 1]), exp_neg_mt[:, c:c + 1])
            o_ref[bi, :, hs] = res[:, :dk] * inv
            st_ref[sc] = decay[:, c:c + 1] * state + _dot_tn(k, (v_aug * w_in[:, c:c + 1]).astype(BF16))


def _mlscan(q, k, v, gates, gate_bias, n_ctx):
    bsz, t, w = q.shape
    nc, ncc = t // CHUNK, n_ctx // CHUNK
    f, bk = _scan_specs(w, nc, ncc, ML_NB)
    gf, gb = _scan_specs(2 * LANES, nc, ncc, ML_NB)
    n_chain = 2 * ML_HEADS
    assert bsz % ML_NB == 0 and 2 * ML_NB <= SUBLANES
    return pl.pallas_call(
        _mlscan_kernel,
        grid=(bsz // ML_NB, nc),
        in_specs=[f, f, f, gf, bk, bk, bk, gb, _const_spec((1, 2 * LANES))],
        out_specs=[f, bk],
        out_shape=[jax.ShapeDtypeStruct((bsz, t, w), F32)] * 2,
        scratch_shapes=[pltpu.VMEM((ML_NB * n_chain, ML_HEAD_DIM, 2 * ML_HEAD_DIM), F32),
                        pltpu.VMEM((SUBLANES, LANES), F32)],
        compiler_params=_cparams(("parallel", "arbitrary")),
        name="mlstm_scan",
    )(q, k, v, gates, q, k, v, gates, gate_bias)


def _retscan_kernel(qf_ref, kf_ref, vf_ref, qb_ref, kb_ref, vb_ref, logit_ref, of_ref, ob_ref,
                    st_ref, dmat_ref, dec_ref):
    i = pl.program_id(1)
    dk = RET_HEAD_DIM
    n_chain = 2 * RET_HEADS

    @pl.when(i == 0)
    def _init():
        st_ref[...] = jnp.zeros_like(st_ref)
        tt = lax.broadcasted_iota(jnp.int32, (CHUNK, CHUNK), 0)
        ss = lax.broadcasted_iota(jnp.int32, (CHUNK, CHUNK), 1)
        pos = lax.broadcasted_iota(jnp.int32, (CHUNK, LANES), 0).astype(F32)
        log_g = _log_sigmoid(logit_ref[...])
        for d in range(2):
            dist = (tt - ss) if d == 0 else (ss - tt)
            distf = jnp.maximum(dist, 0).astype(F32)
            step = pos if d == 0 else (CHUNK - 1.0) - pos
            for h in range(RET_HEADS):
                c = d * RET_HEADS + h
                dmat_ref[c] = jnp.where(dist >= 0, jnp.exp(distf * log_g[c:c + 1, 0:1]), 0.0)
                lg = log_g[c:c + 1, :]
                dec_ref[c, 0] = jnp.exp((step + 1.0) * lg)
                dec_ref[c, 1] = jnp.exp((CHUNK - 1.0 - step) * lg)
                dec_ref[c, 2] = jnp.broadcast_to(jnp.exp(CHUNK * lg), (CHUNK, LANES))

    for bi, d in [(bi, d) for bi in range(RET_NB) for d in range(2)]:
        q_ref, k_ref, v_ref, o_ref =((qf_ref, kf_ref, vf_ref, of_ref), (qb_ref, kb_ref, vb_ref, ob_ref))[d]
        for h in range(RET_HEADS):
            c = d * RET_HEADS + h
            sc = bi * n_chain + c
            hs = slice(h * dk, (h + 1) * dk)
            q, k, v = q_ref[bi, :, hs], k_ref[bi, :, hs], v_ref[bi, :, hs]
            state = st_ref[sc]
            inner = _dot((_dot_nt(q, k) * dmat_ref[c]).astype(BF16), v)
            o_ref[bi, :, hs] = inner + dec_ref[c, 0] * _dot(q, state.astype(BF16))
            st_ref[sc] = dec_ref[c, 2, 0:dk, :] * state + _dot_tn(k, (v.astype(F32) * dec_ref[c, 1]).astype(BF16))


def _retscan(q, k, v, decay_logit, n_ctx):
    bsz, t, w = q.shape
    nc, ncc = t // CHUNK, n_ctx // CHUNK
    f, bk = _scan_specs(w, nc, ncc, RET_NB)
    logit = jnp.broadcast_to(decay_logit.astype(F32).reshape(2 * RET_HEADS, 1), (2 * RET_HEADS, LANES))
    assert bsz % RET_NB == 0 and RET_HEAD_DIM == LANES
    n_chain = 2 * RET_HEADS
    return pl.pallas_call(
        _retscan_kernel,
        grid=(bsz // RET_NB, nc),
        in_specs=[f, f, f, bk, bk, bk, _const_spec((n_chain, LANES))],
        out_specs=[f, bk],
        out_shape=[jax.ShapeDtypeStruct((bsz, t, w), F32)] * 2,
        scratch_shapes=[pltpu.VMEM((RET_NB * n_chain, RET_HEAD_DIM, RET_HEAD_DIM), F32),
                        pltpu.VMEM((n_chain, CHUNK, CHUNK), F32),
                        pltpu.VMEM((n_chain, 3, CHUNK, LANES), F32)],
        compiler_params=_cparams(("parallel", "arbitrary")),
        name="retention_scan",
    )(q, k, v, q, k, v, logit)


def _s5_weights(lam_re, lam_im, log_dt, b_re, b_im, c_re, c_im):
    hp = lax.Precision.HIGHEST
    n = S5_SUB
    dt = jnp.exp(log_dt)[..., None]
    zr, zi = lam_re * dt, lam_im * dt
    steps = jnp.arange(n + 1, dtype=F32)[:, None, None, None]
    pmag = jnp.exp(steps * zr)
    ak_re, ak_im = pmag * jnp.cos(steps * zi), pmag * jnp.sin(steps * zi)
    a_re, a_im = ak_re[1], ak_im[1]
    lam_sq = jnp.square(lam_re) + jnp.square(lam_im)
    e_re = ((a_re - 1.0) * lam_re + a_im * lam_im) / lam_sq
    e_im = (a_im * lam_re - (a_re - 1.0) * lam_im) / lam_sq
    bb_re = e_re[..., None] * b_re - e_im[..., None] * b_im
    bb_im = e_re[..., None] * b_im + e_im[..., None] * b_re
    ab_re = ak_re[..., None] * bb_re - ak_im[..., None] * bb_im
    ab_im = ak_re[..., None] * bb_im + ak_im[..., None] * bb_re
    kern = (jnp.einsum('dgqp,kdgpr->kdgqr', c_re, ab_re, precision=hp)
            - jnp.einsum('dgqp,kdgpr->kdgqr', c_im, ab_im, precision=hp))
    sig = jnp.arange(n)
    zeros = jnp.zeros_like(kern[:n])
    toe = jnp.stack([jnp.concatenate([zeros[:sp], kern[:n - sp]], axis=0) for sp in range(n)], axis=0)
    toe = toe.transpose(2, 3, 0, 5, 1, 4)
    vin_re = ab_re[n - 1 - sig].transpose(1, 2, 0, 4, 3)
    vin_im = ab_im[n - 1 - sig].transpose(1, 2, 0, 4, 3)
    ap_re, ap_im = ak_re[1:], ak_im[1:]
    w_re = c_re[None] * ap_re[:, :, :, None, :] - c_im[None] * ap_im[:, :, :, None, :]
    w_im = -(c_re[None] * ap_im[:, :, :, None, :] + c_im[None] * ap_re[:, :, :, None, :])
    w_re = w_re.transpose(1, 2, 4, 0, 3)
    w_im = w_im.transpose(1, 2, 4, 0, 3)

    def orient(a, axes):
        return jnp.stack([a[0], jnp.flip(a[1], axis=tuple(x - 1 for x in axes))], axis=0)

    toe = orient(toe, (2, 4))
    vin = orient(jnp.stack([vin_re, vin_im], axis=4), (2,))
    wout = orient(jnp.stack([w_re, w_im], axis=2), (4,))

    no = lam_re.shape[1] // S5_OCT
    ns = S5_OCT * S5_STATE

    def compact(a):
        a = a.reshape((2, no, S5_OCT) + a.shape[2:4] + (LANES,))
        return a.transpose(0, 1, 3, 2, 4, 5).reshape(2, no, -1, LANES).astype(BF16)

    t_c = compact(toe.reshape(toe.shape[:4] + (LANES,)))
    v_c = compact(vin.reshape(vin.shape[:4] + (LANES,)))
    w_c = compact(wout.reshape(wout.shape[:4] + (LANES,)))
    a_re = ak_re[n].reshape(2, no, 1, ns)
    a_im = ak_im[n].reshape(2, no, 1, ns)
    return t_c, v_c, w_c, a_re, a_im


def _s5_expand(c_ref, o_ref, row_blk, col_blk):
    n_rows, n_cols = o_ref.shape[1:]
    ent = lax.broadcasted_iota(jnp.int32, (LANES, n_cols), 0)
    col = lax.broadcasted_iota(jnp.int32, (LANES, n_cols), 1)
    spread = (ent == (col // (S5_OCT * col_blk)) * col_blk + col % col_blk).astype(BF16)
    rg = (lax.broadcasted_iota(jnp.int32, (n_rows, n_cols), 0) // row_blk) % S5_OCT
    cg = (lax.broadcasted_iota(jnp.int32, (n_rows, n_cols), 1) // col_blk) % S5_OCT
    for d in range(2):
        o_ref[d] = jnp.where(rg == cg, _dot(c_ref[d], spread), 0.0).astype(BF16)


def _s5_kernel(uf_ref, ub_ref, tc_ref, vc_ref, wc_ref, are_ref, aim_ref, yf_ref, yb_ref,
               t_ref, v_ref, w_ref, in_ref, xp_ref, st_ref, *, bsz, jt):
    ns = S5_OCT * S5_STATE
    rows = bsz * jt

    @pl.when(pl.program_id(1) == 0)
    def _init():
        st_ref[...] = jnp.zeros_like(st_ref)
        _s5_expand(tc_ref, t_ref, S5_GROUP, S5_GROUP)
        _s5_expand(vc_ref, v_ref, S5_GROUP, S5_STATE)
        _s5_expand(wc_ref, w_ref, S5_STATE, S5_GROUP)

    ri = lax.broadcasted_iota(jnp.int32, (rows, rows), 0)
    ci = lax.broadcasted_iota(jnp.int32, (rows, rows), 1)
    perm = lambda inner: (ci == (ri % inner) * (rows // inner) + ri // inner).astype(BF16)
    p_sj = perm(jt)
    p_jb = perm(bsz)
    p_bj = perm(jt)

    for d, (u_ref, y_ref) in enumerate(((uf_ref, yf_ref), (ub_ref, yb_ref))):
        u_rows = []
        for b in range(bsz):
            z = _dot(p_sj, u_ref[b].astype(BF16)).astype(BF16)
            u_rows.append(jnp.concatenate([z[s * jt:(s + 1) * jt] for s in range(S5_SUB)], axis=1))
        u = jnp.concatenate(u_rows, axis=0)
        blk = 2 * LANES
        y_intra = jnp.concatenate(
            [_dot(u[:, :(n + 1) * blk], t_ref[d, :(n + 1) * blk, n * blk:(n + 1) * blk]) if d == 0 else
             _dot(u[:, n * blk:], t_ref[d, n * blk:, n * blk:(n + 1) * blk])
             for n in range(S5_SUB * LANES // blk)], axis=1)
        u_jb = _dot(p_jb, u).astype(BF16)
        in_ref[d] = _dot(u_jb, v_ref[d])
        a_re, a_im = are_ref[d], aim_ref[d]
        xr, xi = st_ref[d, 0], st_ref[d, 1]
        for jj in range(jt):
            j = jj if d == 0 else jt - 1 - jj
            rows_j = slice(j * bsz, (j + 1) * bsz)
            xp_ref[d, rows_j, :] = jnp.concatenate([xr, xi], axis=1)
            inc = in_ref[d, rows_j, :]
            xr, xi = (a_re * xr - a_im * xi + inc[:, :ns], a_re * xi + a_im * xr + inc[:, ns:])
        st_ref[d, 0] = xr
        st_ref[d, 1] = xi
        xp = _dot(p_bj, xp_ref[d].astype(BF16)).astype(BF16)
        y = y_intra + _dot(xp, w_ref[d])
        for b in range(bsz):
            for s in range(S5_SUB):
                y_ref[b, pl.ds(s, jt, stride=S5_SUB), :] = y[b * jt:(b + 1) * jt, s * LANES:(s + 1) * LANES]


def _s5(u, t_c, v_c, w_c, a_re, a_im, n_ctx):
    bsz, t, width = u.shape
    assert bsz == S5_SUB
    no = width // LANES
    nt, nct = t // TM, n_ctx // TM
    jt = TM // S5_SUB
    nq, ns = S5_SUB * LANES, S5_OCT * S5_STATE
    fwd = pl.BlockSpec((bsz, TM, LANES), lambda o, i: (0, i, o))
    bwd = pl.BlockSpec((bsz, TM, LANES), lambda o, i: (0, jnp.where(i < nct, nct - 1 - i, nt + nct - 1 - i), o))
    op = lambda r, c: pl.BlockSpec((2, None, r, c), lambda o, i: (0, o, 0, 0))
    return pl.pallas_call(
        functools.partial(_s5_kernel, bsz=bsz, jt=jt),
        grid=(no, nt),
        in_specs=[fwd, bwd, op(nq, LANES), op(nq, LANES), op(2 * ns, LANES), op(1, ns), op(1, ns)],
        out_specs=[fwd, bwd],
        out_shape=[jax.ShapeDtypeStruct((bsz, t, width), F32)] * 2,
        scratch_shapes=[pltpu.VMEM((2, nq, nq), BF16), pltpu.VMEM((2, nq, 2 * ns), BF16),
                        pltpu.VMEM((2, 2 * ns, nq), BF16),
                        pltpu.VMEM((2, bsz * jt, 2 * ns), F32), pltpu.VMEM((2, bsz * jt, 2 * ns), F32),
                        pltpu.VMEM((2, 2, bsz, ns), F32)],
        compiler_params=_cparams(("parallel", "arbitrary")),
        name="s5",
    )(u, u, t_c, v_c, w_c, a_re, a_im)


def _out_even_body(tile, tok, mod_ref, tables, consts, *, alpha):
    x_ref, na_ref, hf_ref, hb_ref, zo_ref = tok
    gn_ref, w_ref, lng_ref, lnb_ref = consts
    ml = _sigmoid(zo_ref[...]) * _head_norm(hf_ref[...] + hb_ref[...], gn_ref[...], ML_HEADS, ML_HEAD_DIM)
    y = _dot(na_ref[...].astype(BF16), w_ref[0:HALF, :]) + _dot(ml.astype(BF16), w_ref[HALF:, :])
    r = alpha * x_ref[...] + _mod_row(mod_ref, 1, 2) * y
    return [_layer_norm(r, lng_ref[...], lnb_ref[...])]


def _out_odd_body(tile, tok, mod_ref, tables, consts, *, alpha):
    x_ref, ysf_ref, ysb_ref, u_ref, rf_ref, rb_ref, gr_ref = tok
    dsk_ref, gw_ref, gb_ref, gn_ref, w_ref, lng_ref, lnb_ref = consts
    s = _gelu_tanh(ysf_ref[...] + ysb_ref[...] + dsk_ref[...] * u_ref[...])
    s = s * _sigmoid(_dot(s.astype(BF16), gw_ref[...]) + gb_ref[...])
    ret = _silu(gr_ref[...]) * _head_norm(rf_ref[...] + rb_ref[...], gn_ref[...], RET_HEADS, RET_HEAD_DIM)
    y = _dot(s.astype(BF16), w_ref[0:HALF, :]) + _dot(ret.astype(BF16), w_ref[HALF:, :])
    r = alpha * x_ref[...] + _mod_row(mod_ref, 1, 2) * y
    return [_layer_norm(r, lng_ref[...], lnb_ref[...])]


def _out_call(body, name, xa, mod, tok, consts, alpha, n_ctx_tiles):
    d = xa.shape[-1]
    return _token_call(functools.partial(body, alpha=alpha), name, [(xa, 's')] + [(a, 's') for a in tok], mod,
                       [], consts, [d], [F32], xa.shape[1] // TM, n_ctx_tiles)[0]


def _rope_tables(n_ctx, seq):
    nf = RET_HEAD_DIM // 4
    freqs = ROPE_BASE ** (-jnp.arange(nf, dtype=F32) / nf)
    tok = jnp.arange(seq)
    ang_r = (tok // GRID_W).astype(F32)[:, None] * freqs
    ang_c = (tok % GRID_W).astype(F32)[:, None] * freqs
    cos = jnp.concatenate([jnp.cos(ang_r)] * 2 + [jnp.cos(ang_c)] * 2, axis=-1)
    sin = jnp.concatenate([-jnp.sin(ang_r), jnp.sin(ang_r), -jnp.sin(ang_c), jnp.sin(ang_c)], axis=-1)
    cos = jnp.concatenate([jnp.ones((n_ctx, LANES), F32), cos], axis=0)
    sin = jnp.concatenate([jnp.zeros((n_ctx, LANES), F32), sin], axis=0)
    return cos, sin


def _even_layer_mixer(xa, mod, w_in, w_out, rpb, conv_w, conv_b, wq, wk, i_bias, f_bias, gn_w,
                      lng, lnb, alpha, n_ctx):
    d = xa.shape[-1]
    n_ctx_tiles = n_ctx // TM
    ng = 2 * ML_HEADS
    main = 6 * HALF
    w_gates = jnp.zeros((d, 2 * LANES), F32)
    w_gates = w_gates.at[:, 0:ng].set(w_in[:, main:main + ng]).at[:, LANES:LANES + ng].set(w_in[:, main + ng:])
    w_all = jnp.concatenate([w_in[:, :main], w_gates], axis=1).astype(BF16)
    gate_bias = jnp.zeros((1, 2 * LANES), F32)
    gate_bias = gate_bias.at[0, 0:ng].set(i_bias.reshape(ng)).at[0, LANES:LANES + ng].set(f_bias.reshape(ng))
    outs = [(0, HALF, NA_HEAD_DIM ** -0.5, False), (HALF, HALF, 1.0, False), (2 * HALF, HALF, 1.0, False),
            (4 * HALF, HALF, 1.0, False), (5 * HALF, HALF, 1.0, False), (main, 2 * LANES, 1.0, False)]
    dts = [BF16, BF16, BF16, BF16, F32, F32, BF16, BF16]
    conv = (3 * HALF, conv_w, conv_b.reshape(1, HALF), wq.astype(BF16), wk.astype(BF16))
    q_na, k_na, v_na, zv, zo, gates, q_ml, k_ml = _inproj(xa, mod, w_all, outs, dts, n_ctx_tiles, conv=conv)
    na = _na(q_na, k_na, v_na, rpb, n_ctx)
    hf, hb = _mlscan(q_ml, k_ml, zv, gates, gate_bias, n_ctx)
    return _out_call(
        _out_even_body, "out_even", xa, mod, [na, hf, hb, zo],
        [gn_w.reshape(1, HALF), w_out.astype(BF16), lng.reshape(1, d), lnb.reshape(1, d)], alpha, n_ctx_tiles)


def _odd_layer_mixer(xa, mod, w_in, w_out, s5_params, d_skip, glu_w, glu_b, decay_logit, gn_w,
                     lng, lnb, alpha, n_ctx, rope_tabs):
    d = xa.shape[-1]
    n_ctx_tiles = n_ctx // TM
    outs = [(0, HALF, 1.0, False), (HALF, HALF, 1.0, True), (2 * HALF, HALF, RET_HEAD_DIM ** -0.5, True),
            (3 * HALF, HALF, 1.0, False), (4 * HALF, HALF, 1.0, False)]
    dts = [F32, BF16, BF16, BF16, F32]
    u, q_r, k_r, v_r, g_r = _inproj(xa, mod, w_in.astype(BF16), outs, dts, n_ctx_tiles, rope_tabs)
    ysf, ysb = _s5(u, *_s5_weights(*s5_params), n_ctx)
    rf, rb = _retscan(q_r, k_r, v_r, decay_logit, n_ctx)
    return _out_call(
        _out_odd_body, "out_odd", xa, mod, [ysf, ysb, u, rf, rb, g_r],
        [d_skip.reshape(1, HALF), glu_w.astype(BF16), glu_b.reshape(1, HALF), gn_w.reshape(1, HALF),
         w_out.astype(BF16), lng.reshape(1, d), lnb.reshape(1, d)], alpha, n_ctx_tiles)


def kernel(x, c, ctx, c_ctx, ada_w, ada_b, ffn_w_gate, ffn_w_up, ffn_w_down, ln_g, ln_b, ev_w_in, ev_w_out, na_rpb, ml_conv_w, ml_conv_b, ml_wq, ml_wk, ml_i_bias, ml_f_bias, ml_gn_w, od_w_in, od_w_out, s5_lam_re, s5_lam_im, s5_log_dt, s5_b_re, s5_b_im, s5_c_re, s5_c_im, s5_d, s5_glu_w, s5_glu_b, ret_decay_logit, ret_gn_w):
    bsz, seq, d = x.shape
    n_ctx = ctx.shape[1]
    depth = ada_w.shape[0]
    assert n_ctx % TM == 0 and seq % TM == 0 and seq % GRID_W == 0 and bsz == SUBLANES
    n_ctx_tiles = n_ctx // TM
    alpha = (2.0 * depth) ** 0.25

    rows = bsz + SUBLANES
    c_rows = jnp.zeros((rows, d), F32).at[:bsz].set(c).at[bsz].set(c_ctx)
    mod_all = _mod_table(c_rows, ada_w, ada_b)

    xa = x
    rope_tabs = _rope_tables(n_ctx, seq)
    for l in range(depth):
        e = l // 2
        mod = mod_all[l]
        ffn = lambda j, k, h, **kw: _ffn(h, mod, j, ffn_w_gate[l, k].astype(BF16), ffn_w_up[l, k].astype(BF16),
                                         ffn_w_down[l, k].astype(BF16), ln_g[l, j], ln_b[l, j], alpha,
                                         n_ctx_tiles, **kw)
        xa = ffn(0, 0, xa, ctx=ctx) if l == 0 else ffn(0, 0, xa)
        if l % 2 == 0:
            xa = _even_layer_mixer(xa, mod, ev_w_in[e], ev_w_out[e], na_rpb[e], ml_conv_w[e], ml_conv_b[e],
                                   ml_wq[e], ml_wk[e], ml_i_bias[e], ml_f_bias[e], ml_gn_w[e],
                                   ln_g[l, 1], ln_b[l, 1], alpha, n_ctx)
        else:
            s5_params = (s5_lam_re[e], s5_lam_im[e], s5_log_dt[e], s5_b_re[e], s5_b_im[e], s5_c_re[e], s5_c_im[e])
            xa = _odd_layer_mixer(xa, mod, od_w_in[e], od_w_out[e], s5_params, s5_d[e], s5_glu_w[e], s5_glu_b[e],
                                  ret_decay_logit[e], ret_gn_w[e], ln_g[l, 1], ln_b[l, 1], alpha, n_ctx, rope_tabs)
        xa = ffn(2, 1, xa, latent_only=(l == depth - 1))
    return xa
```

```python
import functools
import math

import jax
import jax.numpy as jnp
from jax import lax
from jax.experimental import pallas as pl
from jax.experimental.pallas import tpu as pltpu

F32 = jnp.float32
BF16 = jnp.bfloat16

GRID_W = 64
LN_EPS = 1e-5
N_MOD = 9
NA_HEADS, NA_HEAD_DIM, NA_WIN_H, NA_WIN_W = 8, 64, 8, 16
ML_HEADS, ML_HEAD_DIM, ML_CONV = 4, 128, 5
S5_GROUP, S5_GROUPS, S5_STATE = 16, 32, 64
RET_HEADS, RET_HEAD_DIM = 4, 128
ROPE_BASE = 10000.0
HALF = 512

LANES = 128
SUBLANES = 8
VMEM_LIMIT = 56 * 1024 * 1024

TM = 256
CHUNK = 256
ML_NB, RET_NB = 1, 2
S5_SUB = 8
S5_OCT = LANES // S5_GROUP
NEG = -1e30


def _cparams(sem):
    return pltpu.CompilerParams(dimension_semantics=sem, vmem_limit_bytes=VMEM_LIMIT)


def _const_spec(shape):
    nd = len(shape)
    return pl.BlockSpec(shape, lambda *_: (0,) * nd, pipeline_mode=pl.Buffered(1))


def _dot(a, b):
    return jnp.dot(a, b, preferred_element_type=F32)


def _dot_nt(a, b):
    return lax.dot_general(a, b, (((1,), (1,)), ((), ())), preferred_element_type=F32)


def _dot_tn(a, b):
    return lax.dot_general(a, b, (((0,), (0,)), ((), ())), preferred_element_type=F32)


def _sigmoid(x):
    return 1.0 / (1.0 + jnp.exp(-x))


def _silu(x):
    return x * _sigmoid(x)


def _log_sigmoid(x):
    return jnp.minimum(x, 0.0) - jnp.log1p(jnp.exp(-jnp.abs(x)))


def _gelu_tanh(x):
    return 0.5 * x * (1.0 + jnp.tanh(math.sqrt(2.0 / math.pi) * (x + 0.044715 * (x * x * x))))


def _layer_norm(r, g, b):
    mu = jnp.mean(r, axis=-1, keepdims=True)
    c = r - mu
    var = jnp.mean(c * c, axis=-1, keepdims=True)
    return c * lax.rsqrt(var + LN_EPS) * g + b


def _head_norm(h, w, n_heads, head_dim):
    parts = []
    for k in range(n_heads):
        hk = h[:, k * head_dim:(k + 1) * head_dim]
        mu = jnp.mean(hk, axis=-1, keepdims=True)
        c = hk - mu
        var = jnp.mean(c * c, axis=-1, keepdims=True)
        parts.append(c * lax.rsqrt(var + LN_EPS))
    return jnp.concatenate(parts, axis=-1) * w


def _mod_row(mod_ref, j, k):
    i = 3 * j + k
    return mod_ref[i:i + 1, :]


def _mod_kernel(c_ref, w_ref, b_ref, o_ref):
    c = c_ref[...]
    s = _silu(c).astype(BF16)
    o_ref[...] = _dot(s, w_ref[...].astype(BF16)) + b_ref[...]


def _mod_table(c_rows, ada_w, ada_b):
    depth, d, n = ada_w.shape
    rows = c_rows.shape[0]
    tn = n // 4 if n % (4 * LANES) == 0 else LANES
    out = pl.pallas_call(
        _mod_kernel,
        grid=(depth, n // tn),
        in_specs=[pl.BlockSpec((rows, d), lambda l, j: (0, 0)),
                  pl.BlockSpec((None, d, tn), lambda l, j: (l, 0, j)),
                  pl.BlockSpec((None, 1, tn), lambda l, j: (l, 0, j))],
        out_specs=pl.BlockSpec((None, rows, tn), lambda l, j: (l, 0, j)),
        out_shape=jax.ShapeDtypeStruct((depth, rows, n), F32),
        compiler_params=_cparams(("parallel", "parallel")),
        name="mod_table",
    )(c_rows, ada_w, ada_b.reshape(depth, 1, n))
    return out.reshape(depth, rows, N_MOD, d)


PAIR = 4


def _tok_spec(width, tm=TM):
    return pl.BlockSpec((None, tm, width), lambda b, t: (b, t, 0))


def _tile_index(h, nt, nct, bsz, latent_only):
    if latent_only:
        per = (nt - nct) // PAIR

        def index(i):
            b = i // per
            return b, nct + PAIR * (i % per) + h, b
    else:
        def index(i):
            g = PAIR * i + h
            b, t = g // nt, g % nt
            return b, t, jnp.where(t < nct, bsz, b)
    return index


def _token_call(body, name, tok, mod, tables, consts, out_widths, out_dtypes, nt, nct, latent_only=False):
    bsz, d = tok[0][0].shape[0], mod.shape[-1]
    n_tiles = bsz * ((nt - nct) if latent_only else nt)
    assert n_tiles % PAIR == 0 and (not latent_only or (nt - nct) % PAIR == 0)
    index = [_tile_index(h, nt, nct, bsz, latent_only) for h in range(PAIR)]
    per = TM // SUBLANES
    last = nt * per - 1
    tile_of = {'s': lambda t: t, 'c': lambda t: jnp.minimum(t, nct - 1), 'l': lambda t: jnp.maximum(t - nct, 0),
               'p': lambda t: jnp.maximum(t * per - 1, 0), 'n': lambda t: jnp.minimum((t + 1) * per, last)}
    in_specs, args = [], []
    for h in range(PAIR):
        for arr, kind in tok:
            in_specs.append(pl.BlockSpec(
                (None, SUBLANES if kind in 'pn' else TM, arr.shape[-1]),
                lambda i, f=index[h], k=tile_of[kind]: (f(i)[0], k(f(i)[1]), 0)))
            args.append(arr)
        in_specs.append(pl.BlockSpec((None, N_MOD, d), lambda i, f=index[h]: (f(i)[2], 0, 0)))
        args.append(mod)
        for tab in tables:
            in_specs.append(pl.BlockSpec((TM, tab.shape[-1]), lambda i, f=index[h]: (f(i)[1], 0)))
            args.append(tab)
    in_specs += [_const_spec(c.shape) for c in consts]
    args += list(consts)
    n_half = len(tok) + 1 + len(tables)

    def kern(*refs):
        const_refs = refs[PAIR * n_half:PAIR * n_half + len(consts)]
        out_refs = refs[PAIR * n_half + len(consts):]
        def store(h, refs_h, vals):
            for o_ref, r in zip(refs_h, vals):
                o_ref[h * TM:(h + 1) * TM, :] = r.astype(o_ref.dtype)

        pending = None
        for h in range(PAIR):
            hr = refs[h * n_half:(h + 1) * n_half]
            tile = index[h](pl.program_id(0))[1]
            res = body(tile, hr[:len(tok)], hr[len(tok)], hr[len(tok) + 1:], const_refs)
            finish = None
            if isinstance(res, tuple):
                res, finish = res
            store(h, out_refs, res)
            if pending is not None:
                store(h - 1, out_refs[len(res):], pending())
            pending = finish
        if pending is not None:
            store(PAIR - 1, out_refs[len(res):], pending())

    outs = pl.pallas_call(
        kern,
        grid=(n_tiles // PAIR,),
        in_specs=in_specs,
        out_specs=[pl.BlockSpec((PAIR * TM, w), lambda i: (i, 0)) for w in out_widths],
        out_shape=[jax.ShapeDtypeStruct((n_tiles * TM, w), dt) for w, dt in zip(out_widths, out_dtypes)],
        compiler_params=_cparams(("parallel",)),
        name=name,
    )(*args)
    return [o.reshape(bsz, -1, o.shape[-1]) for o in outs]


def _ffn_body(tile, tok, mod_ref, tables, consts, *, j, alpha, nct):
    wg_ref, wu_ref, wd_ref, lng_ref, lnb_ref = consts
    if len(tok) == 2:
        x = jnp.where(tile < nct, tok[0][...], tok[1][...])
    else:
        x = tok[0][...]
    h = (x * (1.0 + _mod_row(mod_ref, j, 1)) + _mod_row(mod_ref, j, 0)).astype(BF16)
    g = _dot(h, wg_ref[...])
    u = _dot(h, wu_ref[...])

    def finish():
        a = (_silu(g) * u).astype(BF16)
        y = _dot(a, wd_ref[...])
        r = alpha * x + (0.5 * _mod_row(mod_ref, j, 2)) * y
        return [_layer_norm(r, lng_ref[...], lnb_ref[...])]

    return [], finish


def _ffn(xa, mod, j, wg, wu, wd, lng, lnb, alpha, n_ctx_tiles, ctx=None, latent_only=False):
    d = xa.shape[-1]
    nct = n_ctx_tiles
    tok = [(xa, 's')] if ctx is None else [(ctx, 'c'), (xa, 'l')]
    nt = xa.shape[1] // TM + (0 if ctx is None else nct)
    return _token_call(
        functools.partial(_ffn_body, j=j, alpha=alpha, nct=nct), "ffn", tok, mod, [],
        [wg, wu, wd, lng.reshape(1, d), lnb.reshape(1, d)], [d], [F32], nt, nct, latent_only)[0]


def _rope(z, cos, sin):
    lane = lax.broadcasted_iota(jnp.int32, (1, LANES), 1)
    first = (lane % 64) < 32
    parts = []
    for k in range(z.shape[1] // LANES):
        zk = z[:, k * LANES:(k + 1) * LANES]
        partner = jnp.where(first, pltpu.roll(zk, LANES - 32, axis=1), pltpu.roll(zk, 32, axis=1))
        parts.append(zk * cos + partner * sin)
    return jnp.concatenate(parts, axis=-1)


def _inproj_body(tile, tok, mod_ref, tables, consts, *, outs, conv):
    w_ref = consts[0]
    modulate = lambda x: (x * (1.0 + _mod_row(mod_ref, 1, 1)) + _mod_row(mod_ref, 1, 0)).astype(BF16)
    h = modulate(tok[0][...])
    res = []
    for off, width, scale, rope in outs:
        z = _dot(h, w_ref[:, off:off + width])
        if rope:
            z = _rope(z, tables[0][...], tables[1][...])
        if scale != 1.0:
            z = z * scale
        res.append(z)
    if conv is None:
        return res
    off, nct, nt = conv
    cw_ref, cb_ref, wq_ref, wk_ref = consts[1:]
    wx = w_ref[:, off:off + HALF]
    has_prev = jnp.logical_and(tile != 0, tile != nct)
    has_next = jnp.logical_and(tile != nct - 1, tile != nt - 1)
    prev = jnp.where(has_prev, _dot(modulate(tok[1][...]), wx), 0.0)
    nxt = jnp.where(has_next, _dot(modulate(tok[2][...]), wx), 0.0)
    ext = jnp.concatenate([prev, _dot(h, wx), nxt], axis=0)

    def finish():
        acc = cb_ref[...]
        for j in range(ML_CONV):
            o = SUBLANES + j - ML_CONV // 2
            acc = acc + cw_ref[j:j + 1, :] * ext[o:o + TM, :]
        xc = _silu(acc).astype(BF16)
        heads = [slice(k * ML_HEAD_DIM, (k + 1) * ML_HEAD_DIM) for k in range(ML_HEADS)]
        q = jnp.concatenate([_dot(xc[:, hs], wq_ref[k]) for k, hs in enumerate(heads)], axis=-1)
        k = jnp.concatenate([_dot(xc[:, hs], wk_ref[k]) for k, hs in enumerate(heads)], axis=-1)
        return [q, k * ML_HEAD_DIM ** -0.5]

    return res, finish


def _inproj(xa, mod, w, outs, dtypes, n_ctx_tiles, rope_tabs=(), conv=None):
    nt = xa.shape[1] // TM
    tok, consts, widths, conv_static = [(xa, 's')], [w], [o[1] for o in outs], None
    if conv is not None:
        tok += [(xa, 'p'), (xa, 'n')]
        consts += list(conv[1:])
        widths += [HALF, HALF]
        conv_static = (conv[0], n_ctx_tiles, nt)
    return _token_call(
        functools.partial(_inproj_body, outs=tuple(outs), conv=conv_static), "inproj", tok, mod, list(rope_tabs),
        consts, widths, dtypes, nt, n_ctx_tiles)


NA_ROWS_PER_TILE = TM // GRID_W
NA_SLAB_ROWS = NA_WIN_H + NA_ROWS_PER_TILE


def _softmax_pv(s_list, v_list):
    m = s_list[0].max(axis=-1, keepdims=True)
    for s in s_list[1:]:
        m = jnp.maximum(m, s.max(axis=-1, keepdims=True))
    acc, den = None, None
    for s, v in zip(s_list, v_list):
        p = jnp.exp(s - m)
        l = p.sum(axis=-1, keepdims=True)
        o = _dot(p.astype(BF16), v)
        acc = o if acc is None else acc + o
        den = l if den is None else den + l
    return acc / den


def _na_kernel(q_ref, k_ref, v_ref, bias_ref, o_ref, *, n_ctx, rows, n_ctx_tiles):
    t = pl.program_id(1)
    n_loc = NA_SLAB_ROWS * GRID_W
    lane = lax.broadcasted_iota(jnp.int32, (1, LANES), 1)
    low = lane < NA_HEAD_DIM
    zero = jnp.zeros((), BF16)

    def attend(keys):
        def scores(n):
            p, e = divmod(n, 2)
            ls = slice(p * LANES, (p + 1) * LANES)
            qm = jnp.where(low if e == 0 else jnp.logical_not(low), q_ref[:, ls], zero)
            return [(_dot_nt(qm, k_ref[rows_k, ls]) if bias is None else _dot_nt(qm, k_ref[rows_k, ls]) + bias())
                    for rows_k, bias in keys(n)]

        def finish(n, s_list):
            p, e = divmod(n, 2)
            ls = slice(p * LANES, (p + 1) * LANES)
            return _softmax_pv(s_list, [v_ref[rows_k, ls] for rows_k, _ in keys(n)])

        outs = {}
        s_next = scores(0)
        for n in range(NA_HEADS):
            s_cur = s_next
            if n + 1 < NA_HEADS:
                s_next = scores(n + 1)
            outs[n] = finish(n, s_cur)
            if n % 2 == 1:
                ls = slice((n // 2) * LANES, (n // 2 + 1) * LANES)
                o_ref[:, ls] = jnp.where(low, outs[n - 1], outs[n]).astype(o_ref.dtype)

    @pl.when(t < n_ctx_tiles)
    def _ctx():
        attend(lambda n: [(slice(0, n_ctx), None)])

    @pl.when(t >= n_ctx_tiles)
    def _latent():
        r0 = (t - n_ctx_tiles) * NA_ROWS_PER_TILE
        rs0 = jnp.clip(r0 - NA_WIN_H // 2, 0, rows - NA_SLAB_ROWS)
        var = (r0 - rs0) // NA_ROWS_PER_TILE
        start = pl.multiple_of(n_ctx + rs0 * GRID_W, GRID_W)
        attend(lambda n: [(pl.ds(start, n_loc), lambda: bias_ref[var, n]), (slice(0, n_ctx), None)])


def _na_bias_table(rpb, rows):
    rpt, slab = NA_ROWS_PER_TILE, NA_SLAB_ROWS
    cols = jnp.arange(GRID_W)
    cs = jnp.clip(cols - NA_WIN_W // 2, 0, GRID_W - NA_WIN_W)
    kc = jnp.arange(GRID_W)
    in_win = (kc[None, :] >= cs[:, None]) & (kc[None, :] < cs[:, None] + NA_WIN_W)
    col_off = kc[None, :] - cols[:, None] + NA_WIN_W - 1
    diff = (jnp.arange(3) * rpt)[:, None, None]
    i = jnp.arange(rpt)[None, :, None]
    a = jnp.arange(slab)[None, None, :]
    w0 = jnp.clip(diff + i - NA_WIN_H // 2, 0, slab - NA_WIN_H)
    row_ok = (a >= w0) & (a < w0 + NA_WIN_H)
    row_off = a - diff - i + NA_WIN_H - 1
    row_sel = ((row_off[..., None] == jnp.arange(2 * NA_WIN_H - 1)) & row_ok[..., None]).astype(F32)
    col_sel = (col_off[:, :, None] == jnp.arange(2 * NA_WIN_W - 1)).astype(F32)
    tab = jnp.einsum('hrc,viar,qkc->vhiqak', rpb.astype(F32), row_sel, col_sel, precision=lax.Precision.HIGHEST)
    ok = row_ok[:, None, :, None, :, None] & in_win[None, None, None, :, None, :]
    tab = jnp.where(ok, tab, NEG)
    return tab.reshape(3, NA_HEADS, rpt * GRID_W, slab * GRID_W)


def _na(q, k, v, rpb, n_ctx):
    bsz, t, w = q.shape
    rows = (t - n_ctx) // GRID_W
    assert rows >= NA_SLAB_ROWS and rows % NA_ROWS_PER_TILE == 0
    bias = _na_bias_table(rpb, rows)
    return pl.pallas_call(
        functools.partial(_na_kernel, n_ctx=n_ctx, rows=rows, n_ctx_tiles=n_ctx // TM),
        grid=(bsz, t // TM),
        in_specs=[_tok_spec(w),
                  pl.BlockSpec((None, t, w), lambda b, i: (b, 0, 0)),
                  pl.BlockSpec((None, t, w), lambda b, i: (b, 0, 0)),
                  _const_spec(bias.shape)],
        out_specs=_tok_spec(w),
        out_shape=jax.ShapeDtypeStruct((bsz, t, w), BF16),
        compiler_params=_cparams(("parallel", "arbitrary")),
        name="natten",
    )(q, k, v, bias)


def _scan_specs(width, n_chunks, n_ctx_chunks, nb):
    fwd = pl.BlockSpec((nb, CHUNK, width), lambda b, i: (b, i, 0))
    bwd = pl.BlockSpec(
        (nb, CHUNK, width),
        lambda b, i: (b, jnp.where(i < n_ctx_chunks, n_ctx_chunks - 1 - i, n_chunks + n_ctx_chunks - 1 - i), 0))
    return fwd, bwd


def _split3(x):
    hi = x.astype(BF16)
    r1 = x - hi.astype(F32)
    mid = r1.astype(BF16)
    lo = (r1 - mid.astype(F32)).astype(BF16)
    return hi, mid, lo


def _mlscan_kernel(qf_ref, kf_ref, vf_ref, gf_ref, qb_ref, kb_ref, vb_ref, gb_ref, bias_ref,
                   of_ref, ob_ref, st_ref, m_ref):
    i = pl.program_id(1)
    dk = ML_HEAD_DIM

    @pl.when(i == 0)
    def _init():
        st_ref[...] = jnp.zeros_like(st_ref)
        m_ref[...] = jnp.zeros_like(m_ref)

    tt = lax.broadcasted_iota(jnp.int32, (CHUNK, CHUNK), 0)
    ss = lax.broadcasted_iota(jnp.int32, (CHUNK, CHUNK), 1)
    lane = lax.broadcasted_iota(jnp.int32, (CHUNK, LANES), 1)
    row = lax.broadcasted_iota(jnp.int32, (CHUNK, LANES), 0)
    ones_col = (lane == 0).astype(BF16)

    def running_max(x, reverse):
        sh = 1
        while sh < CHUNK:
            if reverse:
                shifted = jnp.where(row < CHUNK - sh, pltpu.roll(x, CHUNK - sh, axis=0), -jnp.inf)
            else:
                shifted = jnp.where(row >= sh, pltpu.roll(x, sh, axis=0), -jnp.inf)
            x = jnp.maximum(x, shifted)
            sh *= 2
        return x

    n_chain = 2 * ML_HEADS
    for bi, d in [(bi, d) for bi in range(ML_NB) for d in range(2)]:
        q_ref, k_ref, v_ref, g_ref, o_ref = ((qf_ref, kf_ref, vf_ref, gf_ref, of_ref),
                                             (qb_ref, kb_ref, vb_ref, gb_ref, ob_ref))[d]
        causal = (ss <= tt) if d == 0 else (ss >= tt)
        tri = causal.astype(BF16)
        gates = g_ref[bi] + bias_ref[...]
        hi, mid, lo = _split3(_log_sigmoid(gates[:, LANES:]))
        bc = _dot(tri, hi) + _dot(tri, mid) + _dot(tri, lo)
        e = gates[:, :LANES] - bc
        e_t = e.T
        m_row = bi * 2 + d
        m_prev = m_ref[m_row:m_row + 1, :]
        g = jnp.maximum(running_max(e, reverse=d == 1), m_prev)
        w_inter = jnp.exp(m_prev - g)
        exp_neg_mt = jnp.exp(-(bc + g))
        last = CHUNK - 1 if d == 0 else 0
        b_last = bc[last:last + 1, :]
        w_log = b_last + e
        m_new = jnp.maximum(b_last + m_prev, w_log.max(axis=0, keepdims=True))
        w_in = jnp.exp(w_log - m_new)
        decay = jnp.exp(b_last + m_prev - m_new)
        m_ref[m_row:m_row + 1, :] = m_new
        for h in range(ML_HEADS):
            c = d * ML_HEADS + h
            sc = bi * n_chain + c
            hs = slice(h * dk, (h + 1) * dk)
            q, k = q_ref[bi, :, hs], k_ref[bi, :, hs]
            v_aug = jnp.concatenate([v_ref[bi, :, hs], ones_col], axis=-1)
            p = jnp.exp(jnp.where(causal, e_t[c:c + 1, :] - g[:, c:c + 1], -jnp.inf))
            s = _dot_nt(q, k) * p
            state = st_ref[sc]
            res = _dot(s.astype(BF16), v_aug) + w_inter[:, c:c + 1] * _dot(q, state.astype(BF16))
            inv = 1.0 / jnp.maximum(jnp.abs(res[:, dk:dk + 1]), exp_neg_mt[:, c:c + 1])
            o_ref[bi, :, hs] = res[:, :dk] * inv
            st_ref[sc] = decay[:, c:c + 1] * state + _dot_tn(k, (v_aug * w_in[:, c:c + 1]).astype(BF16))


def _mlscan(q, k, v, gates, gate_bias, n_ctx):
    bsz, t, w = q.shape
    nc, ncc = t // CHUNK, n_ctx // CHUNK
    f, bk = _scan_specs(w, nc, ncc, ML_NB)
    gf, gb = _scan_specs(2 * LANES, nc, ncc, ML_NB)
    n_chain = 2 * ML_HEADS
    assert bsz % ML_NB == 0 and 2 * ML_NB <= SUBLANES
    return pl.pallas_call(
        _mlscan_kernel,
        grid=(bsz // ML_NB, nc),
        in_specs=[f, f, f, gf, bk, bk, bk, gb, _const_spec((1, 2 * LANES))],
        out_specs=[f, bk],
        out_shape=[jax.ShapeDtypeStruct((bsz, t, w), F32)] * 2,
        scratch_shapes=[pltpu.VMEM((ML_NB * n_chain, ML_HEAD_DIM, 2 * ML_HEAD_DIM), F32),
                        pltpu.VMEM((SUBLANES, LANES), F32)],
        compiler_params=_cparams(("parallel", "arbitrary")),
        name="mlstm_scan",
    )(q, k, v, gates, q, k, v, gates, gate_bias)


def _retscan_kernel(qf_ref, kf_ref, vf_ref, qb_ref, kb_ref, vb_ref, logit_ref, of_ref, ob_ref,
                    st_ref, dmat_ref, dec_ref):
    i = pl.program_id(1)
    dk = RET_HEAD_DIM
    n_chain = 2 * RET_HEADS

    @pl.when(i == 0)
    def _init():
        st_ref[...] = jnp.zeros_like(st_ref)
        tt = lax.broadcasted_iota(jnp.int32, (CHUNK, CHUNK), 0)
        ss = lax.broadcasted_iota(jnp.int32, (CHUNK, CHUNK), 1)
        pos = lax.broadcasted_iota(jnp.int32, (CHUNK, LANES), 0).astype(F32)
        log_g = _log_sigmoid(logit_ref[...])
        for d in range(2):
            dist = (tt - ss) if d == 0 else (ss - tt)
            distf = jnp.maximum(dist, 0).astype(F32)
            step = pos if d == 0 else (CHUNK - 1.0) - pos
            for h in range(RET_HEADS):
                c = d * RET_HEADS + h
                dmat_ref[c] = jnp.where(dist >= 0, jnp.exp(distf * log_g[c:c + 1, 0:1]), 0.0)
                lg = log_g[c:c + 1, :]
                dec_ref[c, 0] = jnp.exp((step + 1.0) * lg)
                dec_ref[c, 1] = jnp.exp((CHUNK - 1.0 - step) * lg)
                dec_ref[c, 2] = jnp.broadcast_to(jnp.exp(CHUNK * lg), (CHUNK, LANES))

    refs = ((qf_ref, kf_ref, vf_ref, of_ref), (qb_ref, kb_ref, vb_ref, ob_ref))
    chains = [(bi, d, h) for bi in range(RET_NB) for d in range(2) for h in range(RET_HEADS)]

    def front(bi, d, h):
        q_ref, k_ref = refs[d][:2]
        hs = slice(h * dk, (h + 1) * dk)
        state = st_ref[bi * n_chain + d * RET_HEADS + h]
        q = q_ref[bi, :, hs]
        return _dot_nt(q, k_ref[bi, :, hs]), _dot(q, state.astype(BF16)), state

    ahead = front(*chains[0])
    for n, (bi, d, h) in enumerate(chains):
        qk, qs, state = ahead
        if n + 1 < len(chains):
            ahead = front(*chains[n + 1])
        _, k_ref, v_ref, o_ref = refs[d]
        c = d * RET_HEADS + h
        hs = slice(h * dk, (h + 1) * dk)
        v = v_ref[bi, :, hs]
        o_ref[bi, :, hs] = _dot((qk * dmat_ref[c]).astype(BF16), v) + dec_ref[c, 0] * qs
        st_ref[bi * n_chain + c] = (dec_ref[c, 2, 0:dk, :] * state
                                    + _dot_tn(k_ref[bi, :, hs], (v.astype(F32) * dec_ref[c, 1]).astype(BF16)))


def _retscan(q, k, v, decay_logit, n_ctx):
    bsz, t, w = q.shape
    nc, ncc = t // CHUNK, n_ctx // CHUNK
    f, bk = _scan_specs(w, nc, ncc, RET_NB)
    logit = jnp.broadcast_to(decay_logit.astype(F32).reshape(2 * RET_HEADS, 1), (2 * RET_HEADS, LANES))
    assert bsz % RET_NB == 0 and RET_HEAD_DIM == LANES
    n_chain = 2 * RET_HEADS
    return pl.pallas_call(
        _retscan_kernel,
        grid=(bsz // RET_NB, nc),
        in_specs=[f, f, f, bk, bk, bk, _const_spec((n_chain, LANES))],
        out_specs=[f, bk],
        out_shape=[jax.ShapeDtypeStruct((bsz, t, w), F32)] * 2,
        scratch_shapes=[pltpu.VMEM((RET_NB * n_chain, RET_HEAD_DIM, RET_HEAD_DIM), F32),
                        pltpu.VMEM((n_chain, CHUNK, CHUNK), F32),
                        pltpu.VMEM((n_chain, 3, CHUNK, LANES), F32)],
        compiler_params=_cparams(("parallel", "arbitrary")),
        name="retention_scan",
    )(q, k, v, q, k, v, logit)


def _s5_weights(lam_re, lam_im, log_dt, b_re, b_im, c_re, c_im):
    hp = lax.Precision.HIGHEST
    n = S5_SUB
    dt = jnp.exp(log_dt)[..., None]
    zr, zi = lam_re * dt, lam_im * dt
    steps = jnp.arange(n + 1, dtype=F32)[:, None, None, None]
    pmag = jnp.exp(steps * zr)
    ak_re, ak_im = pmag * jnp.cos(steps * zi), pmag * jnp.sin(steps * zi)
    a_re, a_im = ak_re[1], ak_im[1]
    lam_sq = jnp.square(lam_re) + jnp.square(lam_im)
    e_re = ((a_re - 1.0) * lam_re + a_im * lam_im) / lam_sq
    e_im = (a_im * lam_re - (a_re - 1.0) * lam_im) / lam_sq
    bb_re = e_re[..., None] * b_re - e_im[..., None] * b_im
    bb_im = e_re[..., None] * b_im + e_im[..., None] * b_re
    ab_re = ak_re[..., None] * bb_re - ak_im[..., None] * bb_im
    ab_im = ak_re[..., None] * bb_im + ak_im[..., None] * bb_re
    kern = (jnp.einsum('dgqp,kdgpr->kdgqr', c_re, ab_re, precision=hp)
            - jnp.einsum('dgqp,kdgpr->kdgqr', c_im, ab_im, precision=hp))
    sig = jnp.arange(n)
    zeros = jnp.zeros_like(kern[:n])
    toe = jnp.stack([jnp.concatenate([zeros[:sp], kern[:n - sp]], axis=0) for sp in range(n)], axis=0)
    toe = toe.transpose(2, 3, 0, 5, 1, 4)
    vin_re = ab_re[n - 1 - sig].transpose(1, 2, 0, 4, 3)
    vin_im = ab_im[n - 1 - sig].transpose(1, 2, 0, 4, 3)
    ap_re, ap_im = ak_re[1:], ak_im[1:]
    w_re = c_re[None] * ap_re[:, :, :, None, :] - c_im[None] * ap_im[:, :, :, None, :]
    w_im = -(c_re[None] * ap_im[:, :, :, None, :] + c_im[None] * ap_re[:, :, :, None, :])
    w_re = w_re.transpose(1, 2, 4, 0, 3)
    w_im = w_im.transpose(1, 2, 4, 0, 3)

    def orient(a, axes):
        return jnp.stack([a[0], jnp.flip(a[1], axis=tuple(x - 1 for x in axes))], axis=0)

    toe = orient(toe, (2, 4))
    vin = orient(jnp.stack([vin_re, vin_im], axis=4), (2,))
    wout = orient(jnp.stack([w_re, w_im], axis=2), (4,))

    no = lam_re.shape[1] // S5_OCT
    ns = S5_OCT * S5_STATE

    def compact(a):
        a = a.reshape((2, no, S5_OCT) + a.shape[2:4] + (LANES,))
        return a.transpose(0, 1, 3, 2, 4, 5).reshape(2, no, -1, LANES).astype(BF16)

    t_c = compact(toe.reshape(toe.shape[:4] + (LANES,)))
    v_c = compact(vin.reshape(vin.shape[:4] + (LANES,)))
    w_c = compact(wout.reshape(wout.shape[:4] + (LANES,)))
    a_re = ak_re[n].reshape(2, no, 1, ns)
    a_im = ak_im[n].reshape(2, no, 1, ns)
    return t_c, v_c, w_c, a_re, a_im


def _s5_expand(c_ref, o_ref, row_blk, col_blk):
    n_rows, n_cols = o_ref.shape[1:]
    ent = lax.broadcasted_iota(jnp.int32, (LANES, n_cols), 0)
    col = lax.broadcasted_iota(jnp.int32, (LANES, n_cols), 1)
    spread = (ent == (col // (S5_OCT * col_blk)) * col_blk + col % col_blk).astype(BF16)
    rg = (lax.broadcasted_iota(jnp.int32, (n_rows, n_cols), 0) // row_blk) % S5_OCT
    cg = (lax.broadcasted_iota(jnp.int32, (n_rows, n_cols), 1) // col_blk) % S5_OCT
    for d in range(2):
        o_ref[d] = jnp.where(rg == cg, _dot(c_ref[d], spread), 0.0).astype(BF16)


def _s5_kernel(uf_ref, ub_ref, tc_ref, vc_ref, wc_ref, are_ref, aim_ref, yf_ref, yb_ref,
               t_ref, v_ref, w_ref, in_ref, xp_ref, st_ref, *, bsz, jt):
    ns = S5_OCT * S5_STATE
    rows = bsz * jt

    @pl.when(pl.program_id(1) == 0)
    def _init():
        st_ref[...] = jnp.zeros_like(st_ref)
        _s5_expand(tc_ref, t_ref, S5_GROUP, S5_GROUP)
        _s5_expand(vc_ref, v_ref, S5_GROUP, S5_STATE)
        _s5_expand(wc_ref, w_ref, S5_STATE, S5_GROUP)

    ri = lax.broadcasted_iota(jnp.int32, (rows, rows), 0)
    ci = lax.broadcasted_iota(jnp.int32, (rows, rows), 1)
    perm = lambda inner: (ci == (ri % inner) * (rows // inner) + ri // inner).astype(BF16)
    p_sj = perm(jt)
    p_jb = perm(bsz)
    p_bj = perm(jt)

    for d, (u_ref, y_ref) in enumerate(((uf_ref, yf_ref), (ub_ref, yb_ref))):
        u_rows = []
        for b in range(bsz):
            z = _dot(p_sj, u_ref[b].astype(BF16)).astype(BF16)
            u_rows.append(jnp.concatenate([z[s * jt:(s + 1) * jt] for s in range(S5_SUB)], axis=1))
        u = jnp.concatenate(u_rows, axis=0)
        blk = 2 * LANES
        y_intra = jnp.concatenate(
            [_dot(u[:, :(n + 1) * blk], t_ref[d, :(n + 1) * blk, n * blk:(n + 1) * blk]) if d == 0 else
             _dot(u[:, n * blk:], t_ref[d, n * blk:, n * blk:(n + 1) * blk])
             for n in range(S5_SUB * LANES // blk)], axis=1)
        u_jb = _dot(p_jb, u).astype(BF16)
        in_ref[d] = _dot(u_jb, v_ref[d])
        a_re, a_im = are_ref[d], aim_ref[d]
        xr, xi = st_ref[d, 0], st_ref[d, 1]
        for jj in range(jt):
            j = jj if d == 0 else jt - 1 - jj
            rows_j = slice(j * bsz, (j + 1) * bsz)
            xp_ref[d, rows_j, :] = jnp.concatenate([xr, xi], axis=1)
            inc = in_ref[d, rows_j, :]
            xr, xi = (a_re * xr - a_im * xi + inc[:, :ns], a_re * xi + a_im * xr + inc[:, ns:])
        st_ref[d, 0] = xr
        st_ref[d, 1] = xi
        xp = _dot(p_bj, xp_ref[d].astype(BF16)).astype(BF16)
        y = y_intra + _dot(xp, w_ref[d])
        for b in range(bsz):
            for s in range(S5_SUB):
                y_ref[b, pl.ds(s, jt, stride=S5_SUB), :] = y[b * jt:(b + 1) * jt, s * LANES:(s + 1) * LANES]


def _s5(u, t_c, v_c, w_c, a_re, a_im, n_ctx):
    bsz, t, width = u.shape
    assert bsz == S5_SUB
    no = width // LANES
    nt, nct = t // TM, n_ctx // TM
    jt = TM // S5_SUB
    nq, ns = S5_SUB * LANES, S5_OCT * S5_STATE
    fwd = pl.BlockSpec((bsz, TM, LANES), lambda o, i: (0, i, o))
    bwd = pl.BlockSpec((bsz, TM, LANES), lambda o, i: (0, jnp.where(i < nct, nct - 1 - i, nt + nct - 1 - i), o))
    op = lambda r, c: pl.BlockSpec((2, None, r, c), lambda o, i: (0, o, 0, 0))
    return pl.pallas_call(
        functools.partial(_s5_kernel, bsz=bsz, jt=jt),
        grid=(no, nt),
        in_specs=[fwd, bwd, op(nq, LANES), op(nq, LANES), op(2 * ns, LANES), op(1, ns), op(1, ns)],
        out_specs=[fwd, bwd],
        out_shape=[jax.ShapeDtypeStruct((bsz, t, width), F32)] * 2,
        scratch_shapes=[pltpu.VMEM((2, nq, nq), BF16), pltpu.VMEM((2, nq, 2 * ns), BF16),
                        pltpu.VMEM((2, 2 * ns, nq), BF16),
                        pltpu.VMEM((2, bsz * jt, 2 * ns), F32), pltpu.VMEM((2, bsz * jt, 2 * ns), F32),
                        pltpu.VMEM((2, 2, bsz, ns), F32)],
        compiler_params=_cparams(("parallel", "arbitrary")),
        name="s5",
    )(u, u, t_c, v_c, w_c, a_re, a_im)


def _out_even_body(tile, tok, mod_ref, tables, consts, *, alpha):
    x_ref, na_ref, hf_ref, hb_ref, zo_ref = tok
    gn_ref, w_ref, lng_ref, lnb_ref = consts
    ml = _sigmoid(zo_ref[...]) * _head_norm(hf_ref[...] + hb_ref[...], gn_ref[...], ML_HEADS, ML_HEAD_DIM)
    y = _dot(na_ref[...].astype(BF16), w_ref[0:HALF, :]) + _dot(ml.astype(BF16), w_ref[HALF:, :])
    r = alpha * x_ref[...] + _mod_row(mod_ref, 1, 2) * y
    return [_layer_norm(r, lng_ref[...], lnb_ref[...])]


def _out_odd_body(tile, tok, mod_ref, tables, consts, *, alpha):
    x_ref, ysf_ref, ysb_ref, u_ref, rf_ref, rb_ref, gr_ref = tok
    dsk_ref, gw_ref, gb_ref, gn_ref, w_ref, lng_ref, lnb_ref = consts
    s = _gelu_tanh(ysf_ref[...] + ysb_ref[...] + dsk_ref[...] * u_ref[...])
    s = s * _sigmoid(_dot(s.astype(BF16), gw_ref[...]) + gb_ref[...])
    ret = _silu(gr_ref[...]) * _head_norm(rf_ref[...] + rb_ref[...], gn_ref[...], RET_HEADS, RET_HEAD_DIM)
    y = _dot(s.astype(BF16), w_ref[0:HALF, :]) + _dot(ret.astype(BF16), w_ref[HALF:, :])
    r = alpha * x_ref[...] + _mod_row(mod_ref, 1, 2) * y
    return [_layer_norm(r, lng_ref[...], lnb_ref[...])]


def _out_call(body, name, xa, mod, tok, consts, alpha, n_ctx_tiles):
    d = xa.shape[-1]
    return _token_call(functools.partial(body, alpha=alpha), name, [(xa, 's')] + [(a, 's') for a in tok], mod,
                       [], consts, [d], [F32], xa.shape[1] // TM, n_ctx_tiles)[0]


def _rope_tables(n_ctx, seq):
    nf = RET_HEAD_DIM // 4
    freqs = ROPE_BASE ** (-jnp.arange(nf, dtype=F32) / nf)
    tok = jnp.arange(seq)
    ang_r = (tok // GRID_W).astype(F32)[:, None] * freqs
    ang_c = (tok % GRID_W).astype(F32)[:, None] * freqs
    cos = jnp.concatenate([jnp.cos(ang_r)] * 2 + [jnp.cos(ang_c)] * 2, axis=-1)
    sin = jnp.concatenate([-jnp.sin(ang_r), jnp.sin(ang_r), -jnp.sin(ang_c), jnp.sin(ang_c)], axis=-1)
    cos = jnp.concatenate([jnp.ones((n_ctx, LANES), F32), cos], axis=0)
    sin = jnp.concatenate([jnp.zeros((n_ctx, LANES), F32), sin], axis=0)
    return cos, sin


def _even_layer_mixer(xa, mod, w_in, w_out, rpb, conv_w, conv_b, wq, wk, i_bias, f_bias, gn_w,
                      lng, lnb, alpha, n_ctx):
    d = xa.shape[-1]
    n_ctx_tiles = n_ctx // TM
    ng = 2 * ML_HEADS
    main = 6 * HALF
    w_gates = jnp.zeros((d, 2 * LANES), F32)
    w_gates = w_gates.at[:, 0:ng].set(w_in[:, main:main + ng]).at[:, LANES:LANES + ng].set(w_in[:, main + ng:])
    w_all = jnp.concatenate([w_in[:, :main], w_gates], axis=1).astype(BF16)
    gate_bias = jnp.zeros((1, 2 * LANES), F32)
    gate_bias = gate_bias.at[0, 0:ng].set(i_bias.reshape(ng)).at[0, LANES:LANES + ng].set(f_bias.reshape(ng))
    outs = [(0, HALF, NA_HEAD_DIM ** -0.5, False), (HALF, HALF, 1.0, False), (2 * HALF, HALF, 1.0, False),
            (4 * HALF, HALF, 1.0, False), (5 * HALF, HALF, 1.0, False), (main, 2 * LANES, 1.0, False)]
    dts = [BF16, BF16, BF16, BF16, F32, F32, BF16, BF16]
    conv = (3 * HALF, conv_w, conv_b.reshape(1, HALF), wq.astype(BF16), wk.astype(BF16))
    q_na, k_na, v_na, zv, zo, gates, q_ml, k_ml = _inproj(xa, mod, w_all, outs, dts, n_ctx_tiles, conv=conv)
    na = _na(q_na, k_na, v_na, rpb, n_ctx)
    hf, hb = _mlscan(q_ml, k_ml, zv, gates, gate_bias, n_ctx)
    return _out_call(
        _out_even_body, "out_even", xa, mod, [na, hf, hb, zo],
        [gn_w.reshape(1, HALF), w_out.astype(BF16), lng.reshape(1, d), lnb.reshape(1, d)], alpha, n_ctx_tiles)


def _odd_layer_mixer(xa, mod, w_in, w_out, s5_params, d_skip, glu_w, glu_b, decay_logit, gn_w,
                     lng, lnb, alpha, n_ctx, rope_tabs):
    d = xa.shape[-1]
    n_ctx_tiles = n_ctx // TM
    outs = [(0, HALF, 1.0, False), (HALF, HALF, 1.0, True), (2 * HALF, HALF, RET_HEAD_DIM ** -0.5, True),
            (3 * HALF, HALF, 1.0, False), (4 * HALF, HALF, 1.0, False)]
    dts = [F32, BF16, BF16, BF16, F32]
    u, q_r, k_r, v_r, g_r = _inproj(xa, mod, w_in.astype(BF16), outs, dts, n_ctx_tiles, rope_tabs)
    ysf, ysb = _s5(u, *_s5_weights(*s5_params), n_ctx)
    rf, rb = _retscan(q_r, k_r, v_r, decay_logit, n_ctx)
    return _out_call(
        _out_odd_body, "out_odd", xa, mod, [ysf, ysb, u, rf, rb, g_r],
        [d_skip.reshape(1, HALF), glu_w.astype(BF16), glu_b.reshape(1, HALF), gn_w.reshape(1, HALF),
         w_out.astype(BF16), lng.reshape(1, d), lnb.reshape(1, d)], alpha, n_ctx_tiles)


def kernel(x, c, ctx, c_ctx, ada_w, ada_b, ffn_w_gate, ffn_w_up, ffn_w_down, ln_g, ln_b, ev_w_in, ev_w_out, na_rpb, ml_conv_w, ml_conv_b, ml_wq, ml_wk, ml_i_bias, ml_f_bias, ml_gn_w, od_w_in, od_w_out, s5_lam_re, s5_lam_im, s5_log_dt, s5_b_re, s5_b_im, s5_c_re, s5_c_im, s5_d, s5_glu_w, s5_glu_b, ret_decay_logit, ret_gn_w):
    bsz, seq, d = x.shape
    n_ctx = ctx.shape[1]
    depth = ada_w.shape[0]
    assert n_ctx % TM == 0 and seq % TM == 0 and seq % GRID_W == 0 and bsz == SUBLANES
    n_ctx_tiles = n_ctx // TM
    alpha = (2.0 * depth) ** 0.25

    rows = bsz + SUBLANES
    c_rows = jnp.zeros((rows, d), F32).at[:bsz].set(c).at[bsz].set(c_ctx)
    mod_all = _mod_table(c_rows, ada_w, ada_b)

    xa = x
    rope_tabs = _rope_tables(n_ctx, seq)
    for l in range(depth):
        e = l // 2
        mod = mod_all[l]
        ffn = lambda j, k, h, **kw: _ffn(h, mod, j, ffn_w_gate[l, k].astype(BF16), ffn_w_up[l, k].astype(BF16),
                                         ffn_w_down[l, k].astype(BF16), ln_g[l, j], ln_b[l, j], alpha,
                                         n_ctx_tiles, **kw)
        xa = ffn(0, 0, xa, ctx=ctx) if l == 0 else ffn(0, 0, xa)
        if l % 2 == 0:
            xa = _even_layer_mixer(xa, mod, ev_w_in[e], ev_w_out[e], na_rpb[e], ml_conv_w[e], ml_conv_b[e],
                                   ml_wq[e], ml_wk[e], ml_i_bias[e], ml_f_bias[e], ml_gn_w[e],
                                   ln_g[l, 1], ln_b[l, 1], alpha, n_ctx)
        else:
            s5_params = (s5_lam_re[e], s5_lam_im[e], s5_log_dt[e], s5_b_re[e], s5_b_im[e], s5_c_re[e], s5_c_im[e])
            xa = _odd_layer_mixer(xa, mod, od_w_in[e], od_w_out[e], s5_params, s5_d[e], s5_glu_w[e], s5_glu_b[e],
                                  ret_decay_logit[e], ret_gn_w[e], ln_g[l, 1], ln_b[l, 1], alpha, n_ctx, rope_tabs)
        xa = ffn(2, 1, xa, latent_only=(l == depth - 1))
    return xa
```

```python
import functools
import math

import jax
import jax.numpy as jnp
from jax import lax
from jax.experimental import pallas as pl
from jax.experimental.pallas import tpu as pltpu

F32 = jnp.float32
BF16 = jnp.bfloat16

GRID_W = 64
LN_EPS = 1e-5
N_MOD = 9
NA_HEADS, NA_HEAD_DIM, NA_WIN_H, NA_WIN_W = 8, 64, 8, 16
ML_HEADS, ML_HEAD_DIM, ML_CONV = 4, 128, 5
S5_GROUP, S5_GROUPS, S5_STATE = 16, 32, 64
RET_HEADS, RET_HEAD_DIM = 4, 128
ROPE_BASE = 10000.0
HALF = 512

LANES = 128
SUBLANES = 8
VMEM_LIMIT = 56 * 1024 * 1024

TM = 256
CHUNK = 256
ML_NB, RET_NB = 1, 2
S5_SUB = 8
S5_OCT = LANES // S5_GROUP
NEG = -1e30


def _cparams(sem):
    return pltpu.CompilerParams(dimension_semantics=sem, vmem_limit_bytes=VMEM_LIMIT)


def _const_spec(shape):
    nd = len(shape)
    return pl.BlockSpec(shape, lambda *_: (0,) * nd, pipeline_mode=pl.Buffered(1))


def _dot(a, b):
    return jnp.dot(a, b, preferred_element_type=F32)


def _dot_nt(a, b):
    return lax.dot_general(a, b, (((1,), (1,)), ((), ())), preferred_element_type=F32)


def _dot_tn(a, b):
    return lax.dot_general(a, b, (((0,), (0,)), ((), ())), preferred_element_type=F32)


def _sigmoid(x):
    return 1.0 / (1.0 + jnp.exp(-x))


def _silu(x):
    return x * _sigmoid(x)


def _log_sigmoid(x):
    return jnp.minimum(x, 0.0) - jnp.log1p(jnp.exp(-jnp.abs(x)))


def _gelu_tanh(x):
    return 0.5 * x * (1.0 + jnp.tanh(math.sqrt(2.0 / math.pi) * (x + 0.044715 * (x * x * x))))


def _layer_norm(r, g, b):
    mu = jnp.mean(r, axis=-1, keepdims=True)
    c = r - mu
    var = jnp.mean(c * c, axis=-1, keepdims=True)
    return c * lax.rsqrt(var + LN_EPS) * g + b


def _head_norm(h, w, n_heads, head_dim):
    parts = []
    for k in range(n_heads):
        hk = h[:, k * head_dim:(k + 1) * head_dim]
        mu = jnp.mean(hk, axis=-1, keepdims=True)
        c = hk - mu
        var = jnp.mean(c * c, axis=-1, keepdims=True)
        parts.append(c * lax.rsqrt(var + LN_EPS))
    return jnp.concatenate(parts, axis=-1) * w


def _mod_row(mod_ref, j, k):
    i = 3 * j + k
    return mod_ref[i:i + 1, :]


def _mod_kernel(c_ref, w_ref, b_ref, o_ref):
    c = c_ref[...]
    s = _silu(c).astype(BF16)
    o_ref[...] = _dot(s, w_ref[...].astype(BF16)) + b_ref[...]


def _mod_table(c_rows, ada_w, ada_b):
    depth, d, n = ada_w.shape
    rows = c_rows.shape[0]
    tn = n // 4 if n % (4 * LANES) == 0 else LANES
    out = pl.pallas_call(
        _mod_kernel,
        grid=(depth, n // tn),
        in_specs=[pl.BlockSpec((rows, d), lambda l, j: (0, 0)),
                  pl.BlockSpec((None, d, tn), lambda l, j: (l, 0, j)),
                  pl.BlockSpec((None, 1, tn), lambda l, j: (l, 0, j))],
        out_specs=pl.BlockSpec((None, rows, tn), lambda l, j: (l, 0, j)),
        out_shape=jax.ShapeDtypeStruct((depth, rows, n), F32),
        compiler_params=_cparams(("parallel", "parallel")),
        name="mod_table",
    )(c_rows, ada_w, ada_b.reshape(depth, 1, n))
    return out.reshape(depth, rows, N_MOD, d)


PAIR = 4


def _tok_spec(width, tm=TM):
    return pl.BlockSpec((None, tm, width), lambda b, t: (b, t, 0))


def _tile_index(h, nt, nct, bsz, latent_only):
    if latent_only:
        per = (nt - nct) // PAIR

        def index(i):
            b = i // per
            return b, nct + PAIR * (i % per) + h, b
    else:
        def index(i):
            g = PAIR * i + h
            b, t = g // nt, g % nt
            return b, t, jnp.where(t < nct, bsz, b)
    return index


def _token_call(body, name, tok, mod, tables, consts, out_widths, out_dtypes, nt, nct, latent_only=False):
    bsz, d = tok[0][0].shape[0], mod.shape[-1]
    n_tiles = bsz * ((nt - nct) if latent_only else nt)
    assert n_tiles % PAIR == 0 and (not latent_only or (nt - nct) % PAIR == 0)
    index = [_tile_index(h, nt, nct, bsz, latent_only) for h in range(PAIR)]
    per = TM // SUBLANES
    last = nt * per - 1
    tile_of = {'s': lambda t: t, 'c': lambda t: jnp.minimum(t, nct - 1), 'l': lambda t: jnp.maximum(t - nct, 0),
               'p': lambda t: jnp.maximum(t * per - 1, 0), 'n': lambda t: jnp.minimum((t + 1) * per, last)}
    in_specs, args = [], []
    for h in range(PAIR):
        for arr, kind in tok:
            in_specs.append(pl.BlockSpec(
                (None, SUBLANES if kind in 'pn' else TM, arr.shape[-1]),
                lambda i, f=index[h], k=tile_of[kind]: (f(i)[0], k(f(i)[1]), 0)))
            args.append(arr)
        in_specs.append(pl.BlockSpec((None, N_MOD, d), lambda i, f=index[h]: (f(i)[2], 0, 0)))
        args.append(mod)
        for tab in tables:
            in_specs.append(pl.BlockSpec((TM, tab.shape[-1]), lambda i, f=index[h]: (f(i)[1], 0)))
            args.append(tab)
    in_specs += [_const_spec(c.shape) for c in consts]
    args += list(consts)
    n_half = len(tok) + 1 + len(tables)

    def kern(*refs):
        const_refs = refs[PAIR * n_half:PAIR * n_half + len(consts)]
        out_refs = refs[PAIR * n_half + len(consts):]
        def store(h, refs_h, vals):
            for o_ref, r in zip(refs_h, vals):
                o_ref[h * TM:(h + 1) * TM, :] = r.astype(o_ref.dtype)

        pending = None
        for h in range(PAIR):
            hr = refs[h * n_half:(h + 1) * n_half]
            tile = index[h](pl.program_id(0))[1]
            res = body(tile, hr[:len(tok)], hr[len(tok)], hr[len(tok) + 1:], const_refs)
            finish = None
            if isinstance(res, tuple):
                res, finish = res
            store(h, out_refs, res)
            if pending is not None:
                store(h - 1, out_refs[len(res):], pending())
            pending = finish
        if pending is not None:
            store(PAIR - 1, out_refs[len(res):], pending())

    outs = pl.pallas_call(
        kern,
        grid=(n_tiles // PAIR,),
        in_specs=in_specs,
        out_specs=[pl.BlockSpec((PAIR * TM, w), lambda i: (i, 0)) for w in out_widths],
        out_shape=[jax.ShapeDtypeStruct((n_tiles * TM, w), dt) for w, dt in zip(out_widths, out_dtypes)],
        compiler_params=_cparams(("parallel",)),
        name=name,
    )(*args)
    return [o.reshape(bsz, -1, o.shape[-1]) for o in outs]


def _ffn_body(tile, tok, mod_ref, tables, consts, *, j, alpha, nct):
    wg_ref, wu_ref, wd_ref, lng_ref, lnb_ref = consts
    if len(tok) == 2:
        x = jnp.where(tile < nct, tok[0][...], tok[1][...])
    else:
        x = tok[0][...]
    h = (x * (1.0 + _mod_row(mod_ref, j, 1)) + _mod_row(mod_ref, j, 0)).astype(BF16)
    g = _dot(h, wg_ref[...])
    u = _dot(h, wu_ref[...])

    def finish():
        a = (_silu(g) * u).astype(BF16)
        y = _dot(a, wd_ref[...])
        r = alpha * x + (0.5 * _mod_row(mod_ref, j, 2)) * y
        return [_layer_norm(r, lng_ref[...], lnb_ref[...])]

    return [], finish


def _ffn(xa, mod, j, wg, wu, wd, lng, lnb, alpha, n_ctx_tiles, ctx=None, latent_only=False):
    d = xa.shape[-1]
    nct = n_ctx_tiles
    tok = [(xa, 's')] if ctx is None else [(ctx, 'c'), (xa, 'l')]
    nt = xa.shape[1] // TM + (0 if ctx is None else nct)
    return _token_call(
        functools.partial(_ffn_body, j=j, alpha=alpha, nct=nct), "ffn", tok, mod, [],
        [wg, wu, wd, lng.reshape(1, d), lnb.reshape(1, d)], [d], [F32], nt, nct, latent_only)[0]


def _rope(z, cos, sin):
    lane = lax.broadcasted_iota(jnp.int32, (1, LANES), 1)
    first = (lane % 64) < 32
    parts = []
    for k in range(z.shape[1] // LANES):
        zk = z[:, k * LANES:(k + 1) * LANES]
        partner = jnp.where(first, pltpu.roll(zk, LANES - 32, axis=1), pltpu.roll(zk, 32, axis=1))
        parts.append(zk * cos + partner * sin)
    return jnp.concatenate(parts, axis=-1)


def _inproj_body(tile, tok, mod_ref, tables, consts, *, outs, conv):
    w_ref = consts[0]
    modulate = lambda x: (x * (1.0 + _mod_row(mod_ref, 1, 1)) + _mod_row(mod_ref, 1, 0)).astype(BF16)
    h = modulate(tok[0][...])
    res = []
    for off, width, scale, rope in outs:
        z = _dot(h, w_ref[:, off:off + width])
        if rope:
            z = _rope(z, tables[0][...], tables[1][...])
        if scale != 1.0:
            z = z * scale
        res.append(z)
    if conv is None:
        return res
    off, nct, nt = conv
    cw_ref, cb_ref, wq_ref, wk_ref = consts[1:]
    wx = w_ref[:, off:off + HALF]
    has_prev = jnp.logical_and(tile != 0, tile != nct)
    has_next = jnp.logical_and(tile != nct - 1, tile != nt - 1)
    prev = jnp.where(has_prev, _dot(modulate(tok[1][...]), wx), 0.0)
    nxt = jnp.where(has_next, _dot(modulate(tok[2][...]), wx), 0.0)
    ext = jnp.concatenate([prev, _dot(h, wx), nxt], axis=0)

    def finish():
        acc = cb_ref[...]
        for j in range(ML_CONV):
            o = SUBLANES + j - ML_CONV // 2
            acc = acc + cw_ref[j:j + 1, :] * ext[o:o + TM, :]
        xc = _silu(acc).astype(BF16)
        heads = [slice(k * ML_HEAD_DIM, (k + 1) * ML_HEAD_DIM) for k in range(ML_HEADS)]
        q = jnp.concatenate([_dot(xc[:, hs], wq_ref[k]) for k, hs in enumerate(heads)], axis=-1)
        k = jnp.concatenate([_dot(xc[:, hs], wk_ref[k]) for k, hs in enumerate(heads)], axis=-1)
        return [q, k * ML_HEAD_DIM ** -0.5]

    return res, finish


def _inproj(xa, mod, w, outs, dtypes, n_ctx_tiles, rope_tabs=(), conv=None):
    nt = xa.shape[1] // TM
    tok, consts, widths, conv_static = [(xa, 's')], [w], [o[1] for o in outs], None
    if conv is not None:
        tok += [(xa, 'p'), (xa, 'n')]
        consts += list(conv[1:])
        widths += [HALF, HALF]
        conv_static = (conv[0], n_ctx_tiles, nt)
    return _token_call(
        functools.partial(_inproj_body, outs=tuple(outs), conv=conv_static), "inproj", tok, mod, list(rope_tabs),
        consts, widths, dtypes, nt, n_ctx_tiles)


NA_ROWS_PER_TILE = TM // GRID_W
NA_SLAB_ROWS = NA_WIN_H + NA_ROWS_PER_TILE


def _softmax_pv(s_list, v_list):
    m = s_list[0].max(axis=-1, keepdims=True)
    for s in s_list[1:]:
        m = jnp.maximum(m, s.max(axis=-1, keepdims=True))
    acc, den = None, None
    for s, v in zip(s_list, v_list):
        p = jnp.exp(s - m)
        l = p.sum(axis=-1, keepdims=True)
        o = _dot(p.astype(BF16), v)
        acc = o if acc is None else acc + o
        den = l if den is None else den + l
    return acc / den


def _na_kernel(q_ref, k_ref, v_ref, bias_ref, o_ref, *, n_ctx, rows, n_ctx_tiles):
    t = pl.program_id(1)
    n_loc = NA_SLAB_ROWS * GRID_W
    lane = lax.broadcasted_iota(jnp.int32, (1, LANES), 1)
    low = lane < NA_HEAD_DIM
    zero = jnp.zeros((), BF16)

    def attend(keys):
        def scores(n):
            p, e = divmod(n, 2)
            ls = slice(p * LANES, (p + 1) * LANES)
            qm = jnp.where(low if e == 0 else jnp.logical_not(low), q_ref[:, ls], zero)
            return [(_dot_nt(qm, k_ref[rows_k, ls]) if bias is None else _dot_nt(qm, k_ref[rows_k, ls]) + bias())
                    for rows_k, bias in keys(n)]

        def finish(n, s_list):
            p, e = divmod(n, 2)
            ls = slice(p * LANES, (p + 1) * LANES)
            return _softmax_pv(s_list, [v_ref[rows_k, ls] for rows_k, _ in keys(n)])

        outs = {}
        s_next = scores(0)
        for n in range(NA_HEADS):
            s_cur = s_next
            if n + 1 < NA_HEADS:
                s_next = scores(n + 1)
            outs[n] = finish(n, s_cur)
            if n % 2 == 1:
                ls = slice((n // 2) * LANES, (n // 2 + 1) * LANES)
                o_ref[:, ls] = jnp.where(low, outs[n - 1], outs[n]).astype(o_ref.dtype)

    @pl.when(t < n_ctx_tiles)
    def _ctx():
        attend(lambda n: [(slice(0, n_ctx), None)])

    @pl.when(t >= n_ctx_tiles)
    def _latent():
        r0 = (t - n_ctx_tiles) * NA_ROWS_PER_TILE
        rs0 = jnp.clip(r0 - NA_WIN_H // 2, 0, rows - NA_SLAB_ROWS)
        var = (r0 - rs0) // NA_ROWS_PER_TILE
        start = pl.multiple_of(n_ctx + rs0 * GRID_W, GRID_W)
        attend(lambda n: [(pl.ds(start, n_loc), lambda: bias_ref[var, n]), (slice(0, n_ctx), None)])


def _na_bias_table(rpb, rows):
    rpt, slab = NA_ROWS_PER_TILE, NA_SLAB_ROWS
    cols = jnp.arange(GRID_W)
    cs = jnp.clip(cols - NA_WIN_W // 2, 0, GRID_W - NA_WIN_W)
    kc = jnp.arange(GRID_W)
    in_win = (kc[None, :] >= cs[:, None]) & (kc[None, :] < cs[:, None] + NA_WIN_W)
    col_off = kc[None, :] - cols[:, None] + NA_WIN_W - 1
    diff = (jnp.arange(3) * rpt)[:, None, None]
    i = jnp.arange(rpt)[None, :, None]
    a = jnp.arange(slab)[None, None, :]
    w0 = jnp.clip(diff + i - NA_WIN_H // 2, 0, slab - NA_WIN_H)
    row_ok = (a >= w0) & (a < w0 + NA_WIN_H)
    row_off = a - diff - i + NA_WIN_H - 1
    row_sel = ((row_off[..., None] == jnp.arange(2 * NA_WIN_H - 1)) & row_ok[..., None]).astype(F32)
    col_sel = (col_off[:, :, None] == jnp.arange(2 * NA_WIN_W - 1)).astype(F32)
    tab = jnp.einsum('hrc,viar,qkc->vhiqak', rpb.astype(F32), row_sel, col_sel, precision=lax.Precision.HIGHEST)
    ok = row_ok[:, None, :, None, :, None] & in_win[None, None, None, :, None, :]
    tab = jnp.where(ok, tab, NEG)
    return tab.reshape(3, NA_HEADS, rpt * GRID_W, slab * GRID_W)


def _na(q, k, v, rpb, n_ctx):
    bsz, t, w = q.shape
    rows = (t - n_ctx) // GRID_W
    assert rows >= NA_SLAB_ROWS and rows % NA_ROWS_PER_TILE == 0
    bias = _na_bias_table(rpb, rows)
    return pl.pallas_call(
        functools.partial(_na_kernel, n_ctx=n_ctx, rows=rows, n_ctx_tiles=n_ctx // TM),
        grid=(bsz, t // TM),
        in_specs=[_tok_spec(w),
                  pl.BlockSpec((None, t, w), lambda b, i: (b, 0, 0)),
                  pl.BlockSpec((None, t, w), lambda b, i: (b, 0, 0)),
                  _const_spec(bias.shape)],
        out_specs=_tok_spec(w),
        out_shape=jax.ShapeDtypeStruct((bsz, t, w), BF16),
        compiler_params=_cparams(("parallel", "arbitrary")),
        name="natten",
    )(q, k, v, bias)


def _scan_specs(width, n_chunks, n_ctx_chunks, nb):
    fwd = pl.BlockSpec((nb, CHUNK, width), lambda b, i: (b, i, 0))
    bwd = pl.BlockSpec(
        (nb, CHUNK, width),
        lambda b, i: (b, jnp.where(i < n_ctx_chunks, n_ctx_chunks - 1 - i, n_chunks + n_ctx_chunks - 1 - i), 0))
    return fwd, bwd


def _split3(x):
    hi = x.astype(BF16)
    r1 = x - hi.astype(F32)
    mid = r1.astype(BF16)
    lo = (r1 - mid.astype(F32)).astype(BF16)
    return hi, mid, lo


def _mlscan_kernel(qf_ref, kf_ref, vf_ref, gf_ref, qb_ref, kb_ref, vb_ref, gb_ref, bias_ref,
                   of_ref, ob_ref, st_ref, m_ref):
    i = pl.program_id(1)
    dk = ML_HEAD_DIM

    @pl.when(i == 0)
    def _init():
        st_ref[...] = jnp.zeros_like(st_ref)
        m_ref[...] = jnp.zeros_like(m_ref)

    tt = lax.broadcasted_iota(jnp.int32, (CHUNK, CHUNK), 0)
    ss = lax.broadcasted_iota(jnp.int32, (CHUNK, CHUNK), 1)
    lane = lax.broadcasted_iota(jnp.int32, (CHUNK, LANES), 1)
    row = lax.broadcasted_iota(jnp.int32, (CHUNK, LANES), 0)
    ones_col = (lane == 0).astype(BF16)

    def running_max(x, reverse):
        sh = 1
        while sh < CHUNK:
            if reverse:
                shifted = jnp.where(row < CHUNK - sh, pltpu.roll(x, CHUNK - sh, axis=0), -jnp.inf)
            else:
                shifted = jnp.where(row >= sh, pltpu.roll(x, sh, axis=0), -jnp.inf)
            x = jnp.maximum(x, shifted)
            sh *= 2
        return x

    n_chain = 2 * ML_HEADS
    for bi, d in [(bi, d) for bi in range(ML_NB) for d in range(2)]:
        q_ref, k_ref, v_ref, g_ref, o_ref = ((qf_ref, kf_ref, vf_ref, gf_ref, of_ref),
                                             (qb_ref, kb_ref, vb_ref, gb_ref, ob_ref))[d]
        causal = (ss <= tt) if d == 0 else (ss >= tt)
        tri = causal.astype(BF16)
        gates = g_ref[bi] + bias_ref[...]
        hi, mid, lo = _split3(_log_sigmoid(gates[:, LANES:]))
        bc = _dot(tri, hi) + _dot(tri, mid) + _dot(tri, lo)
        e = gates[:, :LANES] - bc
        e_t = e.T
        m_row = bi * 2 + d
        m_prev = m_ref[m_row:m_row + 1, :]
        g = jnp.maximum(running_max(e, reverse=d == 1), m_prev)
        w_inter = jnp.exp(m_prev - g)
        exp_neg_mt = jnp.exp(-(bc + g))
        last = CHUNK - 1 if d == 0 else 0
        b_last = bc[last:last + 1, :]
        w_log = b_last + e
        m_new = jnp.maximum(b_last + m_prev, w_log.max(axis=0, keepdims=True))
        w_in = jnp.exp(w_log - m_new)
        decay = jnp.exp(b_last + m_prev - m_new)
        m_ref[m_row:m_row + 1, :] = m_new
        for h in range(ML_HEADS):
            c = d * ML_HEADS + h
            sc = bi * n_chain + c
            hs = slice(h * dk, (h + 1) * dk)
            q, k = q_ref[bi, :, hs], k_ref[bi, :, hs]
            v_aug = jnp.concatenate([v_ref[bi, :, hs], ones_col], axis=-1)
            p = jnp.exp(jnp.where(causal, e_t[c:c + 1, :] - g[:, c:c + 1], -jnp.inf))
            s = _dot_nt(q, k) * p
            state = st_ref[sc]
            res = _dot(s.astype(BF16), v_aug) + w_inter[:, c:c + 1] * _dot(q, state.astype(BF16))
            inv = 1.0 / jnp.maximum(jnp.abs(res[:, dk:dk + 1]), exp_neg_mt[:, c:c + 1])
            o_ref[bi, :, hs] = res[:, :dk] * inv
            st_ref[sc] = decay[:, c:c + 1] * state + _dot_tn(k, (v_aug * w_in[:, c:c + 1]).astype(BF16))


def _mlscan(q, k, v, gates, gate_bias, n_ctx):
    bsz, t, w = q.shape
    nc, ncc = t // CHUNK, n_ctx // CHUNK
    f, bk = _scan_specs(w, nc, ncc, ML_NB)
    gf, gb = _scan_specs(2 * LANES, nc, ncc, ML_NB)
    n_chain = 2 * ML_HEADS
    assert bsz % ML_NB == 0 and 2 * ML_NB <= SUBLANES
    return pl.pallas_call(
        _mlscan_kernel,
        grid=(bsz // ML_NB, nc),
        in_specs=[f, f, f, gf, bk, bk, bk, gb, _const_spec((1, 2 * LANES))],
        out_specs=[f, bk],
        out_shape=[jax.ShapeDtypeStruct((bsz, t, w), F32)] * 2,
        scratch_shapes=[pltpu.VMEM((ML_NB * n_chain, ML_HEAD_DIM, 2 * ML_HEAD_DIM), F32),
                        pltpu.VMEM((SUBLANES, LANES), F32)],
        compiler_params=_cparams(("parallel", "arbitrary")),
        name="mlstm_scan",
    )(q, k, v, gates, q, k, v, gates, gate_bias)


def _retscan_kernel(qf_ref, kf_ref, vf_ref, qb_ref, kb_ref, vb_ref, logit_ref, of_ref, ob_ref,
                    st_ref, dmat_ref, dec_ref):
    i = pl.program_id(1)
    dk = RET_HEAD_DIM
    n_chain = 2 * RET_HEADS

    @pl.when(i == 0)
    def _init():
        st_ref[...] = jnp.zeros_like(st_ref)
        tt = lax.broadcasted_iota(jnp.int32, (CHUNK, CHUNK), 0)
        ss = lax.broadcasted_iota(jnp.int32, (CHUNK, CHUNK), 1)
        pos = lax.broadcasted_iota(jnp.int32, (CHUNK, LANES), 0).astype(F32)
        log_g = _log_sigmoid(logit_ref[...])
        for d in range(2):
            dist = (tt - ss) if d == 0 else (ss - tt)
            distf = jnp.maximum(dist, 0).astype(F32)
            step = pos if d == 0 else (CHUNK - 1.0) - pos
            for h in range(RET_HEADS):
                c = d * RET_HEADS + h
                dmat_ref[c] = jnp.where(dist >= 0, jnp.exp(distf * log_g[c:c + 1, 0:1]), 0.0)
                lg = log_g[c:c + 1, :]
                dec_ref[c, 0] = jnp.exp((step + 1.0) * lg)
                dec_ref[c, 1] = jnp.exp((CHUNK - 1.0 - step) * lg)
                dec_ref[c, 2] = jnp.broadcast_to(jnp.exp(CHUNK * lg), (CHUNK, LANES))

    refs = ((qf_ref, kf_ref, vf_ref, of_ref), (qb_ref, kb_ref, vb_ref, ob_ref))
    chains = [(bi, d, h) for bi in range(RET_NB) for d in range(2) for h in range(RET_HEADS)]

    def front(bi, d, h):
        q_ref, k_ref = refs[d][:2]
        hs = slice(h * dk, (h + 1) * dk)
        state = st_ref[bi * n_chain + d * RET_HEADS + h]
        q = q_ref[bi, :, hs]
        return _dot_nt(q, k_ref[bi, :, hs]), _dot(q, state.astype(BF16)), state

    ahead = front(*chains[0])
    for n, (bi, d, h) in enumerate(chains):
        qk, qs, state = ahead
        if n + 1 < len(chains):
            ahead = front(*chains[n + 1])
        _, k_ref, v_ref, o_ref = refs[d]
        c = d * RET_HEADS + h
        hs = slice(h * dk, (h + 1) * dk)
        v = v_ref[bi, :, hs]
        o_ref[bi, :, hs] = _dot((qk * dmat_ref[c]).astype(BF16), v) + dec_ref[c, 0] * qs
        st_ref[bi * n_chain + c] = (dec_ref[c, 2, 0:dk, :] * state
                                    + _dot_tn(k_ref[bi, :, hs], (v.astype(F32) * dec_ref[c, 1]).astype(BF16)))


def _retscan(q, k, v, decay_logit, n_ctx):
    bsz, t, w = q.shape
    nc, ncc = t // CHUNK, n_ctx // CHUNK
    f, bk = _scan_specs(w, nc, ncc, RET_NB)
    logit = jnp.broadcast_to(decay_logit.astype(F32).reshape(2 * RET_HEADS, 1), (2 * RET_HEADS, LANES))
    assert bsz % RET_NB == 0 and RET_HEAD_DIM == LANES
    n_chain = 2 * RET_HEADS
    return pl.pallas_call(
        _retscan_kernel,
        grid=(bsz // RET_NB, nc),
        in_specs=[f, f, f, bk, bk, bk, _const_spec((n_chain, LANES))],
        out_specs=[f, bk],
        out_shape=[jax.ShapeDtypeStruct((bsz, t, w), F32)] * 2,
        scratch_shapes=[pltpu.VMEM((RET_NB * n_chain, RET_HEAD_DIM, RET_HEAD_DIM), F32),
                        pltpu.VMEM((n_chain, CHUNK, CHUNK), F32),
                        pltpu.VMEM((n_chain, 3, CHUNK, LANES), F32)],
        compiler_params=_cparams(("parallel", "arbitrary")),
        name="retention_scan",
    )(q, k, v, q, k, v, logit)


def _s5_weights(lam_re, lam_im, log_dt, b_re, b_im, c_re, c_im):
    hp = lax.Precision.HIGHEST
    n = S5_SUB
    dt = jnp.exp(log_dt)[..., None]
    zr, zi = lam_re * dt, lam_im * dt
    steps = jnp.arange(n + 1, dtype=F32)[:, None, None, None]
    pmag = jnp.exp(steps * zr)
    ak_re, ak_im = pmag * jnp.cos(steps * zi), pmag * jnp.sin(steps * zi)
    a_re, a_im = ak_re[1], ak_im[1]
    lam_sq = jnp.square(lam_re) + jnp.square(lam_im)
    e_re = ((a_re - 1.0) * lam_re + a_im * lam_im) / lam_sq
    e_im = (a_im * lam_re - (a_re - 1.0) * lam_im) / lam_sq
    bb_re = e_re[..., None] * b_re - e_im[..., None] * b_im
    bb_im = e_re[..., None] * b_im + e_im[..., None] * b_re
    ab_re = ak_re[..., None] * bb_re - ak_im[..., None] * bb_im
    ab_im = ak_re[..., None] * bb_im + ak_im[..., None] * bb_re
    kern = (jnp.einsum('dgqp,kdgpr->kdgqr', c_re, ab_re, precision=hp)
            - jnp.einsum('dgqp,kdgpr->kdgqr', c_im, ab_im, precision=hp))
    sig = jnp.arange(n)
    zeros = jnp.zeros_like(kern[:n])
    toe = jnp.stack([jnp.concatenate([zeros[:sp], kern[:n - sp]], axis=0) for sp in range(n)], axis=0)
    toe = toe.transpose(2, 3, 0, 5, 1, 4)
    vin_re = ab_re[n - 1 - sig].transpose(1, 2, 0, 4, 3)
    vin_im = ab_im[n - 1 - sig].transpose(1, 2, 0, 4, 3)
    ap_re, ap_im = ak_re[1:], ak_im[1:]
    w_re = c_re[None] * ap_re[:, :, :, None, :] - c_im[None] * ap_im[:, :, :, None, :]
    w_im = -(c_re[None] * ap_im[:, :, :, None, :] + c_im[None] * ap_re[:, :, :, None, :])
    w_re = w_re.transpose(1, 2, 4, 0, 3)
    w_im = w_im.transpose(1, 2, 4, 0, 3)

    def orient(a, axes):
        return jnp.stack([a[0], jnp.flip(a[1], axis=tuple(x - 1 for x in axes))], axis=0)

    toe = orient(toe, (2, 4))
    vin = orient(jnp.stack([vin_re, vin_im], axis=4), (2,))
    wout = orient(jnp.stack([w_re, w_im], axis=2), (4,))

    no = lam_re.shape[1] // S5_OCT
    ns = S5_OCT * S5_STATE

    def compact(a):
        a = a.reshape((2, no, S5_OCT) + a.shape[2:4] + (LANES,))
        return a.transpose(0, 1, 3, 2, 4, 5).reshape(2, no, -1, LANES).astype(BF16)

    t_c = compact(toe.reshape(toe.shape[:4] + (LANES,)))
    v_c = compact(vin.reshape(vin.shape[:4] + (LANES,)))
    w_c = compact(wout.reshape(wout.shape[:4] + (LANES,)))
    a_re = ak_re[n].reshape(2, no, 1, ns)
    a_im = ak_im[n].reshape(2, no, 1, ns)
    return t_c, v_c, w_c, a_re, a_im


def _s5_expand(c_ref, o_ref, row_blk, col_blk):
    n_rows, n_cols = o_ref.shape[1:]
    ent = lax.broadcasted_iota(jnp.int32, (LANES, n_cols), 0)
    col = lax.broadcasted_iota(jnp.int32, (LANES, n_cols), 1)
    spread = (ent == (col // (S5_OCT * col_blk)) * col_blk + col % col_blk).astype(BF16)
    rg = (lax.broadcasted_iota(jnp.int32, (n_rows, n_cols), 0) // row_blk) % S5_OCT
    cg = (lax.broadcasted_iota(jnp.int32, (n_rows, n_cols), 1) // col_blk) % S5_OCT
    for d in range(2):
        o_ref[d] = jnp.where(rg == cg, _dot(c_ref[d], spread), 0.0).astype(BF16)


def _s5_kernel(uf_ref, ub_ref, tc_ref, vc_ref, wc_ref, are_ref, aim_ref, yf_ref, yb_ref,
               t_ref, v_ref, w_ref, in_ref, xp_ref, st_ref, *, bsz, jt):
    ns = S5_OCT * S5_STATE
    rows = bsz * jt

    @pl.when(pl.program_id(1) == 0)
    def _init():
        st_ref[...] = jnp.zeros_like(st_ref)
        _s5_expand(tc_ref, t_ref, S5_GROUP, S5_GROUP)
        _s5_expand(vc_ref, v_ref, S5_GROUP, S5_STATE)
        _s5_expand(wc_ref, w_ref, S5_STATE, S5_GROUP)

    ri = lax.broadcasted_iota(jnp.int32, (rows, rows), 0)
    ci = lax.broadcasted_iota(jnp.int32, (rows, rows), 1)
    perm = lambda inner: (ci == (ri % inner) * (rows // inner) + ri // inner).astype(BF16)
    p_sj = perm(jt)
    p_jb = perm(bsz)
    p_bj = perm(jt)

    y_intras = []
    for d, u_ref in enumerate((uf_ref, ub_ref)):
        u_rows = []
        for b in range(bsz):
            z = _dot(p_sj, u_ref[b].astype(BF16)).astype(BF16)
            u_rows.append(jnp.concatenate([z[s * jt:(s + 1) * jt] for s in range(S5_SUB)], axis=1))
        u = jnp.concatenate(u_rows, axis=0)
        blk = 2 * LANES
        y_intra = jnp.concatenate(
            [_dot(u[:, :(n + 1) * blk], t_ref[d, :(n + 1) * blk, n * blk:(n + 1) * blk]) if d == 0 else
             _dot(u[:, n * blk:], t_ref[d, n * blk:, n * blk:(n + 1) * blk])
             for n in range(S5_SUB * LANES // blk)], axis=1)
        u_jb = _dot(p_jb, u).astype(BF16)
        in_ref[d] = _dot(u_jb, v_ref[d])
        y_intras.append(y_intra)

    for d, y_ref in enumerate((yf_ref, yb_ref)):
        y_intra = y_intras[d]
        a_re, a_im = are_ref[d], aim_ref[d]
        xr, xi = st_ref[d, 0], st_ref[d, 1]
        for jj in range(jt):
            j = jj if d == 0 else jt - 1 - jj
            rows_j = slice(j * bsz, (j + 1) * bsz)
            xp_ref[d, rows_j, :] = jnp.concatenate([xr, xi], axis=1)
            inc = in_ref[d, rows_j, :]
            xr, xi = (a_re * xr - a_im * xi + inc[:, :ns], a_re * xi + a_im * xr + inc[:, ns:])
        st_ref[d, 0] = xr
        st_ref[d, 1] = xi
        xp = _dot(p_bj, xp_ref[d].astype(BF16)).astype(BF16)
        y = y_intra + _dot(xp, w_ref[d])
        for b in range(bsz):
            for s in range(S5_SUB):
                y_ref[b, pl.ds(s, jt, stride=S5_SUB), :] = y[b * jt:(b + 1) * jt, s * LANES:(s + 1) * LANES]


def _s5(u, t_c, v_c, w_c, a_re, a_im, n_ctx):
    bsz, t, width = u.shape
    assert bsz == S5_SUB
    no = width // LANES
    nt, nct = t // TM, n_ctx // TM
    jt = TM // S5_SUB
    nq, ns = S5_SUB * LANES, S5_OCT * S5_STATE
    fwd = pl.BlockSpec((bsz, TM, LANES), lambda o, i: (0, i, o))
    bwd = pl.BlockSpec((bsz, TM, LANES), lambda o, i: (0, jnp.where(i < nct, nct - 1 - i, nt + nct - 1 - i), o))
    op = lambda r, c: pl.BlockSpec((2, None, r, c), lambda o, i: (0, o, 0, 0))
    return pl.pallas_call(
        functools.partial(_s5_kernel, bsz=bsz, jt=jt),
        grid=(no, nt),
        in_specs=[fwd, bwd, op(nq, LANES), op(nq, LANES), op(2 * ns, LANES), op(1, ns), op(1, ns)],
        out_specs=[fwd, bwd],
        out_shape=[jax.ShapeDtypeStruct((bsz, t, width), F32)] * 2,
        scratch_shapes=[pltpu.VMEM((2, nq, nq), BF16), pltpu.VMEM((2, nq, 2 * ns), BF16),
                        pltpu.VMEM((2, 2 * ns, nq), BF16),
                        pltpu.VMEM((2, bsz * jt, 2 * ns), F32), pltpu.VMEM((2, bsz * jt, 2 * ns), F32),
                        pltpu.VMEM((2, 2, bsz, ns), F32)],
        compiler_params=_cparams(("parallel", "arbitrary")),
        name="s5",
    )(u, u, t_c, v_c, w_c, a_re, a_im)


def _out_even_body(tile, tok, mod_ref, tables, consts, *, alpha):
    x_ref, na_ref, hf_ref, hb_ref, zo_ref = tok
    gn_ref, w_ref, lng_ref, lnb_ref = consts
    ml = _sigmoid(zo_ref[...]) * _head_norm(hf_ref[...] + hb_ref[...], gn_ref[...], ML_HEADS, ML_HEAD_DIM)
    y = _dot(na_ref[...].astype(BF16), w_ref[0:HALF, :]) + _dot(ml.astype(BF16), w_ref[HALF:, :])
    r = alpha * x_ref[...] + _mod_row(mod_ref, 1, 2) * y
    return [_layer_norm(r, lng_ref[...], lnb_ref[...])]


def _out_odd_body(tile, tok, mod_ref, tables, consts, *, alpha):
    x_ref, ysf_ref, ysb_ref, u_ref, rf_ref, rb_ref, gr_ref = tok
    dsk_ref, gw_ref, gb_ref, gn_ref, w_ref, lng_ref, lnb_ref = consts
    s = _gelu_tanh(ysf_ref[...] + ysb_ref[...] + dsk_ref[...] * u_ref[...])
    s = s * _sigmoid(_dot(s.astype(BF16), gw_ref[...]) + gb_ref[...])
    ret = _silu(gr_ref[...]) * _head_norm(rf_ref[...] + rb_ref[...], gn_ref[...], RET_HEADS, RET_HEAD_DIM)
    y = _dot(s.astype(BF16), w_ref[0:HALF, :]) + _dot(ret.astype(BF16), w_ref[HALF:, :])
    r = alpha * x_ref[...] + _mod_row(mod_ref, 1, 2) * y
    return [_layer_norm(r, lng_ref[...], lnb_ref[...])]


def _out_call(body, name, xa, mod, tok, consts, alpha, n_ctx_tiles):
    d = xa.shape[-1]
    return _token_call(functools.partial(body, alpha=alpha), name, [(xa, 's')] + [(a, 's') for a in tok], mod,
                       [], consts, [d], [F32], xa.shape[1] // TM, n_ctx_tiles)[0]


def _rope_tables(n_ctx, seq):
    nf = RET_HEAD_DIM // 4
    freqs = ROPE_BASE ** (-jnp.arange(nf, dtype=F32) / nf)
    tok = jnp.arange(seq)
    ang_r = (tok // GRID_W).astype(F32)[:, None] * freqs
    ang_c = (tok % GRID_W).astype(F32)[:, None] * freqs
    cos = jnp.concatenate([jnp.cos(ang_r)] * 2 + [jnp.cos(ang_c)] * 2, axis=-1)
    sin = jnp.concatenate([-jnp.sin(ang_r), jnp.sin(ang_r), -jnp.sin(ang_c), jnp.sin(ang_c)], axis=-1)
    cos = jnp.concatenate([jnp.ones((n_ctx, LANES), F32), cos], axis=0)
    sin = jnp.concatenate([jnp.zeros((n_ctx, LANES), F32), sin], axis=0)
    return cos, sin


def _even_layer_mixer(xa, mod, w_in, w_out, rpb, conv_w, conv_b, wq, wk, i_bias, f_bias, gn_w,
                      lng, lnb, alpha, n_ctx):
    d = xa.shape[-1]
    n_ctx_tiles = n_ctx // TM
    ng = 2 * ML_HEADS
    main = 6 * HALF
    w_gates = jnp.zeros((d, 2 * LANES), F32)
    w_gates = w_gates.at[:, 0:ng].set(w_in[:, main:main + ng]).at[:, LANES:LANES + ng].set(w_in[:, main + ng:])
    w_all = jnp.concatenate([w_in[:, :main], w_gates], axis=1).astype(BF16)
    gate_bias = jnp.zeros((1, 2 * LANES), F32)
    gate_bias = gate_bias.at[0, 0:ng].set(i_bias.reshape(ng)).at[0, LANES:LANES + ng].set(f_bias.reshape(ng))
    outs = [(0, HALF, NA_HEAD_DIM ** -0.5, False), (HALF, HALF, 1.0, False), (2 * HALF, HALF, 1.0, False),
            (4 * HALF, HALF, 1.0, False), (5 * HALF, HALF, 1.0, False), (main, 2 * LANES, 1.0, False)]
    dts = [BF16, BF16, BF16, BF16, F32, F32, BF16, BF16]
    conv = (3 * HALF, conv_w, conv_b.reshape(1, HALF), wq.astype(BF16), wk.astype(BF16))
    q_na, k_na, v_na, zv, zo, gates, q_ml, k_ml = _inproj(xa, mod, w_all, outs, dts, n_ctx_tiles, conv=conv)
    na = _na(q_na, k_na, v_na, rpb, n_ctx)
    hf, hb = _mlscan(q_ml, k_ml, zv, gates, gate_bias, n_ctx)
    return _out_call(
        _out_even_body, "out_even", xa, mod, [na, hf, hb, zo],
        [gn_w.reshape(1, HALF), w_out.astype(BF16), lng.reshape(1, d), lnb.reshape(1, d)], alpha, n_ctx_tiles)


def _odd_layer_mixer(xa, mod, w_in, w_out, s5_params, d_skip, glu_w, glu_b, decay_logit, gn_w,
                     lng, lnb, alpha, n_ctx, rope_tabs):
    d = xa.shape[-1]
    n_ctx_tiles = n_ctx // TM
    outs = [(0, HALF, 1.0, False), (HALF, HALF, 1.0, True), (2 * HALF, HALF, RET_HEAD_DIM ** -0.5, True),
            (3 * HALF, HALF, 1.0, False), (4 * HALF, HALF, 1.0, False)]
    dts = [F32, BF16, BF16, BF16, F32]
    u, q_r, k_r, v_r, g_r = _inproj(xa, mod, w_in.astype(BF16), outs, dts, n_ctx_tiles, rope_tabs)
    ysf, ysb = _s5(u, *_s5_weights(*s5_params), n_ctx)
    rf, rb = _retscan(q_r, k_r, v_r, decay_logit, n_ctx)
    return _out_call(
        _out_odd_body, "out_odd", xa, mod, [ysf, ysb, u, rf, rb, g_r],
        [d_skip.reshape(1, HALF), glu_w.astype(BF16), glu_b.reshape(1, HALF), gn_w.reshape(1, HALF),
         w_out.astype(BF16), lng.reshape(1, d), lnb.reshape(1, d)], alpha, n_ctx_tiles)


def kernel(x, c, ctx, c_ctx, ada_w, ada_b, ffn_w_gate, ffn_w_up, ffn_w_down, ln_g, ln_b, ev_w_in, ev_w_out, na_rpb, ml_conv_w, ml_conv_b, ml_wq, ml_wk, ml_i_bias, ml_f_bias, ml_gn_w, od_w_in, od_w_out, s5_lam_re, s5_lam_im, s5_log_dt, s5_b_re, s5_b_im, s5_c_re, s5_c_im, s5_d, s5_glu_w, s5_glu_b, ret_decay_logit, ret_gn_w):
    bsz, seq, d = x.shape
    n_ctx = ctx.shape[1]
    depth = ada_w.shape[0]
    assert n_ctx % TM == 0 and seq % TM == 0 and seq % GRID_W == 0 and bsz == SUBLANES
    n_ctx_tiles = n_ctx // TM
    alpha = (2.0 * depth) ** 0.25

    rows = bsz + SUBLANES
    c_rows = jnp.zeros((rows, d), F32).at[:bsz].set(c).at[bsz].set(c_ctx)
    mod_all = _mod_table(c_rows, ada_w, ada_b)

    xa = x
    rope_tabs = _rope_tables(n_ctx, seq)
    for l in range(depth):
        e = l // 2
        mod = mod_all[l]
        ffn = lambda j, k, h, **kw: _ffn(h, mod, j, ffn_w_gate[l, k].astype(BF16), ffn_w_up[l, k].astype(BF16),
                                         ffn_w_down[l, k].astype(BF16), ln_g[l, j], ln_b[l, j], alpha,
                                         n_ctx_tiles, **kw)
        xa = ffn(0, 0, xa, ctx=ctx) if l == 0 else ffn(0, 0, xa)
        if l % 2 == 0:
            xa = _even_layer_mixer(xa, mod, ev_w_in[e], ev_w_out[e], na_rpb[e], ml_conv_w[e], ml_conv_b[e],
                                   ml_wq[e], ml_wk[e], ml_i_bias[e], ml_f_bias[e], ml_gn_w[e],
                                   ln_g[l, 1], ln_b[l, 1], alpha, n_ctx)
        else:
            s5_params = (s5_lam_re[e], s5_lam_im[e], s5_log_dt[e], s5_b_re[e], s5_b_im[e], s5_c_re[e], s5_c_im[e])
            xa = _odd_layer_mixer(xa, mod, od_w_in[e], od_w_out[e], s5_params, s5_d[e], s5_glu_w[e], s5_glu_b[e],
                                  ret_decay_logit[e], ret_gn_w[e], ln_g[l, 1], ln_b[l, 1], alpha, n_ctx, rope_tabs)
        xa = ffn(2, 1, xa, latent_only=(l == depth - 1))
    return xa
```
